```python
import jax
import jax.numpy as jnp
from jax import lax
import numpy as np

D_MODEL = 2048
BATCH = 2
SEQ = 4096
DEPTH = 4
DEC_BATCH = 8
DEC_SEQ = 1
PAST_LEN = 16384
PAGE_SIZE = 128

HEAD_DIM = 64
N_Q_HEADS = 8
N_KV_HEADS = 2
GQA_GROUP = N_Q_HEADS // N_KV_HEADS
MIX_W = N_Q_HEADS * HEAD_DIM
KV_W = N_KV_HEADS * HEAD_DIM
N_BRANCH = 4
Q_BLOCK = 128
ROPE_THETA = 10000.0
LN_EPS = 1e-5
NSA_BLOCK = 64
NSA_TOPK = 16
NSA_WINDOW = 512
IDX_HEADS = 4
IDX_DIM = 64
DSA_TOPK_MAX = 256
FOX_BIAS_INIT = 3.0
N_MEM = 256
MEM_HEADS = 4
MEM_HEAD_DIM = 128
MEM_W = MEM_HEADS * MEM_HEAD_DIM
N_GROUPS = 4
EXPERTS_PER_GROUP = 4
N_EXPERTS = N_GROUPS * EXPERTS_PER_GROUP
TOP_K_IN_GROUP = 2
D_FF_EXPERT = 256

IN_SPLITS = (
    ('sb_q', MIX_W), ('sb_k', KV_W), ('sb_v', KV_W),
    ('nsa_q', MIX_W), ('nsa_kc', KV_W), ('nsa_vc', KV_W), ('nsa_ks', KV_W), ('nsa_vs', KV_W),
    ('nsa_kw', KV_W), ('nsa_vw', KV_W), ('nsa_g', 3 * N_Q_HEADS),
    ('dsa_q', MIX_W), ('dsa_k', KV_W), ('dsa_v', KV_W),
    ('idx_q', IDX_HEADS * IDX_DIM), ('idx_k', IDX_DIM), ('idx_w', IDX_HEADS),
    ('fox_q', MIX_W), ('fox_k', KV_W), ('fox_v', KV_W), ('fox_f', N_Q_HEADS),
)
N_IN = sum(width for _, width in IN_SPLITS)

kernel_name = 'hybrid_sb_nsa_dsa_fox_decoder_step'


def split_columns(h):
    parts, off = {}, 0
    for name, width in IN_SPLITS:
        parts[name] = h[..., off:off + width]
        off += width
    return parts


def layer_norm(x, g, b):
    xf = x.astype(jnp.float32)
    mu = jnp.mean(xf, axis=-1, keepdims=True)
    var = jnp.mean(jnp.square(xf - mu), axis=-1, keepdims=True)
    y = (xf - mu) * lax.rsqrt(var + LN_EPS) * g.astype(jnp.float32) + b.astype(jnp.float32)
    return y.astype(x.dtype)


def rope(x, pos):
    half = x.shape[-1] // 2
    freq = ROPE_THETA ** (-jnp.arange(half, dtype=jnp.float32) / half)
    ang = pos.astype(jnp.float32)[:, None] * freq[None, :]
    cos = jnp.cos(ang)[None, :, None, :].astype(x.dtype)
    sin = jnp.sin(ang)[None, :, None, :].astype(x.dtype)
    x1, x2 = x[..., :half], x[..., half:]
    return jnp.concatenate([x1 * cos - x2 * sin, x2 * cos + x1 * sin], axis=-1)


def masked_softmax(s, mask):
    s = jnp.where(mask, s.astype(jnp.float32), -jnp.inf)
    m = jnp.max(s, axis=-1, keepdims=True)
    m = jnp.where(jnp.isfinite(m), m, 0.0)
    e = jnp.exp(s - m)
    return e / jnp.maximum(jnp.sum(e, axis=-1, keepdims=True), 1e-30)


def qblock_sweep(fn, T, *q_inputs):
    qb = min(Q_BLOCK, T)
    nb = -(-T // qb)
    pad = nb * qb - T

    def split(a):
        a = jnp.pad(a, [(0, 0), (0, pad)] + [(0, 0)] * (a.ndim - 2))
        return jnp.swapaxes(a.reshape(a.shape[0], nb, qb, *a.shape[2:]), 0, 1)

    blocks = tuple(split(a) for a in q_inputs)
    out = lax.map(lambda args: fn(args[0], *args[1]), (jnp.arange(nb), blocks))
    out = jnp.swapaxes(out, 0, 1)
    return out.reshape(out.shape[0], nb * qb, *out.shape[3:])[:, :T]


def gather_pages(cache_l, page_table):
    g = cache_l[page_table]
    return g.reshape(g.shape[0], g.shape[1] * g.shape[2], *g.shape[3:])


def stick_breaking_attention(q, k_all, v_all, P):
    B, T = q.shape[:2]
    L = k_all.shape[1]
    qb = min(Q_BLOCK, T)
    kpos = jnp.arange(L)
    scale = HEAD_DIM ** -0.5

    def block(j, qj):
        qpos = P + j * qb + jnp.arange(qb)
        qg = qj.reshape(B, qb, N_KV_HEADS, GQA_GROUP, HEAD_DIM)
        z = jnp.einsum('btgrd,bsgd->bgrts', qg, k_all).astype(jnp.float32) * scale
        causal = kpos[None, :] < qpos[:, None]
        log_not = jnp.where(causal, jax.nn.log_sigmoid(-z), 0.0)
        between = lax.cumsum(log_not, axis=4, reverse=True) - log_not
        a = jnp.where(causal, jnp.exp(jax.nn.log_sigmoid(z) + between), 0.0)
        o = jnp.einsum('bgrts,bsgd->btgrd', a.astype(v_all.dtype), v_all)
        return o.reshape(B, qb, MIX_W)

    return qblock_sweep(block, T, q)


def nsa_attention(q, gates, kv_all, win_rows, pe_k, pe_v, w_ck, w_cv, P):
    B, T = q.shape[:2]
    L = kv_all.shape[1]
    qb = min(Q_BLOCK, T)
    nq = -(-T // qb)
    nblk = -(-L // NSA_BLOCK)
    k_sel = min(NSA_TOPK, nblk)
    dt = kv_all.dtype
    scale = HEAD_DIM ** -0.5
    kv_blk = jnp.pad(kv_all, [(0, 0), (0, nblk * NSA_BLOCK - L), (0, 0), (0, 0), (0, 0)])
    kv_blk = kv_blk.reshape(B, nblk, NSA_BLOCK, 4, N_KV_HEADS, HEAD_DIM)
    cmp_k = jnp.einsum('bngd,de->bnge', jnp.mean(kv_blk[:, :, :, 0] + pe_k[None, None, :, None, :], axis=2), w_ck)
    cmp_v = jnp.einsum('bngd,de->bnge', jnp.mean(kv_blk[:, :, :, 1] + pe_v[None, None, :, None, :], axis=2), w_cv)
    sel_k = jnp.transpose(kv_blk[:, :, :, 2], (0, 3, 1, 2, 4))
    sel_v = jnp.transpose(kv_blk[:, :, :, 3], (0, 3, 1, 2, 4))
    blk_id = jnp.arange(nblk)
    blk_end = (blk_id + 1) * NSA_BLOCK - 1
    g_idx = jnp.arange(N_KV_HEADS)[None, :, None]
    win_rows = jnp.pad(win_rows, [(0, 0), (0, nq * qb - T), (0, 0), (0, 0), (0, 0)])

    def gather_blocks(rows, idx):
        g = jax.vmap(lambda r, i: r[g_idx, i])(rows, idx)
        return g.reshape(B, qb, N_KV_HEADS, k_sel * NSA_BLOCK, HEAD_DIM)

    def block(j, qj, gj):
        qpos = P + j * qb + jnp.arange(qb)
        qg = qj.reshape(B, qb, N_KV_HEADS, GQA_GROUP, HEAD_DIM)
        vis = blk_end[None, :] <= qpos[:, None]
        zc = jnp.einsum('btgrd,bngd->btgrn', qg, cmp_k).astype(jnp.float32) * scale
        pc = masked_softmax(zc, vis[None, :, None, None, :])
        o_cmp = jnp.einsum('btgrn,bngd->btgrd', pc.astype(dt), cmp_v)
        imp = jnp.where(vis[None, :, None, :], jnp.sum(pc, axis=3), -jnp.inf)
        forced = (blk_id[None, :] == (qpos // NSA_BLOCK)[:, None]) | (blk_id[None, :] == 0)
        imp = jnp.where(forced[None, :, None, :], jnp.inf, imp)
        top_val, top_blk = lax.top_k(imp, k_sel)
        ks = gather_blocks(sel_k, top_blk)
        vs = gather_blocks(sel_v, top_blk)
        tok = (top_blk[..., None] * NSA_BLOCK + jnp.arange(NSA_BLOCK)).reshape(B, qb, N_KV_HEADS, -1)
        smask = jnp.repeat(top_val > -jnp.inf, NSA_BLOCK, axis=-1) & (tok <= qpos[None, :, None, None])
        zs = jnp.einsum('btgrd,btgsd->btgrs', qg, ks).astype(jnp.float32) * scale
        ps = masked_softmax(zs, smask[:, :, :, None, :])
        o_sel = jnp.einsum('btgrs,btgsd->btgrd', ps.astype(dt), vs)
        wj = lax.dynamic_slice_in_dim(win_rows, j * qb, qb + NSA_WINDOW, axis=1)
        wpos = P - NSA_WINDOW + j * qb + jnp.arange(qb + NSA_WINDOW)
        wmask = (wpos[None, :] >= 0) & (wpos[None, :] <= qpos[:, None]) & (wpos[None, :] >= qpos[:, None] - NSA_WINDOW)
        zw = jnp.einsum('btgrd,bsgd->btgrs', qg, wj[:, :, 0]).astype(jnp.float32) * scale
        pw = masked_softmax(zw, wmask[None, :, None, None, :])
        o_win = jnp.einsum('btgrs,bsgd->btgrd', pw.astype(dt), wj[:, :, 1])
        g = jax.nn.sigmoid(gj.astype(jnp.float32)).reshape(B, qb, N_KV_HEADS, GQA_GROUP, 3).astype(dt)
        o = g[..., 0:1] * o_cmp + g[..., 1:2] * o_sel + g[..., 2:3] * o_win
        return o.reshape(B, qb, MIX_W)

    return qblock_sweep(block, T, q, gates)


def dsa_attention(q, q_idx, w_idx, k_all, v_all, k_idx_all, P):
    B, T = q.shape[:2]
    L = k_all.shape[1]
    qb = min(Q_BLOCK, T)
    topk = min(DSA_TOPK_MAX, L // 4)
    kpos = jnp.arange(L)
    scale = HEAD_DIM ** -0.5
    gather = jax.vmap(lambda rows, idx: rows[idx])

    def block(j, qj, qij, wij):
        qpos = P + j * qb + jnp.arange(qb)
        score = jnp.einsum('bthd,bsd->bths', qij, k_idx_all)
        idx_score = jnp.einsum('bth,bths->bts', wij, jax.nn.relu(score)).astype(jnp.float32)
        idx_score = jnp.where(kpos[None, None, :] <= qpos[None, :, None], idx_score, -jnp.inf)
        top_val, top_idx = lax.top_k(idx_score, topk)
        kg = gather(k_all, top_idx)
        vg = gather(v_all, top_idx)
        qg = qj.reshape(B, qb, N_KV_HEADS, GQA_GROUP, HEAD_DIM)
        z = jnp.einsum('btgrd,btkgd->btgrk', qg, kg).astype(jnp.float32) * scale
        p = masked_softmax(z, jnp.isfinite(top_val)[:, :, None, None, :])
        o = jnp.einsum('btgrk,btkgd->btgrd', p.astype(vg.dtype), vg)
        return o.reshape(B, qb, MIX_W)

    return qblock_sweep(block, T, q, q_idx, w_idx)


def forgetting_attention(q, k_all, v_all, c_all, P):
    B, T = q.shape[:2]
    L = k_all.shape[1]
    qb = min(Q_BLOCK, T)
    kpos = jnp.arange(L)
    scale = HEAD_DIM ** -0.5
    c_key = jnp.moveaxis(c_all, 2, 1).reshape(B, N_KV_HEADS, GQA_GROUP, 1, L)

    def block(j, qj, cj):
        qpos = P + j * qb + jnp.arange(qb)
        qg = qj.reshape(B, qb, N_KV_HEADS, GQA_GROUP, HEAD_DIM)
        z = jnp.einsum('btgrd,bsgd->bgrts', qg, k_all).astype(jnp.float32) * scale
        c_q = jnp.moveaxis(cj, 2, 1).reshape(B, N_KV_HEADS, GQA_GROUP, qb, 1)
        p = masked_softmax(z + c_q - c_key, kpos[None, :] <= qpos[:, None])
        o = jnp.einsum('bgrts,bsgd->btgrd', p.astype(v_all.dtype), v_all)
        return o.reshape(B, qb, MIX_W)

    return qblock_sweep(block, T, q, c_all[:, P:])


def token_mixers(x, P, past, lw):
    B, T, _ = x.shape
    G, H = N_KV_HEADS, N_Q_HEADS
    pos = P + jnp.arange(T)
    h = split_columns(x @ lw['w_in'])

    def heads(name, n):
        return h[name].reshape(B, T, n, HEAD_DIM)

    def with_past(name, rows):
        return rows if past is None else jnp.concatenate([past[name], rows], axis=1)

    new = {}
    new['sb_kv'] = jnp.stack([heads('sb_k', G), heads('sb_v', G)], axis=2)
    kv = with_past('sb_kv', new['sb_kv'])
    y_a = stick_breaking_attention(heads('sb_q', H), kv[:, :, 0], kv[:, :, 1], P)
    new['nsa_kv'] = jnp.stack([rope(heads('nsa_kc', G), pos), heads('nsa_vc', G),
                               rope(heads('nsa_ks', G), pos), heads('nsa_vs', G)], axis=2)
    win_new = jnp.stack([rope(heads('nsa_kw', G), pos), heads('nsa_vw', G)], axis=2)
    win_all = with_past('nsa_win_kv', win_new)
    n_prev = win_all.shape[1] - T
    keep = min(NSA_WINDOW, T) if past is None else n_prev
    new['nsa_win_kv'] = win_all[:, win_all.shape[1] - keep:]
    win_rows = jnp.pad(win_all, [(0, 0), (NSA_WINDOW - n_prev, 0), (0, 0), (0, 0), (0, 0)])
    y_b = nsa_attention(rope(heads('nsa_q', H), pos), h['nsa_g'], with_past('nsa_kv', new['nsa_kv']), win_rows,
                        lw['nsa_pe_k'], lw['nsa_pe_v'], lw['nsa_w_ck'], lw['nsa_w_cv'], P)
    new['dsa_kv'] = jnp.stack([rope(heads('dsa_k', G), pos), heads('dsa_v', G)], axis=2)
    new['dsa_idx_k'] = rope(h['idx_k'][:, :, None, :], pos)[:, :, 0]
    kv = with_past('dsa_kv', new['dsa_kv'])
    q_idx = rope(h['idx_q'].reshape(B, T, IDX_HEADS, IDX_DIM), pos)
    y_c = dsa_attention(rope(heads('dsa_q', H), pos), q_idx, h['idx_w'], kv[:, :, 0], kv[:, :, 1],
                        with_past('dsa_idx_k', new['dsa_idx_k']), P)
    new['fox_kv'] = jnp.stack([heads('fox_k', G), heads('fox_v', G)], axis=2)
    new['fox_logf'] = jax.nn.log_sigmoid((h['fox_f'] + lw['fox_b_f']).astype(jnp.float32)).astype(x.dtype)
    kv = with_past('fox_kv', new['fox_kv'])
    c_all = jnp.cumsum(with_past('fox_logf', new['fox_logf']).astype(jnp.float32), axis=1)
    y_d = forgetting_attention(heads('fox_q', H), kv[:, :, 0], kv[:, :, 1], c_all, P)
    branches = jnp.stack([y_a, y_b, y_c, y_d], axis=2)
    proj = jnp.einsum('btnc,ncd->btnd', branches, lw['w_br'])
    gate = jax.nn.sigmoid((x @ lw['w_gate'] + lw['b_gate']).astype(jnp.float32)).reshape(B, T, N_BRANCH, D_MODEL)
    merged = jnp.sum(gate.astype(proj.dtype) * proj, axis=2)
    return merged @ lw['w_o'], new


def memory_cross_attention(x, mem_kv, w_mq, w_mo):
    B, T, _ = x.shape
    q = (x @ w_mq).reshape(B, T, MEM_HEADS, MEM_HEAD_DIM)
    z = jnp.einsum('bthd,bmhd->bhtm', q, mem_kv[:, :, 0]).astype(jnp.float32) * MEM_HEAD_DIM ** -0.5
    p = jax.nn.softmax(z, axis=-1)
    o = jnp.einsum('bhtm,bmhd->bthd', p.astype(x.dtype), mem_kv[:, :, 1])
    return o.reshape(B, T, MEM_W) @ w_mo


def hier_moe(x, lw):
    B, T, D = x.shape
    xf = x.reshape(B * T, D)
    grp_logits = (xf @ lw['w_rg'] + lw['b_rg']).astype(jnp.float32)
    grp_prob = jax.nn.softmax(grp_logits, axis=-1)
    grp = jnp.argmax(grp_logits, axis=-1)
    p_grp = jnp.take_along_axis(grp_prob, grp[:, None], axis=-1)
    exp_logits = (xf @ lw['w_re'] + lw['b_re']).astype(jnp.float32).reshape(-1, N_GROUPS, EXPERTS_PER_GROUP)
    in_grp = jnp.take_along_axis(exp_logits, grp[:, None, None], axis=1)[:, 0]
    top_val, top_idx = lax.top_k(in_grp, TOP_K_IN_GROUP)
    top_w = jax.nn.softmax(top_val, axis=-1) * p_grp
    expert_id = grp[:, None] * EXPERTS_PER_GROUP + top_idx
    gate = jnp.sum(jax.nn.one_hot(expert_id, N_EXPERTS, dtype=jnp.float32) * top_w[..., None], axis=1)
    hid = jax.nn.silu(jnp.einsum('nd,edf->nef', xf, lw['w_eg'])) * jnp.einsum('nd,edf->nef', xf, lw['w_eu'])
    y = jnp.einsum('nef,efd->nd', hid * gate[..., None].astype(hid.dtype), lw['w_ed'])
    return y.reshape(B, T, D)


def trunk_layer(x, P, past, mem_kv, lw):
    alpha = (2.0 * DEPTH) ** 0.25
    y, new = token_mixers(x, P, past, lw)
    x = layer_norm(alpha * x + y, lw['ln1_g'], lw['ln1_b'])
    x = layer_norm(alpha * x + memory_cross_attention(x, mem_kv, lw['w_mq'], lw['w_mo']), lw['ln2_g'], lw['ln2_b'])
    x = layer_norm(alpha * x + hier_moe(x, lw), lw['ln3_g'], lw['ln3_b'])
    return x, new


def run_trunk(x, P, past_fn, mem_kv_fn, ln_in_g, ln_in_b, layer_params):
    x = layer_norm(x, ln_in_g, ln_in_b)
    per_layer = []
    for l in range(DEPTH):
        lw = {name: p[l] for name, p in layer_params.items()}
        x, new = trunk_layer(x, P, past_fn(l), mem_kv_fn(l), lw)
        per_layer.append(new)
    new_state = {name: jnp.stack([n[name] for n in per_layer], axis=0) for name in per_layer[0]}
    return x, new_state


def setup_inputs(seed: int = 0) -> dict:
    key = jax.random.key(seed)
    keys = list(jax.random.split(key, 64))

    def nrm(shape, scale):
        return jax.random.normal(keys.pop(), shape, jnp.float32) * scale

    beta = (8.0 * DEPTH) ** -0.25
    n_pages = PAST_LEN // PAGE_SIZE
    n_used = DEC_BATCH * n_pages
    n_pool = n_used + n_used // 4 + 1
    w_past = min(NSA_WINDOW, PAST_LEN)
    G, H, d = N_KV_HEADS, N_Q_HEADS, HEAD_DIM
    page_table = jax.random.permutation(keys.pop(), n_pool)[:n_used].reshape(DEC_BATCH, n_pages).astype(jnp.int32)
    return {
        'x_prompt': nrm((BATCH, SEQ, D_MODEL), 1.0),
        'x_sample': nrm((DEC_BATCH, DEC_SEQ, D_MODEL), 1.0),
        'mem_prompt': nrm((BATCH, N_MEM, D_MODEL), 1.0),
        'cache_sb_kv': nrm((DEPTH, n_pool, PAGE_SIZE, 2, G, d), 1.0),
        'cache_nsa_kv': nrm((DEPTH, n_pool, PAGE_SIZE, 4, G, d), 1.0),
        'state_nsa_win_kv': nrm((DEPTH, DEC_BATCH, w_past, 2, G, d), 1.0),
        'cache_dsa_kv': nrm((DEPTH, n_pool, PAGE_SIZE, 2, G, d), 1.0),
        'cache_dsa_idx_k': nrm((DEPTH, n_pool, PAGE_SIZE, IDX_DIM), 1.0),
        'cache_fox_kv': nrm((DEPTH, n_pool, PAGE_SIZE, 2, G, d), 1.0),
        'cache_fox_logf': jax.nn.log_sigmoid(FOX_BIAS_INIT + nrm((DEPTH, n_pool, PAGE_SIZE, H), 1.0)),
        'cache_mem_kv': nrm((DEPTH, DEC_BATCH, N_MEM, 2, MEM_HEADS, MEM_HEAD_DIM), 1.0),
        'page_table': page_table,
        'ln_in_g': 1.0 + nrm((D_MODEL,), 0.01),
        'ln_in_b': nrm((D_MODEL,), 0.01),
        'w_in': nrm((DEPTH, D_MODEL, N_IN), D_MODEL ** -0.5),
        'fox_b_f': FOX_BIAS_INIT + nrm((DEPTH, H), 0.5),
        'nsa_pe_k': nrm((DEPTH, NSA_BLOCK, d), 0.1),
        'nsa_pe_v': nrm((DEPTH, NSA_BLOCK, d), 0.1),
        'nsa_w_ck': nrm((DEPTH, d, d), d ** -0.5),
        'nsa_w_cv': nrm((DEPTH, d, d), d ** -0.5),
        'w_br': nrm((DEPTH, N_BRANCH, MIX_W, D_MODEL), MIX_W ** -0.5),
        'w_gate': nrm((DEPTH, D_MODEL, N_BRANCH * D_MODEL), D_MODEL ** -0.5),
        'b_gate': nrm((DEPTH, N_BRANCH * D_MODEL), 0.01),
        'w_o': nrm((DEPTH, D_MODEL, D_MODEL), beta * D_MODEL ** -0.5),
        'ln1_g': 1.0 + nrm((DEPTH, D_MODEL), 0.01),
        'ln1_b': nrm((DEPTH, D_MODEL), 0.01),
        'w_mq': nrm((DEPTH, D_MODEL, MEM_W), D_MODEL ** -0.5),
        'w_mkv': nrm((DEPTH, D_MODEL, 2 * MEM_W), D_MODEL ** -0.5),
        'w_mo': nrm((DEPTH, MEM_W, D_MODEL), beta * MEM_W ** -0.5),
        'ln2_g': 1.0 + nrm((DEPTH, D_MODEL), 0.01),
        'ln2_b': nrm((DEPTH, D_MODEL), 0.01),
        'w_rg': nrm((DEPTH, D_MODEL, N_GROUPS), D_MODEL ** -0.5),
        'b_rg': nrm((DEPTH, N_GROUPS), 0.01),
        'w_re': nrm((DEPTH, D_MODEL, N_EXPERTS), D_MODEL ** -0.5),
        'b_re': nrm((DEPTH, N_EXPERTS), 0.01),
        'w_eg': nrm((DEPTH, N_EXPERTS, D_MODEL, D_FF_EXPERT), D_MODEL ** -0.5),
        'w_eu': nrm((DEPTH, N_EXPERTS, D_MODEL, D_FF_EXPERT), D_MODEL ** -0.5),
        'w_ed': nrm((DEPTH, N_EXPERTS, D_FF_EXPERT, D_MODEL), beta * D_FF_EXPERT ** -0.5),
        'ln3_g': 1.0 + nrm((DEPTH, D_MODEL), 0.01),
        'ln3_b': nrm((DEPTH, D_MODEL), 0.01),
    }


def reference(x_prompt, x_sample, mem_prompt, cache_sb_kv, cache_nsa_kv, state_nsa_win_kv, cache_dsa_kv,
              cache_dsa_idx_k, cache_fox_kv, cache_fox_logf, cache_mem_kv, page_table,
              ln_in_g, ln_in_b, w_in, fox_b_f, nsa_pe_k, nsa_pe_v, nsa_w_ck, nsa_w_cv, w_br, w_gate, b_gate, w_o,
              ln1_g, ln1_b, w_mq, w_mkv, w_mo, ln2_g, ln2_b, w_rg, b_rg, w_re, b_re, w_eg, w_eu, w_ed, ln3_g, ln3_b):
    layer_params = {
        'w_in': w_in, 'fox_b_f': fox_b_f, 'nsa_pe_k': nsa_pe_k, 'nsa_pe_v': nsa_pe_v,
        'nsa_w_ck': nsa_w_ck, 'nsa_w_cv': nsa_w_cv, 'w_br': w_br, 'w_gate': w_gate, 'b_gate': b_gate,
        'w_o': w_o, 'ln1_g': ln1_g, 'ln1_b': ln1_b, 'w_mq': w_mq, 'w_mo': w_mo, 'ln2_g': ln2_g, 'ln2_b': ln2_b,
        'w_rg': w_rg, 'b_rg': b_rg, 'w_re': w_re, 'b_re': b_re, 'w_eg': w_eg, 'w_eu': w_eu, 'w_ed': w_ed,
        'ln3_g': ln3_g, 'ln3_b': ln3_b,
    }
    mem_kv_prompt = jnp.einsum('bmd,lde->lbme', mem_prompt, w_mkv).reshape(
        DEPTH, mem_prompt.shape[0], mem_prompt.shape[1], 2, MEM_HEADS, MEM_HEAD_DIM)
    y_prompt, st_p = run_trunk(x_prompt, 0, lambda l: None, lambda l: mem_kv_prompt[l],
                               ln_in_g, ln_in_b, layer_params)
    past_len = page_table.shape[1] * cache_sb_kv.shape[2]

    def sample_past(l):
        return {
            'sb_kv': gather_pages(cache_sb_kv[l], page_table),
            'nsa_kv': gather_pages(cache_nsa_kv[l], page_table),
            'nsa_win_kv': state_nsa_win_kv[l],
            'dsa_kv': gather_pages(cache_dsa_kv[l], page_table),
            'dsa_idx_k': gather_pages(cache_dsa_idx_k[l], page_table),
            'fox_kv': gather_pages(cache_fox_kv[l], page_table),
            'fox_logf': gather_pages(cache_fox_logf[l], page_table),
        }

    y_sample, st_s = run_trunk(x_sample, past_len, sample_past, lambda l: cache_mem_kv[l],
                               ln_in_g, ln_in_b, layer_params)
    return (y_prompt, y_sample,
            st_p['sb_kv'], st_s['sb_kv'],
            st_p['nsa_kv'], st_s['nsa_kv'],
            st_p['nsa_win_kv'], st_s['nsa_win_kv'],
            st_p['dsa_kv'], st_s['dsa_kv'],
            st_p['dsa_idx_k'], st_s['dsa_idx_k'],
            st_p['fox_kv'], st_s['fox_kv'],
            st_p['fox_logf'], st_s['fox_logf'],
            mem_kv_prompt)
```

```python
import functools
import math

import jax
import jax.numpy as jnp
import numpy as np
from jax import lax
from jax.experimental import pallas as pl
from jax.experimental.pallas import tpu as pltpu

F32 = jnp.float32
BF16 = jnp.bfloat16
I32 = jnp.int32

D_MODEL = 2048
HEAD_DIM = 64
N_Q_HEADS = 8
N_KV_HEADS = 2
GQA_GROUP = 4
MIX_W = 512
KV_W = 128
N_BRANCH = 4
ROPE_THETA = 10000.0
LN_EPS = 1e-5
NSA_BLOCK = 64
NSA_TOPK = 16
NSA_WINDOW = 512
IDX_HEADS = 4
IDX_DIM = 64
DSA_TOPK_MAX = 256
N_MEM = 256
MEM_HEADS = 4
MEM_HEAD_DIM = 128
MEM_W = 512
N_GROUPS = 4
EXPERTS_PER_GROUP = 4
N_EXPERTS = 16
D_FF_EXPERT = 256

LANES = 128
NEG = -1e30
VMEM_LIMIT = 48 * 1024 * 1024

IN_SPLITS = (
    ('sb_q', MIX_W), ('sb_k', KV_W), ('sb_v', KV_W),
    ('nsa_q', MIX_W), ('nsa_kc', KV_W), ('nsa_vc', KV_W), ('nsa_ks', KV_W), ('nsa_vs', KV_W),
    ('nsa_kw', KV_W), ('nsa_vw', KV_W), ('nsa_g', 3 * N_Q_HEADS),
    ('dsa_q', MIX_W), ('dsa_k', KV_W), ('dsa_v', KV_W),
    ('idx_q', IDX_HEADS * IDX_DIM), ('idx_k', IDX_DIM), ('idx_w', IDX_HEADS),
    ('fox_q', MIX_W), ('fox_k', KV_W), ('fox_v', KV_W), ('fox_f', N_Q_HEADS),
)
N_IN = sum(w for _, w in IN_SPLITS)

T_NSAQ, T_DSAQ, T_IDXQ, T_IDXK, T_NSA_KC, T_NSA_KS, T_NSA_KW, T_DSA_K = 0, 4, 8, 10, 11, 12, 13, 14
N_ROPE_TILES = 16
T_SBQ, T_FOXQ, T_SB_K, T_SB_V, T_NSA_VC, T_NSA_VS, T_NSA_VW, T_DSA_V, T_FOX_K, T_FOX_V, T_MISC = (
    16, 20, 24, 25, 26, 27, 28, 29, 30, 31, 32)
N_TILES = 34
PROJ_W = N_TILES * LANES
MISC_FOXF, MISC_NSAG, MISC_IDXW = 0, 8, 32


def _in_offsets():
    off, out = 0, {}
    for name, width in IN_SPLITS:
        out[name] = off
        off += width
    return out


def _proj_column_perm():
    off = _in_offsets()
    perm = np.full((PROJ_W,), N_IN, np.int32)

    def put(tile, lane, name, start, width):
        base = tile * LANES + lane
        perm[base:base + width] = off[name] + start + np.arange(width)

    def put_q(tile0, name):
        for r in range(GQA_GROUP):
            put(tile0 + r, 0, name, r * HEAD_DIM, HEAD_DIM)
            put(tile0 + r, HEAD_DIM, name, (GQA_GROUP + r) * HEAD_DIM, HEAD_DIM)

    put_q(T_NSAQ, 'nsa_q')
    put_q(T_DSAQ, 'dsa_q')
    put_q(T_SBQ, 'sb_q')
    put_q(T_FOXQ, 'fox_q')
    put(T_IDXQ, 0, 'idx_q', 0, 128)
    put(T_IDXQ + 1, 0, 'idx_q', 128, 128)
    put(T_IDXK, 0, 'idx_k', 0, 64)
    put(T_IDXK, 64, 'idx_k', 0, 64)
    for tile, name in ((T_NSA_KC, 'nsa_kc'), (T_NSA_KS, 'nsa_ks'), (T_NSA_KW, 'nsa_kw'), (T_DSA_K, 'dsa_k'),
                       (T_SB_K, 'sb_k'), (T_SB_V, 'sb_v'), (T_NSA_VC, 'nsa_vc'), (T_NSA_VS, 'nsa_vs'),
                       (T_NSA_VW, 'nsa_vw'), (T_DSA_V, 'dsa_v'), (T_FOX_K, 'fox_k'), (T_FOX_V, 'fox_v')):
        put(tile, 0, name, 0, 128)
    put(T_MISC, MISC_FOXF, 'fox_f', 0, 8)
    put(T_MISC, MISC_NSAG, 'nsa_g', 0, 24)
    put(T_MISC, MISC_IDXW, 'idx_w', 0, 4)
    return perm


def _y_row_perm():
    perm = np.zeros((MIX_W,), np.int32)
    for r in range(GQA_GROUP):
        for half in range(2):
            h = half * GQA_GROUP + r
            perm[r * LANES + half * HEAD_DIM:r * LANES + (half + 1) * HEAD_DIM] = h * HEAD_DIM + np.arange(HEAD_DIM)
    return perm


def _dot(a, b):
    return jnp.dot(a, b, preferred_element_type=F32)


def _dot_nt(a, b):
    return lax.dot_general(a, b, (((1,), (1,)), ((), ())), preferred_element_type=F32)


def _split2(x):
    hi = x.astype(BF16)
    lo = (x - hi.astype(F32)).astype(BF16)
    return hi, lo


def _dot_x_exact(x, m_bf16):
    hi, lo = _split2(x)
    return _dot(hi, m_bf16) + _dot(lo, m_bf16)


def _dot_exact_x(m_bf16, x):
    hi, lo = _split2(x)
    return _dot(m_bf16, hi) + _dot(m_bf16, lo)


def _dot3(a, b):
    ah, al = _split2(a)
    bh, bl = _split2(b)
    return _dot(ah, bh) + (_dot(ah, bl) + _dot(al, bh))


def _dot3_nt(a, b):
    ah, al = _split2(a)
    bh, bl = _split2(b)
    return _dot_nt(ah, bh) + (_dot_nt(ah, bl) + _dot_nt(al, bh))


def _iota(shape, dim):
    return lax.broadcasted_iota(I32, shape, dim)


def _layer_norm(v, g, b):
    mu = jnp.mean(v, axis=-1, keepdims=True)
    c = v - mu
    var = jnp.mean(c * c, axis=-1, keepdims=True)
    return c * lax.rsqrt(var + LN_EPS) * g + b


def _stack_q(q, scale):
    tq = q.shape[0]
    low = _iota((tq, LANES), 1) < HEAD_DIM
    parts = []
    for g in range(N_KV_HEADS):
        for r in range(GQA_GROUP):
            t = q[:, r * LANES:(r + 1) * LANES]
            parts.append(jnp.where(low if g == 0 else jnp.logical_not(low), t * scale, 0.0))
    return jnp.concatenate(parts, axis=0)


def _store_heads(o_ref, o, tq):
    low = _iota((tq, LANES), 1) < HEAD_DIM
    for r in range(GQA_GROUP):
        lo_head = o[r * tq:(r + 1) * tq]
        hi_head = o[(GQA_GROUP + r) * tq:(GQA_GROUP + r + 1) * tq]
        o_ref[0, :, r * LANES:(r + 1) * LANES] = jnp.where(low, lo_head, hi_head)


def _rep_rows(x, n):
    return jnp.concatenate([x] * n, axis=0)


def _params(*sem):
    return pltpu.CompilerParams(dimension_semantics=sem, vmem_limit_bytes=VMEM_LIMIT)


def _ln_kernel(x_ref, g_ref, b_ref, o_ref):
    o_ref[...] = _layer_norm(x_ref[...], g_ref[...], b_ref[...])


def layer_norm_rows(x, g, b, tm):
    n, d = x.shape
    return pl.pallas_call(
        _ln_kernel,
        grid=(n // tm,),
        in_specs=[pl.BlockSpec((tm, d), lambda i: (i, 0)),
                  pl.BlockSpec((1, d), lambda i: (0, 0)),
                  pl.BlockSpec((1, d), lambda i: (0, 0))],
        out_specs=pl.BlockSpec((tm, d), lambda i: (i, 0)),
        out_shape=jax.ShapeDtypeStruct((n, d), F32),
        compiler_params=_params("arbitrary"),
        name="ln_in",
    )(x, g.reshape(1, d), b.reshape(1, d))


def _proj_kernel(x_ref, w_ref, cos_ref, sin_ref, o_ref, xb_ref, *, n_rope_blocks, tn):
    j = pl.program_id(1)

    @pl.when(j == 0)
    def _():
        xb_ref[...] = x_ref[...].astype(BF16)

    acc = _dot(xb_ref[...], w_ref[...])

    @pl.when(j < n_rope_blocks)
    def _():
        cos = cos_ref[...]
        sin = sin_ref[...]
        first = (_iota(cos.shape, 1) & (HEAD_DIM - 1)) < HEAD_DIM // 2
        for s in range(tn // LANES):
            a = acc[:, s * LANES:(s + 1) * LANES]
            rot = jnp.where(first, pltpu.roll(a, LANES - HEAD_DIM // 2, 1), pltpu.roll(a, HEAD_DIM // 2, 1))
            o_ref[:, s * LANES:(s + 1) * LANES] = a * cos + rot * sin

    @pl.when(j >= n_rope_blocks)
    def _():
        o_ref[...] = acc


def project(x, w_perm, cos_t, sin_t, tm, tn=256):
    n, d = x.shape
    n_pos_blocks = cos_t.shape[0] // tm
    return pl.pallas_call(
        functools.partial(_proj_kernel, n_rope_blocks=N_ROPE_TILES * LANES // tn, tn=tn),
        grid=(n // tm, PROJ_W // tn),
        in_specs=[pl.BlockSpec((tm, d), lambda i, j: (i, 0)),
                  pl.BlockSpec((d, tn), lambda i, j: (0, j)),
                  pl.BlockSpec((tm, LANES), lambda i, j: (i % n_pos_blocks, 0)),
                  pl.BlockSpec((tm, LANES), lambda i, j: (i % n_pos_blocks, 0))],
        out_specs=pl.BlockSpec((tm, tn), lambda i, j: (i, j)),
        out_shape=jax.ShapeDtypeStruct((n, PROJ_W), F32),
        scratch_shapes=[pltpu.VMEM((tm, d), BF16)],
        compiler_params=_params("arbitrary", "arbitrary"),
        name="proj",
    )(x, w_perm, cos_t, sin_t)


def _matmul_kernel(x_ref, w_ref, o_ref):
    o_ref[...] = _dot(x_ref[...].astype(BF16), w_ref[...])


def matmul(x, w, tm, tn):
    n, d = x.shape
    m = w.shape[1]
    return pl.pallas_call(
        _matmul_kernel,
        grid=(n // tm, m // tn),
        in_specs=[pl.BlockSpec((tm, d), lambda i, j: (i, 0)),
                  pl.BlockSpec((d, tn), lambda i, j: (0, j))],
        out_specs=pl.BlockSpec((tm, tn), lambda i, j: (i, j)),
        out_shape=jax.ShapeDtypeStruct((n, m), F32),
        compiler_params=_params("arbitrary", "arbitrary"),
        name="matmul",
    )(x, w)


def _merge_kernel(x_ref, ya_ref, yb_ref, yc_ref, yd_ref, wg0, wg1, wg2, wg3, bg0, bg1, bg2, bg3,
                  wb0, wb1, wb2, wb3, o_ref, xb_ref, yb16_ref):
    j = pl.program_id(1)

    @pl.when(j == 0)
    def _():
        xb_ref[...] = x_ref[...].astype(BF16)
        for b, y in enumerate((ya_ref, yb_ref, yc_ref, yd_ref)):
            yb16_ref[b] = y[...].astype(BF16)

    xb = xb_ref[...]
    acc = None
    for b, (wg, bg, wb) in enumerate(((wg0, bg0, wb0), (wg1, bg1, wb1), (wg2, bg2, wb2), (wg3, bg3, wb3))):
        gate = jax.nn.sigmoid(_dot(xb, wg[...]) + bg[...])
        term = gate * _dot(yb16_ref[b], wb[0])
        acc = term if acc is None else acc + term
    o_ref[...] = acc.astype(BF16)


def gated_merge(x, ys, w_gate, b_gate, w_br, tm, tn=256):
    n, d = x.shape
    nj = d // tn
    in_specs = [pl.BlockSpec((tm, d), lambda i, j: (i, 0))]
    in_specs += [pl.BlockSpec((tm, MIX_W), lambda i, j: (i, 0)) for _ in range(N_BRANCH)]
    in_specs += [pl.BlockSpec((d, tn), functools.partial(lambda i, j, b: (0, b * nj + j), b=b)) for b in range(N_BRANCH)]
    in_specs += [pl.BlockSpec((1, tn), functools.partial(lambda i, j, b: (0, b * nj + j), b=b)) for b in range(N_BRANCH)]
    in_specs += [pl.BlockSpec((1, MIX_W, tn), functools.partial(lambda i, j, b: (b, 0, j), b=b)) for b in range(N_BRANCH)]
    return pl.pallas_call(
        _merge_kernel,
        grid=(n // tm, nj),
        in_specs=in_specs,
        out_specs=pl.BlockSpec((tm, tn), lambda i, j: (i, j)),
        out_shape=jax.ShapeDtypeStruct((n, d), BF16),
        scratch_shapes=[pltpu.VMEM((tm, d), BF16), pltpu.VMEM((N_BRANCH, tm, MIX_W), BF16)],
        compiler_params=_params("arbitrary", "arbitrary"),
        name="gated_merge",
    )(x, *ys, *([w_gate] * N_BRANCH), *([b_gate] * N_BRANCH), *([w_br] * N_BRANCH))


def _matmul_res_ln_kernel(a_ref, w_ref, r_ref, g_ref, b_ref, o_ref, *, alpha):
    y = _dot(a_ref[...].astype(BF16), w_ref[...])
    o_ref[...] = _layer_norm(alpha * r_ref[...] + y, g_ref[...], b_ref[...])


def matmul_res_ln(a, w, res, g, b, alpha, tm):
    n, k = a.shape
    d = w.shape[1]
    return pl.pallas_call(
        functools.partial(_matmul_res_ln_kernel, alpha=alpha),
        grid=(n // tm,),
        in_specs=[pl.BlockSpec((tm, k), lambda i: (i, 0)),
                  pl.BlockSpec((k, d), lambda i: (0, 0)),
                  pl.BlockSpec((tm, d), lambda i: (i, 0)),
                  pl.BlockSpec((1, d), lambda i: (0, 0)),
                  pl.BlockSpec((1, d), lambda i: (0, 0))],
        out_specs=pl.BlockSpec((tm, d), lambda i: (i, 0)),
        out_shape=jax.ShapeDtypeStruct((n, d), F32),
        compiler_params=_params("arbitrary"),
        name="matmul_res_ln",
    )(a, w, res, g.reshape(1, d), b.reshape(1, d))


def _mem_kernel(x_ref, wq_ref, k_ref, v_ref, wo_ref, g_ref, b_ref, o_ref, *, alpha):
    x = x_ref[0]
    q = _dot(x.astype(BF16), wq_ref[...])
    outs = []
    for h in range(MEM_HEADS):
        sl = slice(h * MEM_HEAD_DIM, (h + 1) * MEM_HEAD_DIM)
        z = _dot_nt(q[:, sl].astype(BF16), k_ref[0, 0, :, sl].astype(BF16)) * (MEM_HEAD_DIM ** -0.5)
        m = jnp.max(z, axis=-1, keepdims=True)
        e = jnp.exp(z - m)
        p = e / jnp.sum(e, axis=-1, keepdims=True)
        outs.append(_dot(p.astype(BF16), v_ref[0, 0, :, sl].astype(BF16)))
    o = jnp.concatenate(outs, axis=1)
    y = _dot(o.astype(BF16), wo_ref[...])
    o_ref[0] = _layer_norm(alpha * x + y, g_ref[...], b_ref[...])


def memory_block(x, mem_kv, layer, w_mq, w_mo, g, b, alpha, tm):
    bsz, t, d = x.shape
    return pl.pallas_call(
        functools.partial(_mem_kernel, alpha=alpha),
        grid=(bsz, t // tm),
        in_specs=[pl.BlockSpec((1, tm, d), lambda bi, i: (bi, i, 0)),
                  pl.BlockSpec((d, MEM_W), lambda bi, i: (0, 0)),
                  pl.BlockSpec((1, 1, N_MEM, MEM_W), lambda bi, i: (layer, bi, 0, 0)),
                  pl.BlockSpec((1, 1, N_MEM, MEM_W), lambda bi, i: (layer, bi, 0, 1)),
                  pl.BlockSpec((MEM_W, d), lambda bi, i: (0, 0)),
                  pl.BlockSpec((1, d), lambda bi, i: (0, 0)),
                  pl.BlockSpec((1, d), lambda bi, i: (0, 0))],
        out_specs=pl.BlockSpec((1, tm, d), lambda bi, i: (bi, i, 0)),
        out_shape=jax.ShapeDtypeStruct((bsz, t, d), F32),
        compiler_params=_params("arbitrary", "arbitrary"),
        name="memory_block",
    )(x, w_mq, mem_kv, mem_kv, w_mo, g.reshape(1, d), b.reshape(1, d))


def _moe_kernel(x_ref, wr_ref, br_ref, weg_ref, weu_ref, wed_ref, g_ref, b_ref, o_ref,
                xb_ref, gate_ref, acc_ref, *, alpha):
    e = pl.program_id(1)
    tm = x_ref.shape[0]
    lane = _iota((tm, LANES), 1)

    @pl.when(e == 0)
    def _():
        x = x_ref[...]
        xb_ref[...] = x.astype(BF16)
        acc_ref[...] = jnp.zeros_like(acc_ref)
        logits = _dot3(x, wr_ref[...]) + br_ref[...]
        is_grp = lane < N_GROUPS
        gl = jnp.where(is_grp, logits, NEG)
        gmax = jnp.max(gl, axis=-1, keepdims=True)
        grp = jnp.min(jnp.where(is_grp & (gl == gmax), lane, LANES), axis=-1, keepdims=True)
        p_grp = 1.0 / jnp.sum(jnp.where(is_grp, jnp.exp(gl - gmax), 0.0), axis=-1, keepdims=True)
        in_grp = (lane >= N_GROUPS) & (lane < N_GROUPS + N_EXPERTS) & (
            ((lane - N_GROUPS) >> 2) == grp)
        el = jnp.where(in_grp, logits, NEG)
        v1 = jnp.max(el, axis=-1, keepdims=True)
        i1 = jnp.min(jnp.where(in_grp & (el == v1), lane, LANES), axis=-1, keepdims=True)
        rest = in_grp & (lane != i1)
        el2 = jnp.where(rest, logits, NEG)
        v2 = jnp.max(el2, axis=-1, keepdims=True)
        i2 = jnp.min(jnp.where(rest & (el2 == v2), lane, LANES), axis=-1, keepdims=True)
        e2 = jnp.exp(v2 - v1)
        w1 = p_grp / (1.0 + e2)
        w2 = p_grp * e2 / (1.0 + e2)
        gate_ref[...] = jnp.where(lane == i1, w1, 0.0) + jnp.where(lane == i2, w2, 0.0)

    gate_e = jnp.sum(jnp.where(lane == e + N_GROUPS, gate_ref[...], 0.0), axis=-1, keepdims=True)

    @pl.when(jnp.max(gate_e) > 0.0)
    def _():
        xb = xb_ref[...]
        hid = jax.nn.silu(_dot(xb, weg_ref[0])) * _dot(xb, weu_ref[0])
        acc_ref[...] += _dot((hid * gate_e).astype(BF16), wed_ref[0])

    @pl.when(e == N_EXPERTS - 1)
    def _():
        o_ref[...] = _layer_norm(alpha * x_ref[...] + acc_ref[...], g_ref[...], b_ref[...])


def moe_block(x, w_router, b_router, w_eg, w_eu, w_ed, g, b, alpha, tm):
    n, d = x.shape
    f = w_eg.shape[-1]
    return pl.pallas_call(
        functools.partial(_moe_kernel, alpha=alpha),
        grid=(n // tm, N_EXPERTS),
        in_specs=[pl.BlockSpec((tm, d), lambda i, e: (i, 0)),
                  pl.BlockSpec((d, LANES), lambda i, e: (0, 0)),
                  pl.BlockSpec((1, LANES), lambda i, e: (0, 0)),
                  pl.BlockSpec((1, d, f), lambda i, e: (e, 0, 0)),
                  pl.BlockSpec((1, d, f), lambda i, e: (e, 0, 0)),
                  pl.BlockSpec((1, f, d), lambda i, e: (e, 0, 0)),
                  pl.BlockSpec((1, d), lambda i, e: (0, 0)),
                  pl.BlockSpec((1, d), lambda i, e: (0, 0))],
        out_specs=pl.BlockSpec((tm, d), lambda i, e: (i, 0)),
        out_shape=jax.ShapeDtypeStruct((n, d), F32),
        scratch_shapes=[pltpu.VMEM((tm, d), BF16), pltpu.VMEM((tm, LANES), F32), pltpu.VMEM((tm, d), F32)],
        compiler_params=_params("arbitrary", "arbitrary"),
        name="moe_block",
    )(x, w_router, b_router, w_eg, w_eu, w_ed, g.reshape(1, d), b.reshape(1, d))


TQ = 128
BK = 128


def _log_sigmoid_pair(z):
    l1p = jnp.log1p(jnp.exp(-jnp.abs(z)))
    return jnp.minimum(z, 0.0) - l1p, jnp.minimum(-z, 0.0) - l1p


def _strict_upper_ones(n):
    return jnp.where(_iota((n, n), 0) > _iota((n, n), 1), 1.0, 0.0).astype(BF16)


def _strict_lower_ones(n):
    return jnp.where(_iota((n, n), 0) < _iota((n, n), 1), 1.0, 0.0).astype(BF16)


def _sb_kernel(q_ref, k_ref, v_ref, o_ref):
    qi = pl.program_id(1)
    r_rows = N_Q_HEADS * TQ
    qs = _stack_q(q_ref[0], HEAD_DIM ** -0.5).astype(BF16)
    qpos = qi * TQ + (_iota((r_rows, BK), 0) & (TQ - 1))
    col = _iota((r_rows, BK), 1)
    upper = _strict_upper_ones(BK)

    def body(it, carry):
        acc, run = carry
        kb = qi - it
        ks = pl.multiple_of(kb * BK, BK)
        k = k_ref[0, pl.ds(ks, BK), :].astype(BF16)
        v = v_ref[0, pl.ds(ks, BK), :].astype(BF16)
        z = _dot_nt(qs, k)
        causal = (ks + col) < qpos
        ls, ln = _log_sigmoid_pair(z)
        ln = jnp.where(causal, ln, 0.0)
        between = _dot_x_exact(ln, upper)
        a = jnp.where(causal, jnp.exp(ls + between + run), 0.0)
        acc = acc + _dot(a.astype(BF16), v)
        run = run + between[:, 0:1] + ln[:, 0:1]
        return acc, run

    acc, _ = lax.fori_loop(0, qi + 1, body,
                           (jnp.zeros((r_rows, LANES), F32), jnp.zeros((r_rows, 1), F32)))
    _store_heads(o_ref, acc, TQ)


def _h_spec(rows, tile, width=1):
    if rows is None:
        return lambda t: pl.BlockSpec((1, t, width * LANES), lambda b, i: (b, 0, tile // width))
    return pl.BlockSpec((1, rows, width * LANES), lambda b, i: (b, i, tile // width))


def sb_attention(h):
    bsz, t, _ = h.shape
    return pl.pallas_call(
        _sb_kernel,
        grid=(bsz, t // TQ),
        in_specs=[_h_spec(TQ, T_SBQ, 4), _h_spec(None, T_SB_K)(t), _h_spec(None, T_SB_V)(t)],
        out_specs=pl.BlockSpec((1, TQ, MIX_W), lambda b, i: (b, i, 0)),
        out_shape=jax.ShapeDtypeStruct((bsz, t, MIX_W), F32),
        compiler_params=_params("arbitrary", "arbitrary"),
        name="sb_attention",
    )(h, h, h)


def _fox_prep_kernel(m_ref, bf_ref, logf_ref, c_ref, ct_ref, carry_ref):
    i = pl.program_id(1)
    tc = m_ref.shape[1]

    @pl.when(i == 0)
    def _():
        carry_ref[...] = jnp.zeros_like(carry_ref)

    logf, _ = _log_sigmoid_pair(m_ref[0] + bf_ref[...])
    logf_ref[0] = logf
    incl = jnp.where(_iota((tc, tc), 0) >= _iota((tc, tc), 1), 1.0, 0.0).astype(BF16)
    c = _dot_exact_x(incl, logf) + carry_ref[...]
    c_ref[0] = c
    carry_ref[...] = c[tc - 1:tc, :]
    for s in range(tc // LANES):
        ct_ref[0, s] = c[s * LANES:(s + 1) * LANES, :].T[0:N_Q_HEADS, :]


def fox_prep(h, fox_b_f, tc=256):
    bsz, t, _ = h.shape
    bias = jnp.zeros((1, LANES), F32).at[0, MISC_FOXF:MISC_FOXF + N_Q_HEADS].set(fox_b_f)
    return pl.pallas_call(
        _fox_prep_kernel,
        grid=(bsz, t // tc),
        in_specs=[pl.BlockSpec((1, tc, LANES), lambda b, i: (b, i, T_MISC)),
                  pl.BlockSpec((1, LANES), lambda b, i: (0, 0))],
        out_specs=[pl.BlockSpec((1, tc, LANES), lambda b, i: (b, i, 0)),
                   pl.BlockSpec((1, tc, LANES), lambda b, i: (b, i, 0)),
                   pl.BlockSpec((1, tc // LANES, N_Q_HEADS, LANES), lambda b, i: (b, i, 0, 0))],
        out_shape=[jax.ShapeDtypeStruct((bsz, t, LANES), F32),
                   jax.ShapeDtypeStruct((bsz, t, LANES), F32),
                   jax.ShapeDtypeStruct((bsz, t // LANES, N_Q_HEADS, LANES), F32)],
        scratch_shapes=[pltpu.VMEM((1, LANES), F32)],
        compiler_params=_params("arbitrary", "arbitrary"),
        name="fox_prep",
    )(h, bias)


def _head_columns(x, lane0, stride=1):
    return jnp.concatenate([x[:, lane0 + stride * h:lane0 + stride * h + 1] for h in range(N_Q_HEADS)], axis=0)


def _softmax_step(z, mask, v, m, l, acc):
    if mask is not None:
        z = jnp.where(mask, z, NEG)
    m_new = jnp.maximum(m, jnp.max(z, axis=-1, keepdims=True))
    p = jnp.exp(z - m_new)
    if mask is not None:
        p = jnp.where(mask, p, 0.0)
    alpha = jnp.exp(m - m_new)
    l = alpha * l + jnp.sum(p, axis=-1, keepdims=True)
    acc = alpha * acc + _dot(p.astype(BF16), v)
    return m_new, l, acc


def _softmax_init(rows):
    return (jnp.full((rows, 1), NEG, F32), jnp.zeros((rows, 1), F32), jnp.zeros((rows, LANES), F32))


def _fox_kernel(q_ref, k_ref, v_ref, cq_ref, ck_ref, o_ref):
    qi = pl.program_id(1)
    r_rows = N_Q_HEADS * TQ
    qs = _stack_q(q_ref[0], HEAD_DIM ** -0.5).astype(BF16)
    cq = _head_columns(cq_ref[0], 0)
    qpos = qi * TQ + (_iota((r_rows, BK), 0) & (TQ - 1))
    col = _iota((r_rows, BK), 1)

    def body(kb, carry):
        ks = pl.multiple_of(kb * BK, BK)
        k = k_ref[0, pl.ds(ks, BK), :].astype(BF16)
        v = v_ref[0, pl.ds(ks, BK), :].astype(BF16)
        ck = ck_ref[0, kb]
        cks = jnp.concatenate([jnp.broadcast_to(ck[h:h + 1, :], (TQ, BK)) for h in range(N_Q_HEADS)], axis=0)
        z = _dot_nt(qs, k) + (cq - cks)
        return _softmax_step(z, (ks + col) <= qpos, v, *carry)

    _, l, acc = lax.fori_loop(0, qi + 1, body, _softmax_init(r_rows))
    _store_heads(o_ref, acc / jnp.maximum(l, 1e-30), TQ)


def fox_attention(h, c_tok, c_tiles):
    bsz, t, _ = h.shape
    return pl.pallas_call(
        _fox_kernel,
        grid=(bsz, t // TQ),
        in_specs=[_h_spec(TQ, T_FOXQ, 4), _h_spec(None, T_FOX_K)(t), _h_spec(None, T_FOX_V)(t),
                  pl.BlockSpec((1, TQ, LANES), lambda b, i: (b, i, 0)),
                  pl.BlockSpec((1, t // LANES, N_Q_HEADS, LANES), lambda b, i: (b, 0, 0, 0))],
        out_specs=pl.BlockSpec((1, TQ, MIX_W), lambda b, i: (b, i, 0)),
        out_shape=jax.ShapeDtypeStruct((bsz, t, MIX_W), F32),
        compiler_params=_params("arbitrary", "arbitrary"),
        name="fox_attention",
    )(h, h, h, c_tok, c_tiles)


def _nsa_cmp_kernel(k_ref, v_ref, pek_ref, pev_ref, wk_ref, wv_ref, ok_ref, ov_ref, *, nblk):
    for src, pe, w, dst in ((k_ref, pek_ref, wk_ref, ok_ref), (v_ref, pev_ref, wv_ref, ov_ref)):
        rows = src[0].reshape(nblk, NSA_BLOCK, LANES) + pe[...][None]
        mean = jnp.sum(rows, axis=1) * (1.0 / NSA_BLOCK)
        dst[0] = jnp.zeros(dst.shape[1:], F32)
        dst[0, 0:nblk, :] = _dot3(mean, w[...])


def nsa_compress(h, pe_k2, pe_v2, w_ck2, w_cv2):
    bsz, t, _ = h.shape
    nblk = t // NSA_BLOCK
    nblk_pad = -(-nblk // LANES) * LANES
    full = lambda shape: pl.BlockSpec(shape, lambda b: (0,) * len(shape))
    return pl.pallas_call(
        functools.partial(_nsa_cmp_kernel, nblk=nblk),
        grid=(bsz,),
        in_specs=[pl.BlockSpec((1, t, LANES), lambda b: (b, 0, T_NSA_KC)),
                  pl.BlockSpec((1, t, LANES), lambda b: (b, 0, T_NSA_VC)),
                  full((NSA_BLOCK, LANES)), full((NSA_BLOCK, LANES)),
                  full((LANES, LANES)), full((LANES, LANES))],
        out_specs=[pl.BlockSpec((1, nblk_pad, LANES), lambda b: (b, 0, 0))] * 2,
        out_shape=[jax.ShapeDtypeStruct((bsz, nblk_pad, LANES), F32)] * 2,
        compiler_params=_params("arbitrary"),
        name="nsa_compress",
    )(h, h, pe_k2, pe_v2, w_ck2, w_cv2)


def _topk_block_mask(imp_t, nblk, k_sel):
    n_idx = _iota(imp_t.shape, 0)
    rank = jnp.zeros(imp_t.shape, F32)
    for m in range(nblk):
        row = imp_t[m:m + 1, :]
        before = jnp.where(n_idx > m, 1.0, 0.0)
        rank = rank + jnp.where(row > imp_t, 1.0, jnp.where(row == imp_t, before, 0.0))
    return jnp.where(rank < k_sel, jnp.where(imp_t > -jnp.inf, 1.0, 0.0), 0.0)


def _nsa_kernel(q_ref, ck_ref, cv_ref, ks_ref, vs_ref, kw_ref, vw_ref, g_ref, o_ref, *, nblk, k_sel):
    qi = pl.program_id(1)
    r_rows = N_Q_HEADS * TQ
    half = GQA_GROUP * TQ
    qf = _stack_q(q_ref[0], HEAD_DIM ** -0.5)
    qs = qf.astype(BF16)
    qpos = qi * TQ + (_iota((r_rows, BK), 0) & (TQ - 1))
    col = _iota((r_rows, BK), 1)

    n_ctile = ck_ref.shape[1] // LANES
    sel_tiles = []
    state = _softmax_init(r_rows)
    zc_tiles, vis_tiles = [], []
    for c in range(n_ctile):
        blk = c * LANES + col
        vis = ((blk + 1) * NSA_BLOCK - 1 <= qpos) & (blk < nblk)
        zc = _dot3_nt(qf, ck_ref[0, c * LANES:(c + 1) * LANES, :])
        zc_tiles.append(zc)
        vis_tiles.append(vis)
        m_run = jnp.maximum(state[0], jnp.max(jnp.where(vis, zc, NEG), axis=-1, keepdims=True))
        state = (m_run,) + state[1:]
    m_c = state[0]
    e_tiles = [jnp.where(vis, jnp.exp(zc - m_c), 0.0) for zc, vis in zip(zc_tiles, vis_tiles)]
    denom = e_tiles[0].sum(axis=-1, keepdims=True)
    for e in e_tiles[1:]:
        denom = denom + e.sum(axis=-1, keepdims=True)
    denom = jnp.maximum(denom, 1e-30)
    o_cmp = jnp.zeros((r_rows, LANES), F32)
    tq_pos = qi * TQ + _iota((TQ, LANES), 0)
    tq_col = _iota((TQ, LANES), 1)
    for c in range(n_ctile):
        pc = e_tiles[c] / denom
        o_cmp = o_cmp + _dot(pc.astype(BF16), cv_ref[0, c * LANES:(c + 1) * LANES, :].astype(BF16))
        blk = c * LANES + tq_col
        vis_t = ((blk + 1) * NSA_BLOCK - 1 <= tq_pos) & (blk < nblk)
        forced = (blk == (tq_pos >> 6)) | (blk == 0)
        per_group = []
        for g in range(N_KV_HEADS):
            imp = pc[g * half:g * half + TQ]
            for r in range(1, GQA_GROUP):
                imp = imp + pc[g * half + r * TQ:g * half + (r + 1) * TQ]
            imp = jnp.where(forced, jnp.inf, jnp.where(vis_t, imp, -jnp.inf))
            per_group.append(imp)
        sel_tiles.append(per_group)

    sel = []
    for g in range(N_KV_HEADS):
        imp_t = jnp.concatenate([sel_tiles[c][g].T for c in range(n_ctile)], axis=0)[0:nblk]
        chosen = _topk_block_mask(imp_t, nblk, k_sel)
        pad = n_ctile * LANES - nblk
        if pad:
            chosen = jnp.concatenate([chosen, jnp.zeros((pad, TQ), F32)], axis=0)
        sel.append([chosen[c * LANES:(c + 1) * LANES].T.astype(BF16) for c in range(n_ctile)])

    blocks_per_tile = BK // NSA_BLOCK
    e_row = _iota((LANES, BK), 0)
    e_col = _iota((LANES, BK), 1) >> 6

    def sel_body(kb, carry):
        ks = pl.multiple_of(kb * BK, BK)
        k = ks_ref[0, pl.ds(ks, BK), :].astype(BF16)
        v = vs_ref[0, pl.ds(ks, BK), :].astype(BF16)
        masks = []
        for g in range(N_KV_HEADS):
            hit = None
            for c in range(n_ctile):
                expand = jnp.where(e_row + c * LANES == kb * blocks_per_tile + e_col, 1.0, 0.0).astype(BF16)
                part = _dot(sel[g][c], expand)
                hit = part if hit is None else hit + part
            masks.append(_rep_rows(hit, GQA_GROUP))
        chosen = jnp.concatenate(masks, axis=0) > 0.5
        mask = chosen & ((ks + col) <= qpos)
        return _softmax_step(_dot_nt(qs, k), mask, v, *carry)

    _, l_s, acc_s = lax.fori_loop(0, qi + 1, sel_body, _softmax_init(r_rows))
    o_sel = acc_s / jnp.maximum(l_s, 1e-30)

    def win_body(kb, carry):
        ks = pl.multiple_of(kb * BK, BK)
        k = kw_ref[0, pl.ds(ks, BK), :].astype(BF16)
        v = vw_ref[0, pl.ds(ks, BK), :].astype(BF16)
        kpos = ks + col
        mask = (kpos <= qpos) & (kpos >= qpos - NSA_WINDOW)
        return _softmax_step(_dot_nt(qs, k), mask, v, *carry)

    first = jnp.maximum(qi - NSA_WINDOW // BK, 0)
    _, l_w, acc_w = lax.fori_loop(first, qi + 1, win_body, _softmax_init(r_rows))
    o_win = acc_w / jnp.maximum(l_w, 1e-30)

    gates = jax.nn.sigmoid(g_ref[0])
    o = (_head_columns(gates, MISC_NSAG, 3) * o_cmp + _head_columns(gates, MISC_NSAG + 1, 3) * o_sel
         + _head_columns(gates, MISC_NSAG + 2, 3) * o_win)
    _store_heads(o_ref, o, TQ)


def nsa_attention(h, cmp_k, cmp_v):
    bsz, t, _ = h.shape
    nblk = t // NSA_BLOCK
    nblk_pad = cmp_k.shape[1]
    return pl.pallas_call(
        functools.partial(_nsa_kernel, nblk=nblk, k_sel=min(NSA_TOPK, nblk)),
        grid=(bsz, t // TQ),
        in_specs=[_h_spec(TQ, T_NSAQ, 4),
                  pl.BlockSpec((1, nblk_pad, LANES), lambda b, i: (b, 0, 0)),
                  pl.BlockSpec((1, nblk_pad, LANES), lambda b, i: (b, 0, 0)),
                  _h_spec(None, T_NSA_KS)(t), _h_spec(None, T_NSA_VS)(t),
                  _h_spec(None, T_NSA_KW)(t), _h_spec(None, T_NSA_VW)(t),
                  _h_spec(TQ, T_MISC)],
        out_specs=pl.BlockSpec((1, TQ, MIX_W), lambda b, i: (b, i, 0)),
        out_shape=jax.ShapeDtypeStruct((bsz, t, MIX_W), F32),
        compiler_params=_params("arbitrary", "arbitrary"),
        name="nsa_attention",
    )(h, cmp_k, cmp_v, h, h, h, h, h)


def _order_key(x):
    bits = lax.bitcast_convert_type(x + 0.0, I32)
    return bits ^ ((bits >> 31) & 0x7FFFFFFF)


INT_MIN = -2 ** 31
NEG_INF_KEY = -2139095041


def _kth_largest_key(count_ge, rows, k):
    def bit_body(it, kappa):
        bit = lax.shift_left(jnp.int32(1), 31 - it)
        cand = kappa | bit
        cnt = count_ge(cand ^ INT_MIN)
        return jnp.where(cnt >= k, cand, kappa)

    kappa = lax.fori_loop(0, 32, bit_body, jnp.zeros((rows, 1), I32))
    return kappa ^ INT_MIN


def _dsa_kernel(q_ref, iq_ref, ik_ref, w_ref, k_ref, v_ref, o_ref, key_ref, *, topk):
    qi = pl.program_id(1)
    r_rows = N_Q_HEADS * TQ
    qs = _stack_q(q_ref[0], HEAD_DIM ** -0.5).astype(BF16)
    iq = iq_ref[0]
    low = _iota((TQ, LANES), 1) < HEAD_DIM
    iq_rows = jnp.concatenate(
        [jnp.where(low if h % 2 == 0 else jnp.logical_not(low), iq[:, (h // 2) * LANES:(h // 2 + 1) * LANES], 0.0)
         for h in range(IDX_HEADS)], axis=0)
    w = w_ref[0]
    t_pos = qi * TQ + _iota((TQ, BK), 0)
    t_col = _iota((TQ, BK), 1)
    n_tiles = qi + 1

    def score_body(kb, _):
        ks = pl.multiple_of(kb * BK, BK)
        sc = jnp.maximum(_dot3_nt(iq_rows, ik_ref[0, pl.ds(ks, BK), :]), 0.0)
        tot = w[:, MISC_IDXW:MISC_IDXW + 1] * sc[0:TQ]
        for h in range(1, IDX_HEADS):
            tot = tot + w[:, MISC_IDXW + h:MISC_IDXW + h + 1] * sc[h * TQ:(h + 1) * TQ]
        tot = jnp.where(ks + t_col <= t_pos, tot, -jnp.inf)
        key_ref[kb] = _order_key(tot)
        return 0

    lax.fori_loop(0, n_tiles, score_body, 0)

    def count_where(pred):
        def body(kb, acc):
            return acc + jnp.where(pred(key_ref[kb]), 1.0, 0.0)
        return jnp.sum(lax.fori_loop(0, n_tiles, body, jnp.zeros((TQ, BK), F32)), axis=-1, keepdims=True)

    kappa = _kth_largest_key(lambda c: count_where(lambda key: key >= c), TQ, float(topk))
    need = float(topk) - count_where(lambda key: key > kappa)
    neg_inf_key = NEG_INF_KEY
    lower = _strict_lower_ones(BK)
    qpos = qi * TQ + (_iota((r_rows, BK), 0) & (TQ - 1))
    col = _iota((r_rows, BK), 1)

    def attn_body(kb, carry):
        ties_before, m, l, acc = carry
        ks = pl.multiple_of(kb * BK, BK)
        key = key_ref[kb]
        tie = jnp.where(key == kappa, 1.0, 0.0)
        rank = _dot(tie.astype(BF16), lower) + ties_before
        chosen = jnp.where(key > kappa, 1.0, jnp.where(rank < need, tie, 0.0))
        chosen = jnp.where(key > neg_inf_key, chosen, 0.0)
        ties_before = ties_before + jnp.sum(tie, axis=-1, keepdims=True)
        mask = _rep_rows(chosen, N_Q_HEADS) > 0.5
        k = k_ref[0, pl.ds(ks, BK), :].astype(BF16)
        v = v_ref[0, pl.ds(ks, BK), :].astype(BF16)
        return (ties_before,) + _softmax_step(_dot_nt(qs, k), mask, v, m, l, acc)

    _, _, l, acc = lax.fori_loop(0, n_tiles, attn_body, (jnp.zeros((TQ, 1), F32),) + _softmax_init(r_rows))
    _store_heads(o_ref, acc / jnp.maximum(l, 1e-30), TQ)


def dsa_attention(h):
    bsz, t, _ = h.shape
    return pl.pallas_call(
        functools.partial(_dsa_kernel, topk=min(DSA_TOPK_MAX, t // 4)),
        grid=(bsz, t // TQ),
        in_specs=[_h_spec(TQ, T_DSAQ, 4), _h_spec(TQ, T_IDXQ, 2), _h_spec(None, T_IDXK)(t),
                  _h_spec(TQ, T_MISC), _h_spec(None, T_DSA_K)(t), _h_spec(None, T_DSA_V)(t)],
        out_specs=pl.BlockSpec((1, TQ, MIX_W), lambda b, i: (b, i, 0)),
        out_shape=jax.ShapeDtypeStruct((bsz, t, MIX_W), F32),
        scratch_shapes=[pltpu.VMEM((t // BK, TQ, BK), I32)],
        compiler_params=_params("arbitrary", "arbitrary"),
        name="dsa_attention",
    )(h, h, h, h, h, h)


PAGES_PER_STEP = 8


def _page_specs(width, col, layer, n_chunks, reverse):
    def spec(p):
        def index(b, c, pt_ref, *_):
            chunk = (n_chunks - 1 - c) if reverse else c
            return (layer, pt_ref[b, chunk * PAGES_PER_STEP + p], 0, col)
        return index
    return lambda rows: [pl.BlockSpec((1, 1, rows, width), spec(p)) for p in range(PAGES_PER_STEP)]


def _row_of(ref, b):
    return ref[pl.ds(b, 1), :]


def _diag_column(row, lane0, stride):
    wide = jnp.broadcast_to(row, (N_Q_HEADS, LANES))
    pick = _iota((N_Q_HEADS, LANES), 1) == lane0 + stride * _iota((N_Q_HEADS, LANES), 0)
    return jnp.sum(jnp.where(pick, wide, 0.0), axis=-1, keepdims=True)


def _softmax_token(z, v_row, m, l, acc):
    m_new = jnp.maximum(m, z)
    alpha = jnp.exp(m - m_new)
    p = jnp.exp(z - m_new)
    return m_new, alpha * l + p, alpha * acc + p * v_row


def _sb_dec_kernel(pt_ref, q_ref, *refs):
    pages = refs[:PAGES_PER_STEP]
    o_ref, acc_ref, run_ref = refs[PAGES_PER_STEP:]
    c = pl.program_id(1)

    @pl.when(c == 0)
    def _():
        acc_ref[...] = jnp.zeros_like(acc_ref)
        run_ref[...] = jnp.zeros_like(run_ref)

    qs = (q_ref[0] * HEAD_DIM ** -0.5).astype(BF16)
    upper = _strict_upper_ones(LANES)
    acc = acc_ref[...]
    run = run_ref[:, 0:1]
    for p in reversed(range(PAGES_PER_STEP)):
        page = pages[p][0, 0]
        z = _dot_nt(qs, page[:, 0:KV_W].astype(BF16))
        ls, ln = _log_sigmoid_pair(z)
        between = _dot_x_exact(ln, upper)
        a = jnp.exp(ls + between + run)
        acc = acc + _dot(a.astype(BF16), page[:, KV_W:2 * KV_W].astype(BF16))
        run = run + between[:, 0:1] + ln[:, 0:1]
    acc_ref[...] = acc
    run_ref[...] = jnp.broadcast_to(run, run_ref.shape)

    @pl.when(c == pl.num_programs(1) - 1)
    def _():
        o_ref[0] = acc


def _decode_call(kernel, name, page_table, n_chunks, in_specs, args, out_specs, out_shape, scratch, n_prefetch=1,
                 grid=None, prefetch=None):
    bsz = page_table.shape[0]
    return pl.pallas_call(
        kernel,
        grid_spec=pltpu.PrefetchScalarGridSpec(
            num_scalar_prefetch=n_prefetch,
            grid=grid or (bsz, n_chunks),
            in_specs=in_specs,
            out_specs=out_specs,
            scratch_shapes=scratch),
        out_shape=out_shape,
        compiler_params=_params("arbitrary", "arbitrary"),
        name=name,
    )(*(prefetch or (page_table,)), *args)


def _q_spec():
    return pl.BlockSpec((1, N_Q_HEADS, LANES), lambda b, c, *_: (b, 0, 0))


def _hs_spec(bsz, tile):
    return pl.BlockSpec((bsz, LANES), lambda b, c, *_: (0, tile))


def _head_out(bsz):
    return (pl.BlockSpec((1, N_Q_HEADS, LANES), lambda b, c, *_: (b, 0, 0)),
            jax.ShapeDtypeStruct((bsz, N_Q_HEADS, LANES), F32))


def sb_decode(q, cache, layer, page_table):
    bsz, n_pages = page_table.shape
    n_chunks = n_pages // PAGES_PER_STEP
    rows = cache.shape[2]
    out_spec, out_shape = _head_out(bsz)
    return _decode_call(
        _sb_dec_kernel, "sb_decode", page_table, n_chunks,
        [_q_spec()] + _page_specs(2 * KV_W, 0, layer, n_chunks, True)(rows),
        [q] + [cache] * PAGES_PER_STEP, out_spec, out_shape,
        [pltpu.VMEM((N_Q_HEADS, LANES), F32), pltpu.VMEM((N_Q_HEADS, LANES), F32)])


def _fox_dec_kernel(pt_ref, q_ref, kn_ref, vn_ref, misc_ref, bf_ref, *refs):
    pages = refs[:PAGES_PER_STEP]
    lf_pages = refs[PAGES_PER_STEP:2 * PAGES_PER_STEP]
    o_ref, lf_out_ref, m_ref, l_ref, acc_ref, run_ref = refs[2 * PAGES_PER_STEP:]
    b = pl.program_id(0)
    c = pl.program_id(1)
    qf = q_ref[0] * HEAD_DIM ** -0.5

    @pl.when(c == 0)
    def _():
        logf_row, _ = _log_sigmoid_pair(_row_of(misc_ref, b) + bf_ref[...])
        lf_out_ref[0] = logf_row
        z_new = jnp.sum(qf * _row_of(kn_ref, b), axis=-1, keepdims=True)
        m_ref[...] = jnp.broadcast_to(z_new, m_ref.shape)
        l_ref[...] = jnp.ones_like(l_ref)
        acc_ref[...] = jnp.broadcast_to(_row_of(vn_ref, b), acc_ref.shape)
        run_ref[...] = jnp.broadcast_to(_diag_column(logf_row, MISC_FOXF, 1), run_ref.shape)

    qs = qf.astype(BF16)
    upper = _strict_upper_ones(LANES)
    state = (m_ref[:, 0:1], l_ref[:, 0:1], acc_ref[...])
    run = run_ref[:, 0:1]
    for p in reversed(range(PAGES_PER_STEP)):
        page = pages[p][0, 0]
        lf = lf_pages[p][0, 0]
        later = _dot_x_exact(lf, upper)
        z = _dot_nt(qs, page[:, 0:KV_W].astype(BF16)) + (later + run)
        state = _softmax_step(z, None, page[:, KV_W:2 * KV_W].astype(BF16), *state)
        run = run + later[:, 0:1] + lf[:, 0:1]
    m_ref[...] = jnp.broadcast_to(state[0], m_ref.shape)
    l_ref[...] = jnp.broadcast_to(state[1], l_ref.shape)
    acc_ref[...] = state[2]
    run_ref[...] = jnp.broadcast_to(run, run_ref.shape)

    @pl.when(c == pl.num_programs(1) - 1)
    def _():
        o_ref[0] = state[2] / state[1]


def fox_decode(q, hs, bias_row, cache, logf_t, layer, page_table):
    bsz, n_pages = page_table.shape
    n_chunks = n_pages // PAGES_PER_STEP
    rows = cache.shape[2]
    out_spec, out_shape = _head_out(bsz)
    stat = pltpu.VMEM((N_Q_HEADS, LANES), F32)
    return _decode_call(
        _fox_dec_kernel, "fox_decode", page_table, n_chunks,
        [_q_spec(), _hs_spec(bsz, T_FOX_K), _hs_spec(bsz, T_FOX_V), _hs_spec(bsz, T_MISC),
         pl.BlockSpec((1, LANES), lambda b, c, *_: (0, 0))]
        + _page_specs(2 * KV_W, 0, layer, n_chunks, True)(rows)
        + _page_specs(logf_t.shape[3], 0, layer, n_chunks, True)(N_Q_HEADS),
        [q, hs, hs, hs, bias_row] + [cache] * PAGES_PER_STEP + [logf_t] * PAGES_PER_STEP,
        [out_spec, pl.BlockSpec((1, 1, LANES), lambda b, c, *_: (b, 0, 0))],
        [out_shape, jax.ShapeDtypeStruct((bsz, 1, LANES), F32)],
        [stat, stat, stat, stat])


def _nsa_dec_cmp_kernel(pt_ref, q_ref, pek_ref, pev_ref, wk_ref, wv_ref, *refs, n_blocks_past, k_sel):
    pages = refs[:PAGES_PER_STEP]
    ocmp_ref, sel_ref, ck_ref, cv_ref = refs[PAGES_PER_STEP:]
    c = pl.program_id(1)
    blocks_per_page = pages[0].shape[2] // NSA_BLOCK
    step_blocks = PAGES_PER_STEP * blocks_per_page
    means_k, means_v = [], []
    for p in range(PAGES_PER_STEP):
        page = pages[p][0, 0]
        means_k.append(jnp.sum(page[:, 0:KV_W].reshape(blocks_per_page, NSA_BLOCK, LANES) + pek_ref[...][None],
                               axis=1) * (1.0 / NSA_BLOCK))
        means_v.append(jnp.sum(page[:, KV_W:2 * KV_W].reshape(blocks_per_page, NSA_BLOCK, LANES) + pev_ref[...][None],
                               axis=1) * (1.0 / NSA_BLOCK))
    row0 = pl.multiple_of(c * step_blocks, step_blocks)
    ck_ref[pl.ds(row0, step_blocks), :] = _dot3(jnp.concatenate(means_k, axis=0), wk_ref[...])
    cv_ref[pl.ds(row0, step_blocks), :] = _dot3(jnp.concatenate(means_v, axis=0), wv_ref[...])

    @pl.when(c == pl.num_programs(1) - 1)
    def _():
        nb_pad = ck_ref.shape[0]
        n_tiles = nb_pad // LANES + 1
        qpos = n_blocks_past * NSA_BLOCK
        qf = q_ref[0] * HEAD_DIM ** -0.5
        zc = _dot3_nt(qf, ck_ref[...])
        blk = _iota(zc.shape, 1)
        vis = ((blk + 1) * NSA_BLOCK - 1 <= qpos) & (blk < n_blocks_past)
        m = jnp.max(jnp.where(vis, zc, NEG), axis=-1, keepdims=True)
        e = jnp.where(vis, jnp.exp(zc - m), 0.0)
        pc = e / jnp.maximum(jnp.sum(e, axis=-1, keepdims=True), 1e-30)
        ocmp_ref[0] = _dot(pc.astype(BF16), cv_ref[...].astype(BF16))
        n_blocks = n_blocks_past + 1
        rows = []
        for g in range(N_KV_HEADS):
            imp = jnp.sum(pc[g * GQA_GROUP:(g + 1) * GQA_GROUP], axis=0, keepdims=True)
            imp = jnp.where(vis[0:1], imp, -jnp.inf)
            rows.append(jnp.concatenate([imp, jnp.full((1, LANES), -jnp.inf, F32)], axis=1))
        imp_r = jnp.concatenate(rows + [jnp.full((LANES - N_KV_HEADS, n_tiles * LANES), -jnp.inf, F32)], axis=0)
        blk_r = _iota(imp_r.shape, 1)
        forced = (blk_r == qpos // NSA_BLOCK) | (blk_r == 0)
        imp_r = jnp.where(forced, jnp.inf, imp_r)
        imp_c = jnp.concatenate([imp_r[:, t * LANES:(t + 1) * LANES].T for t in range(n_tiles)], axis=0)
        n_col = _iota((n_tiles * LANES, LANES), 0)
        r_lane = _iota((n_tiles * LANES, LANES), 1)
        sel_ref[0] = jnp.zeros(sel_ref.shape[1:], I32)
        for g in range(N_KV_HEADS):
            col = imp_c[:, g:g + 1]
            rank = jnp.zeros((n_tiles * LANES, 1), F32)
            for t in range(n_tiles):
                row = imp_r[g:g + 1, t * LANES:(t + 1) * LANES]
                m_idx = t * LANES + _iota((n_tiles * LANES, LANES), 1)
                ahead = jnp.where(row > col, 1.0, jnp.where(row == col, jnp.where(m_idx < n_col, 1.0, 0.0), 0.0))
                ahead = jnp.where(m_idx < n_blocks, ahead, 0.0)
                rank = rank + jnp.sum(ahead, axis=-1, keepdims=True)
            hit = jnp.where((rank == r_lane.astype(F32)) & (n_col < n_blocks), 1.0, 0.0)
            idx_row = jnp.sum(hit * n_col.astype(F32), axis=0, keepdims=True)
            ok_row = jnp.sum(jnp.where(col > -jnp.inf, hit, 0.0), axis=0, keepdims=True)
            ok_row = jnp.where(_iota((1, LANES), 1) < k_sel, ok_row, 0.0)
            sel_ref[0, g:g + 1, :] = idx_row.astype(I32)
            sel_ref[0, N_KV_HEADS + g:N_KV_HEADS + g + 1, :] = ok_row.astype(I32)


def nsa_decode_compress(q, pe_k2, pe_v2, w_ck2, w_cv2, cache, layer, page_table):
    bsz, n_pages = page_table.shape
    n_chunks = n_pages // PAGES_PER_STEP
    rows = cache.shape[2]
    n_blocks_past = n_pages * rows // NSA_BLOCK
    nb_pad = -(-n_blocks_past // LANES) * LANES
    assert nb_pad == n_blocks_past
    k_sel = min(NSA_TOPK, n_blocks_past + 1)
    out_spec, out_shape = _head_out(bsz)
    full = lambda shape: pl.BlockSpec(shape, lambda b, c, *_: (0,) * len(shape))
    return _decode_call(
        functools.partial(_nsa_dec_cmp_kernel, n_blocks_past=n_blocks_past, k_sel=k_sel),
        "nsa_decode_compress", page_table, n_chunks,
        [_q_spec(), full((NSA_BLOCK, LANES)), full((NSA_BLOCK, LANES)), full((LANES, LANES)), full((LANES, LANES))]
        + _page_specs(2 * KV_W, 0, layer, n_chunks, False)(rows),
        [q, pe_k2, pe_v2, w_ck2, w_cv2] + [cache] * PAGES_PER_STEP,
        [out_spec, pl.BlockSpec((1, N_Q_HEADS, LANES), lambda b, c, *_: (b, 0, 0))],
        [out_shape, jax.ShapeDtypeStruct((bsz, N_Q_HEADS, LANES), I32)],
        [pltpu.VMEM((nb_pad, LANES), F32), pltpu.VMEM((nb_pad, LANES), F32)])


def _nsa_dec_sel_kernel(pt_ref, blk_ref, ok_ref, q_ref, kn_ref, vn_ref, page_ref, o_ref, m_ref, l_ref, acc_ref,
                        *, n_blocks_past, k_sel):
    b = pl.program_id(0)
    s = pl.program_id(1)
    qf = q_ref[0] * HEAD_DIM ** -0.5

    @pl.when(s == 0)
    def _():
        m_ref[...] = jnp.full(m_ref.shape, NEG, F32)
        l_ref[...] = jnp.zeros_like(l_ref)
        acc_ref[...] = jnp.zeros_like(acc_ref)

    blk = blk_ref[b, s]
    g = s // k_sel
    valid = (ok_ref[b, s] > 0) & (blk < n_blocks_past)
    blocks_per_page = page_ref.shape[2] // NSA_BLOCK
    r0 = pl.multiple_of((blk % blocks_per_page) * NSA_BLOCK, NSA_BLOCK)
    rows = page_ref[0, 0, pl.ds(r0, NSA_BLOCK), :]
    z = _dot_nt(qf.astype(BF16), rows[:, 0:KV_W].astype(BF16))
    mask = (_iota(z.shape, 0) >> 2) == jnp.where(valid, g, -1)
    state = _softmax_step(z, mask, rows[:, KV_W:2 * KV_W].astype(BF16), m_ref[:, 0:1], l_ref[:, 0:1], acc_ref[...])

    @pl.when(s < pl.num_programs(1) - 1)
    def _():
        m_ref[...] = jnp.broadcast_to(state[0], m_ref.shape)
        l_ref[...] = jnp.broadcast_to(state[1], l_ref.shape)
        acc_ref[...] = state[2]

    @pl.when(s == pl.num_programs(1) - 1)
    def _():
        z_new = jnp.sum(qf * _row_of(kn_ref, b), axis=-1, keepdims=True)
        _, l, acc = _softmax_token(z_new, _row_of(vn_ref, b), *state)
        o_ref[0] = acc / l


def nsa_decode_select(q, hs, sel, cache, layer, page_table):
    bsz, n_pages = page_table.shape
    rows = cache.shape[2]
    n_blocks_past = n_pages * rows // NSA_BLOCK
    k_sel = min(NSA_TOPK, n_blocks_past + 1)
    blocks_per_page = rows // NSA_BLOCK
    blk = sel[:, 0:N_KV_HEADS, 0:k_sel].reshape(bsz, N_KV_HEADS * k_sel)
    ok = sel[:, N_KV_HEADS:2 * N_KV_HEADS, 0:k_sel].reshape(bsz, N_KV_HEADS * k_sel)
    out_spec, out_shape = _head_out(bsz)

    def page_index(b, s, pt_ref, blk_ref, ok_ref):
        page = jnp.minimum(blk_ref[b, s] // blocks_per_page, n_pages - 1)
        return (layer, pt_ref[b, page], 0, 1)

    stat = pltpu.VMEM((N_Q_HEADS, LANES), F32)
    return _decode_call(
        functools.partial(_nsa_dec_sel_kernel, n_blocks_past=n_blocks_past, k_sel=k_sel),
        "nsa_decode_select", page_table, None,
        [_q_spec(), _hs_spec(bsz, T_NSA_KS), _hs_spec(bsz, T_NSA_VS),
         pl.BlockSpec((1, 1, rows, 2 * KV_W), page_index)],
        [q, hs, hs, cache], out_spec, out_shape, [stat, stat, stat],
        n_prefetch=3, grid=(bsz, N_KV_HEADS * k_sel), prefetch=(page_table, blk, ok))


def _nsa_dec_win_kernel(q_ref, kn_ref, vn_ref, misc_ref, win_ref, ocmp_ref, osel_ref, o_ref):
    b = pl.program_id(0)
    qf = q_ref[0] * HEAD_DIM ** -0.5
    win = win_ref[0, 0]
    z = _dot_nt(qf.astype(BF16), win[:, 0:KV_W].astype(BF16))
    state = _softmax_step(z, None, win[:, KV_W:2 * KV_W].astype(BF16), *_softmax_init(N_Q_HEADS))
    z_new = jnp.sum(qf * _row_of(kn_ref, b), axis=-1, keepdims=True)
    _, l, acc = _softmax_token(z_new, _row_of(vn_ref, b), *state)
    gates = jax.nn.sigmoid(_row_of(misc_ref, b))
    o_ref[0] = (_diag_column(gates, MISC_NSAG, 3) * ocmp_ref[0] + _diag_column(gates, MISC_NSAG + 1, 3) * osel_ref[0]
                + _diag_column(gates, MISC_NSAG + 2, 3) * (acc / l))


def nsa_decode_combine(q, hs, win_state, layer, o_cmp, o_sel):
    bsz = q.shape[0]
    w_rows = win_state.shape[2]
    head = pl.BlockSpec((1, N_Q_HEADS, LANES), lambda b: (b, 0, 0))
    tile = lambda t: pl.BlockSpec((bsz, LANES), lambda b: (0, t))
    return pl.pallas_call(
        _nsa_dec_win_kernel,
        grid=(bsz,),
        in_specs=[head, tile(T_NSA_KW), tile(T_NSA_VW), tile(T_MISC),
                  pl.BlockSpec((1, 1, w_rows, 2 * KV_W), lambda b: (layer, b, 0, 0)), head, head],
        out_specs=head,
        out_shape=jax.ShapeDtypeStruct((bsz, N_Q_HEADS, LANES), F32),
        compiler_params=_params("arbitrary"),
        name="nsa_decode_combine",
    )(q, hs, hs, hs, win_state, o_cmp, o_sel)


def _dsa_dec_index_kernel(pt_ref, iq_ref, w_ref, kn_ref, *refs, topk):
    pages = refs[:PAGES_PER_STEP]
    mask_ref, flag_ref, key_ref = refs[PAGES_PER_STEP:]
    b = pl.program_id(0)
    c = pl.program_id(1)
    iq = iq_ref[0]
    w = w_ref[0]
    for p in range(PAGES_PER_STEP):
        sc = jnp.maximum(_dot3_nt(iq, pages[p][0, 0]), 0.0)
        tot = jnp.sum(w * sc, axis=0, keepdims=True)
        key_ref[pl.ds(c * PAGES_PER_STEP + p, 1), :] = _order_key(tot)

    @pl.when(c == pl.num_programs(1) - 1)
    def _():
        k_new = _row_of(kn_ref, b)[:, 0:IDX_DIM]
        sc_new = jnp.maximum(jnp.sum(iq * k_new, axis=-1, keepdims=True), 0.0)
        key_new = _order_key(jnp.sum(w[:, 0:1] * sc_new, axis=0, keepdims=True))
        keys = key_ref[...]

        def total(x):
            return jnp.sum(jnp.sum(x, axis=-1, keepdims=True), axis=0, keepdims=True)

        def count(pred):
            return total(jnp.where(pred(keys), 1.0, 0.0)) + jnp.where(pred(key_new), 1.0, 0.0)

        k = float(topk)
        kappa = _kth_largest_key(lambda cand: count(lambda x: x >= cand), 1, k)
        need = k - count(lambda x: x > kappa)
        tie = jnp.where(keys == kappa, 1.0, 0.0)
        n_rows = keys.shape[0]
        in_row = _dot(tie.astype(BF16), _strict_lower_ones(LANES))
        row_tot = jnp.broadcast_to(jnp.sum(tie, axis=-1, keepdims=True), (n_rows, LANES))
        rows_before = jnp.where(_iota((n_rows, n_rows), 1) < _iota((n_rows, n_rows), 0), 1.0, 0.0).astype(BF16)
        rank = in_row + _dot(rows_before, row_tot.astype(BF16))
        chosen = jnp.where(keys > kappa, 1.0, jnp.where(rank < need, tie, 0.0))
        mask_ref[0] = jnp.where(keys > NEG_INF_KEY, chosen, 0.0)
        new_in = jnp.where(key_new > kappa, 1.0, jnp.where((key_new == kappa) & (total(tie) < need), 1.0, 0.0))
        flag_ref[0] = jnp.broadcast_to(new_in, flag_ref.shape[1:])


def dsa_decode_index(iq, w_rows, hs, cache, layer, page_table):
    bsz, n_pages = page_table.shape
    n_chunks = n_pages // PAGES_PER_STEP
    rows = cache.shape[2]
    assert rows == LANES
    topk = min(DSA_TOPK_MAX, (n_pages * rows + 1) // 4)
    return _decode_call(
        functools.partial(_dsa_dec_index_kernel, topk=topk), "dsa_decode_index", page_table, n_chunks,
        [pl.BlockSpec((1, N_Q_HEADS, IDX_DIM), lambda b, c, *_: (b, 0, 0)), _q_spec(), _hs_spec(bsz, T_IDXK)]
        + _page_specs(IDX_DIM, 0, layer, n_chunks, False)(rows),
        [iq, w_rows, hs] + [cache] * PAGES_PER_STEP,
        [pl.BlockSpec((1, n_pages, rows), lambda b, c, *_: (b, 0, 0)),
         pl.BlockSpec((1, N_Q_HEADS, LANES), lambda b, c, *_: (b, 0, 0))],
        [jax.ShapeDtypeStruct((bsz, n_pages, rows), F32), jax.ShapeDtypeStruct((bsz, N_Q_HEADS, LANES), F32)],
        [pltpu.VMEM((n_pages, rows), I32)])


def _dsa_dec_attn_kernel(pt_ref, q_ref, kn_ref, vn_ref, mask_ref, flag_ref, *refs):
    pages = refs[:PAGES_PER_STEP]
    o_ref, m_ref, l_ref, acc_ref = refs[PAGES_PER_STEP:]
    b = pl.program_id(0)
    c = pl.program_id(1)
    qf = q_ref[0] * HEAD_DIM ** -0.5

    @pl.when(c == 0)
    def _():
        m_ref[...] = jnp.full(m_ref.shape, NEG, F32)
        l_ref[...] = jnp.zeros_like(l_ref)
        acc_ref[...] = jnp.zeros_like(acc_ref)

    qs = qf.astype(BF16)
    state = (m_ref[:, 0:1], l_ref[:, 0:1], acc_ref[...])
    for p in range(PAGES_PER_STEP):
        page = pages[p][0, 0]
        mask = jnp.broadcast_to(mask_ref[0, p:p + 1, :], (N_Q_HEADS, LANES)) > 0.5
        state = _softmax_step(_dot_nt(qs, page[:, 0:KV_W].astype(BF16)), mask,
                              page[:, KV_W:2 * KV_W].astype(BF16), *state)
    m_ref[...] = jnp.broadcast_to(state[0], m_ref.shape)
    l_ref[...] = jnp.broadcast_to(state[1], l_ref.shape)
    acc_ref[...] = state[2]

    @pl.when(c == pl.num_programs(1) - 1)
    def _():
        z_new = jnp.sum(qf * _row_of(kn_ref, b), axis=-1, keepdims=True)
        z_new = jnp.where(flag_ref[0][:, 0:1] > 0.5, z_new, NEG)
        m, l, acc = state
        m_new = jnp.maximum(m, z_new)
        alpha = jnp.exp(m - m_new)
        p_new = jnp.where(flag_ref[0][:, 0:1] > 0.5, jnp.exp(z_new - m_new), 0.0)
        l = alpha * l + p_new
        acc = alpha * acc + p_new * _row_of(vn_ref, b)
        o_ref[0] = acc / jnp.maximum(l, 1e-30)


def dsa_decode_attention(q, hs, mask, flag, cache, layer, page_table):
    bsz, n_pages = page_table.shape
    n_chunks = n_pages // PAGES_PER_STEP
    rows = cache.shape[2]
    out_spec, out_shape = _head_out(bsz)
    stat = pltpu.VMEM((N_Q_HEADS, LANES), F32)
    return _decode_call(
        _dsa_dec_attn_kernel, "dsa_decode_attention", page_table, n_chunks,
        [_q_spec(), _hs_spec(bsz, T_DSA_K), _hs_spec(bsz, T_DSA_V),
         pl.BlockSpec((1, PAGES_PER_STEP, rows), lambda b, c, *_: (b, c, 0)),
         pl.BlockSpec((1, N_Q_HEADS, LANES), lambda b, c, *_: (b, 0, 0))]
        + _page_specs(2 * KV_W, 0, layer, n_chunks, False)(rows),
        [q, hs, hs, mask, flag] + [cache] * PAGES_PER_STEP, out_spec, out_shape, [stat, stat, stat])


def _rope_tables(pos):
    half = HEAD_DIM // 2
    freq = ROPE_THETA ** (-jnp.arange(half, dtype=F32) / half)
    ang = pos.astype(F32)[:, None] * freq[None, :]
    cos = jnp.cos(ang)
    sin = jnp.sin(ang)
    return jnp.tile(cos, (1, 4)), jnp.tile(jnp.concatenate([-sin, sin], axis=1), (1, 2))


def _prepare_weights(p):
    depth = p['w_in'].shape[0]
    w_in = jnp.concatenate([p['w_in'], jnp.zeros((depth, D_MODEL, 1), F32)], axis=-1)
    eye2 = jnp.eye(N_KV_HEADS, dtype=F32)
    w = {
        'w_in': jnp.take(w_in, jnp.asarray(_proj_column_perm()), axis=-1).astype(BF16),
        'w_gate': p['w_gate'].astype(BF16),
        'b_gate': p['b_gate'].reshape(depth, 1, N_BRANCH * D_MODEL),
        'w_br': jnp.take(p['w_br'], jnp.asarray(_y_row_perm()), axis=2).astype(BF16),
        'w_o': p['w_o'].astype(BF16),
        'w_mq': p['w_mq'].astype(BF16),
        'w_mo': p['w_mo'].astype(BF16),
        'w_mkv': p['w_mkv'].astype(BF16),
        'w_router': jnp.concatenate(
            [p['w_rg'], p['w_re'], jnp.zeros((depth, D_MODEL, LANES - N_GROUPS - N_EXPERTS), F32)], axis=-1),
        'b_router': jnp.concatenate(
            [p['b_rg'], p['b_re'], jnp.zeros((depth, LANES - N_GROUPS - N_EXPERTS), F32)], axis=-1)[:, None, :],
        'w_eg': p['w_eg'].astype(BF16),
        'w_eu': p['w_eu'].astype(BF16),
        'w_ed': p['w_ed'].astype(BF16),
        'pe_k': jnp.tile(p['nsa_pe_k'], (1, 1, N_KV_HEADS)),
        'pe_v': jnp.tile(p['nsa_pe_v'], (1, 1, N_KV_HEADS)),
        'w_ck': jnp.einsum('gh,lde->lgdhe', eye2, p['nsa_w_ck']).reshape(depth, LANES, LANES),
        'w_cv': jnp.einsum('gh,lde->lgdhe', eye2, p['nsa_w_cv']).reshape(depth, LANES, LANES),
    }
    for name in ('fox_b_f', 'ln1_g', 'ln1_b', 'ln2_g', 'ln2_b', 'ln3_g', 'ln3_b'):
        w[name] = p[name]
    return w


def _tiles(h, tiles):
    parts = [h[:, :, t * LANES:(t + 1) * LANES] for t in tiles]
    return jnp.stack(parts, axis=2).reshape(h.shape[0], h.shape[1], len(tiles), N_KV_HEADS, HEAD_DIM)


def _new_state(h, logf):
    return {
        'sb_kv': _tiles(h, (T_SB_K, T_SB_V)),
        'nsa_kv': _tiles(h, (T_NSA_KC, T_NSA_VC, T_NSA_KS, T_NSA_VS)),
        'nsa_win_kv': _tiles(h, (T_NSA_KW, T_NSA_VW)),
        'dsa_kv': _tiles(h, (T_DSA_K, T_DSA_V)),
        'dsa_idx_k': h[:, :, T_IDXK * LANES:T_IDXK * LANES + IDX_DIM],
        'fox_kv': _tiles(h, (T_FOX_K, T_FOX_V)),
        'fox_logf': logf[:, :, MISC_FOXF:MISC_FOXF + N_Q_HEADS],
    }


def _dense_tail(x, ys, mem_kv, layer, w, alpha, tm, tm_mem, bsz):
    n = x.shape[0]
    l = layer
    merged = gated_merge(x, ys, w['w_gate'][l], w['b_gate'][l], w['w_br'][l], tm)
    x = matmul_res_ln(merged, w['w_o'][l], x, w['ln1_g'][l], w['ln1_b'][l], alpha, min(tm, 256))
    t = n // bsz
    xm = x.reshape(bsz, t, D_MODEL)
    if t < tm_mem:
        xm = jnp.broadcast_to(xm[:, 0:1], (bsz, tm_mem, D_MODEL))
    x = memory_block(xm, mem_kv, l, w['w_mq'][l], w['w_mo'][l], w['ln2_g'][l], w['ln2_b'][l], alpha,
                     tm_mem)[:, 0:t].reshape(n, D_MODEL)
    return moe_block(x, w['w_router'][l], w['b_router'][l], w['w_eg'][l], w['w_eu'][l], w['w_ed'][l],
                     w['ln3_g'][l], w['ln3_b'][l], alpha, tm)


def _prompt_layer(x, mem_kv, layer, w, rope, alpha, bsz):
    n = x.shape[0]
    t = n // bsz
    l = layer
    tm = min(512, t)
    h = project(x, w['w_in'][l], rope[0], rope[1], tm).reshape(bsz, t, PROJ_W)
    logf, c_tok, c_tiles = fox_prep(h, w['fox_b_f'][l])
    cmp_k, cmp_v = nsa_compress(h, w['pe_k'][l], w['pe_v'][l], w['w_ck'][l], w['w_cv'][l])
    ys = [sb_attention(h), nsa_attention(h, cmp_k, cmp_v), dsa_attention(h), fox_attention(h, c_tok, c_tiles)]
    ys = [y.reshape(n, MIX_W) for y in ys]
    x = _dense_tail(x, ys, mem_kv, l, w, alpha, tm, min(256, t), bsz)
    return x, _new_state(h, logf)


def _decode_q(hs, tile0):
    q4 = hs[:, tile0 * LANES:(tile0 + GQA_GROUP) * LANES].reshape(hs.shape[0], GQA_GROUP, LANES)
    low = jnp.arange(LANES) < HEAD_DIM
    return jnp.concatenate([jnp.where(low, q4, 0.0), jnp.where(low, 0.0, q4)], axis=1)


def _decode_y(o):
    low = jnp.arange(LANES) < HEAD_DIM
    return jnp.where(low, o[:, 0:GQA_GROUP], o[:, GQA_GROUP:]).reshape(o.shape[0], MIX_W)


def _sample_layer(x, caches, mem_kv, layer, w, rope, alpha, page_table):
    bsz = x.shape[0]
    l = layer
    hs = project(x, w['w_in'][l], rope[0], rope[1], bsz)
    q_nsa = _decode_q(hs, T_NSAQ)
    o_cmp, sel = nsa_decode_compress(q_nsa, w['pe_k'][l], w['pe_v'][l], w['w_ck'][l], w['w_cv'][l],
                                     caches['nsa_kv'], l, page_table)
    o_sel = nsa_decode_select(q_nsa, hs, sel, caches['nsa_kv'], l, page_table)
    y_nsa = nsa_decode_combine(q_nsa, hs, caches['nsa_win'], l, o_cmp, o_sel)
    iq = hs[:, T_IDXQ * LANES:(T_IDXQ + 2) * LANES].reshape(bsz, IDX_HEADS, IDX_DIM)
    iq = jnp.concatenate([iq, jnp.zeros((bsz, N_Q_HEADS - IDX_HEADS, IDX_DIM), F32)], axis=1)
    iw = hs[:, T_MISC * LANES + MISC_IDXW:T_MISC * LANES + MISC_IDXW + IDX_HEADS]
    iw = jnp.concatenate([iw, jnp.zeros((bsz, N_Q_HEADS - IDX_HEADS), F32)], axis=1)
    iw = jnp.broadcast_to(iw[:, :, None], (bsz, N_Q_HEADS, LANES))
    mask, flag = dsa_decode_index(iq, iw, hs, caches['dsa_idx_k'], l, page_table)
    y_dsa = dsa_decode_attention(_decode_q(hs, T_DSAQ), hs, mask, flag, caches['dsa_kv'], l, page_table)
    bias_row = jnp.zeros((1, LANES), F32).at[0, MISC_FOXF:MISC_FOXF + N_Q_HEADS].set(w['fox_b_f'][l])
    y_fox, logf = fox_decode(_decode_q(hs, T_FOXQ), hs, bias_row, caches['fox_kv'], caches['fox_logf_t'], l,
                             page_table)
    y_sb = sb_decode(_decode_q(hs, T_SBQ), caches['sb_kv'], l, page_table)
    ys = [_decode_y(y) for y in (y_sb, y_nsa, y_dsa, y_fox)]
    x = _dense_tail(x, ys, mem_kv, l, w, alpha, bsz, 8, bsz)
    return x, _new_state(hs[:, None, :], logf)


def kernel(x_prompt, x_sample, mem_prompt, cache_sb_kv, cache_nsa_kv, state_nsa_win_kv, cache_dsa_kv,
           cache_dsa_idx_k, cache_fox_kv, cache_fox_logf, cache_mem_kv, page_table,
           ln_in_g, ln_in_b, w_in, fox_b_f, nsa_pe_k, nsa_pe_v, nsa_w_ck, nsa_w_cv, w_br, w_gate, b_gate, w_o,
           ln1_g, ln1_b, w_mq, w_mkv, w_mo, ln2_g, ln2_b, w_rg, b_rg, w_re, b_re, w_eg, w_eu, w_ed, ln3_g, ln3_b):
    depth = w_in.shape[0]
    bsz, seq, d = x_prompt.shape
    dec_b, dec_seq, _ = x_sample.shape
    n_pool, page = cache_sb_kv.shape[1:3]
    past_len = page_table.shape[1] * page
    assert dec_seq == 1 and state_nsa_win_kv.shape[2] == NSA_WINDOW and past_len >= NSA_WINDOW
    alpha = (2.0 * depth) ** 0.25
    w = _prepare_weights(dict(
        w_in=w_in, fox_b_f=fox_b_f, nsa_pe_k=nsa_pe_k, nsa_pe_v=nsa_pe_v, nsa_w_ck=nsa_w_ck, nsa_w_cv=nsa_w_cv,
        w_br=w_br, w_gate=w_gate, b_gate=b_gate, w_o=w_o, ln1_g=ln1_g, ln1_b=ln1_b, w_mq=w_mq, w_mkv=w_mkv,
        w_mo=w_mo, ln2_g=ln2_g, ln2_b=ln2_b, w_rg=w_rg, b_rg=b_rg, w_re=w_re, b_re=b_re, w_eg=w_eg, w_eu=w_eu,
        w_ed=w_ed, ln3_g=ln3_g, ln3_b=ln3_b))

    mem_rows = mem_prompt.reshape(bsz * N_MEM, d)
    mem_kv_p = jnp.stack([matmul(mem_rows, w['w_mkv'][l], 256, 512) for l in range(depth)], axis=0)
    mem_kv_p = mem_kv_p.reshape(depth, bsz, N_MEM, 2 * MEM_W)
    rope_p = _rope_tables(jnp.arange(seq))
    x = layer_norm_rows(x_prompt.reshape(bsz * seq, d), ln_in_g, ln_in_b, 256)
    st_p = []
    for l in range(depth):
        x, new = _prompt_layer(x, mem_kv_p, l, w, rope_p, alpha, bsz)
        st_p.append(new)
    y_prompt = x.reshape(bsz, seq, d)

    caches = {
        'sb_kv': cache_sb_kv.reshape(depth, n_pool, page, 2 * KV_W),
        'nsa_kv': cache_nsa_kv.reshape(depth, n_pool, page, 4 * KV_W),
        'nsa_win': state_nsa_win_kv.reshape(depth, dec_b, NSA_WINDOW, 2 * KV_W),
        'dsa_kv': cache_dsa_kv.reshape(depth, n_pool, page, 2 * KV_W),
        'dsa_idx_k': cache_dsa_idx_k,
        'fox_kv': cache_fox_kv.reshape(depth, n_pool, page, 2 * KV_W),
        'fox_logf_t': jnp.swapaxes(cache_fox_logf, 2, 3),
    }
    mem_kv_s = cache_mem_kv.reshape(depth, dec_b, N_MEM, 2 * MEM_W)
    rope_s = _rope_tables(jnp.full((dec_b,), past_len))
    x = layer_norm_rows(x_sample.reshape(dec_b, d), ln_in_g, ln_in_b, dec_b)
    st_s = []
    for l in range(depth):
        x, new = _sample_layer(x, caches, mem_kv_s, l, w, rope_s, alpha, page_table)
        new['nsa_win_kv'] = jnp.concatenate([state_nsa_win_kv[l][:, 1:], new['nsa_win_kv']], axis=1)
        st_s.append(new)
    y_sample = x.reshape(dec_b, dec_seq, d)

    def stacked(states, name):
        return jnp.stack([s[name] for s in states], axis=0)

    win_p = stacked(st_p, 'nsa_win_kv')[:, :, seq - min(NSA_WINDOW, seq):]
    return (y_prompt, y_sample,
            stacked(st_p, 'sb_kv'), stacked(st_s, 'sb_kv'),
            stacked(st_p, 'nsa_kv'), stacked(st_s, 'nsa_kv'),
            win_p, stacked(st_s, 'nsa_win_kv'),
            stacked(st_p, 'dsa_kv'), stacked(st_s, 'dsa_kv'),
            stacked(st_p, 'dsa_idx_k'), stacked(st_s, 'dsa_idx_k'),
            stacked(st_p, 'fox_kv'), stacked(st_s, 'fox_kv'),
            stacked(st_p, 'fox_logf'), stacked(st_s, 'fox_logf'),
            mem_kv_p.reshape(depth, bsz, N_MEM, 2, MEM_HEADS, MEM_HEAD_DIM))
```

```python
import functools
import math

import jax
import jax.numpy as jnp
import numpy as np
from jax import lax
from jax.experimental import pallas as pl
from jax.experimental.pallas import tpu as pltpu

F32 = jnp.float32
BF16 = jnp.bfloat16
I32 = jnp.int32

D_MODEL = 2048
HEAD_DIM = 64
N_Q_HEADS = 8
N_KV_HEADS = 2
GQA_GROUP = 4
MIX_W = 512
KV_W = 128
N_BRANCH = 4
ROPE_THETA = 10000.0
LN_EPS = 1e-5
NSA_BLOCK = 64
NSA_TOPK = 16
NSA_WINDOW = 512
IDX_HEADS = 4
IDX_DIM = 64
DSA_TOPK_MAX = 256
N_MEM = 256
MEM_HEADS = 4
MEM_HEAD_DIM = 128
MEM_W = 512
N_GROUPS = 4
EXPERTS_PER_GROUP = 4
N_EXPERTS = 16
D_FF_EXPERT = 256

LANES = 128
NEG = -1e30
F32_EXP_ZERO = -104.0
VMEM_LIMIT = 48 * 1024 * 1024

IN_SPLITS = (
    ('sb_q', MIX_W), ('sb_k', KV_W), ('sb_v', KV_W),
    ('nsa_q', MIX_W), ('nsa_kc', KV_W), ('nsa_vc', KV_W), ('nsa_ks', KV_W), ('nsa_vs', KV_W),
    ('nsa_kw', KV_W), ('nsa_vw', KV_W), ('nsa_g', 3 * N_Q_HEADS),
    ('dsa_q', MIX_W), ('dsa_k', KV_W), ('dsa_v', KV_W),
    ('idx_q', IDX_HEADS * IDX_DIM), ('idx_k', IDX_DIM), ('idx_w', IDX_HEADS),
    ('fox_q', MIX_W), ('fox_k', KV_W), ('fox_v', KV_W), ('fox_f', N_Q_HEADS),
)
N_IN = sum(w for _, w in IN_SPLITS)

T_NSAQ, T_DSAQ, T_IDXQ, T_IDXK, T_NSA_KC, T_NSA_KS, T_NSA_KW, T_DSA_K = 0, 4, 8, 10, 11, 12, 13, 14
N_ROPE_TILES = 16
T_SBQ, T_FOXQ, T_SB_K, T_SB_V, T_NSA_VC, T_NSA_VS, T_NSA_VW, T_DSA_V, T_FOX_K, T_FOX_V, T_MISC = (
    16, 20, 24, 25, 26, 27, 28, 29, 30, 31, 32)
N_TILES = 34
PROJ_W = N_TILES * LANES
MISC_FOXF, MISC_NSAG, MISC_IDXW = 0, 8, 32


def _in_offsets():
    off, out = 0, {}
    for name, width in IN_SPLITS:
        out[name] = off
        off += width
    return out


def _proj_column_perm():
    off = _in_offsets()
    perm = np.full((PROJ_W,), N_IN, np.int32)

    def put(tile, lane, name, start, width):
        base = tile * LANES + lane
        perm[base:base + width] = off[name] + start + np.arange(width)

    def put_q(tile0, name):
        for r in range(GQA_GROUP):
            put(tile0 + r, 0, name, r * HEAD_DIM, HEAD_DIM)
            put(tile0 + r, HEAD_DIM, name, (GQA_GROUP + r) * HEAD_DIM, HEAD_DIM)

    put_q(T_NSAQ, 'nsa_q')
    put_q(T_DSAQ, 'dsa_q')
    put_q(T_SBQ, 'sb_q')
    put_q(T_FOXQ, 'fox_q')
    put(T_IDXQ, 0, 'idx_q', 0, 128)
    put(T_IDXQ + 1, 0, 'idx_q', 128, 128)
    put(T_IDXK, 0, 'idx_k', 0, 64)
    put(T_IDXK, 64, 'idx_k', 0, 64)
    for tile, name in ((T_NSA_KC, 'nsa_kc'), (T_NSA_KS, 'nsa_ks'), (T_NSA_KW, 'nsa_kw'), (T_DSA_K, 'dsa_k'),
                       (T_SB_K, 'sb_k'), (T_SB_V, 'sb_v'), (T_NSA_VC, 'nsa_vc'), (T_NSA_VS, 'nsa_vs'),
                       (T_NSA_VW, 'nsa_vw'), (T_DSA_V, 'dsa_v'), (T_FOX_K, 'fox_k'), (T_FOX_V, 'fox_v')):
        put(tile, 0, name, 0, 128)
    put(T_MISC, MISC_FOXF, 'fox_f', 0, 8)
    put(T_MISC, MISC_NSAG, 'nsa_g', 0, 24)
    put(T_MISC, MISC_IDXW, 'idx_w', 0, 4)
    return perm


def _y_row_perm():
    perm = np.zeros((MIX_W,), np.int32)
    for r in range(GQA_GROUP):
        for half in range(2):
            h = half * GQA_GROUP + r
            perm[r * LANES + half * HEAD_DIM:r * LANES + (half + 1) * HEAD_DIM] = h * HEAD_DIM + np.arange(HEAD_DIM)
    return perm


def _dot(a, b):
    return jnp.dot(a, b, preferred_element_type=F32)


def _dot_nt(a, b):
    return lax.dot_general(a, b, (((1,), (1,)), ((), ())), preferred_element_type=F32)


def _split2(x):
    hi = x.astype(BF16)
    lo = (x - hi.astype(F32)).astype(BF16)
    return hi, lo


def _dot_x_exact(x, m_bf16):
    hi, lo = _split2(x)
    return _dot(hi, m_bf16) + _dot(lo, m_bf16)


def _dot_exact_x(m_bf16, x):
    hi, lo = _split2(x)
    return _dot(m_bf16, hi) + _dot(m_bf16, lo)


def _dot3(a, b):
    ah, al = _split2(a)
    bh, bl = _split2(b)
    return _dot(ah, bh) + (_dot(ah, bl) + _dot(al, bh))


def _dot3_nt(a, b):
    ah, al = _split2(a)
    bh, bl = _split2(b)
    return _dot_nt(ah, bh) + (_dot_nt(ah, bl) + _dot_nt(al, bh))


def _iota(shape, dim):
    return lax.broadcasted_iota(I32, shape, dim)


def _layer_norm(v, g, b):
    mu = jnp.mean(v, axis=-1, keepdims=True)
    c = v - mu
    var = jnp.mean(c * c, axis=-1, keepdims=True)
    return c * lax.rsqrt(var + LN_EPS) * g + b


def _stack_q(q, scale):
    tq = q.shape[0]
    low = _iota((tq, LANES), 1) < HEAD_DIM
    parts = []
    for g in range(N_KV_HEADS):
        for r in range(GQA_GROUP):
            t = q[:, r * LANES:(r + 1) * LANES]
            parts.append(jnp.where(low if g == 0 else jnp.logical_not(low), t * scale, 0.0))
    return jnp.concatenate(parts, axis=0)


def _store_heads(o_ref, o, tq):
    low = _iota((tq, LANES), 1) < HEAD_DIM
    for r in range(GQA_GROUP):
        lo_head = o[r * tq:(r + 1) * tq]
        hi_head = o[(GQA_GROUP + r) * tq:(GQA_GROUP + r + 1) * tq]
        o_ref[0, :, r * LANES:(r + 1) * LANES] = jnp.where(low, lo_head, hi_head)


def _rep_rows(x, n):
    return jnp.concatenate([x] * n, axis=0)


def _params(*sem):
    return pltpu.CompilerParams(dimension_semantics=sem, vmem_limit_bytes=VMEM_LIMIT)


def _ln_kernel(x_ref, g_ref, b_ref, o_ref):
    o_ref[...] = _layer_norm(x_ref[...], g_ref[...], b_ref[...])


def layer_norm_rows(x, g, b, tm):
    n, d = x.shape
    return pl.pallas_call(
        _ln_kernel,
        grid=(n // tm,),
        in_specs=[pl.BlockSpec((tm, d), lambda i: (i, 0)),
                  pl.BlockSpec((1, d), lambda i: (0, 0)),
                  pl.BlockSpec((1, d), lambda i: (0, 0))],
        out_specs=pl.BlockSpec((tm, d), lambda i: (i, 0)),
        out_shape=jax.ShapeDtypeStruct((n, d), F32),
        compiler_params=_params("arbitrary"),
        name="ln_in",
    )(x, g.reshape(1, d), b.reshape(1, d))


def _proj_kernel(x_ref, w_ref, cos_ref, sin_ref, o_ref, xb_ref, *, n_rope_blocks, tn):
    j = pl.program_id(1)

    @pl.when(j == 0)
    def _():
        xb_ref[...] = x_ref[...].astype(BF16)

    acc = _dot(xb_ref[...], w_ref[...])

    @pl.when(j < n_rope_blocks)
    def _():
        cos = cos_ref[...]
        sin = sin_ref[...]
        first = (_iota(cos.shape, 1) & (HEAD_DIM - 1)) < HEAD_DIM // 2
        for s in range(tn // LANES):
            a = acc[:, s * LANES:(s + 1) * LANES]
            rot = jnp.where(first, pltpu.roll(a, LANES - HEAD_DIM // 2, 1), pltpu.roll(a, HEAD_DIM // 2, 1))
            o_ref[:, s * LANES:(s + 1) * LANES] = a * cos + rot * sin

    @pl.when(j >= n_rope_blocks)
    def _():
        o_ref[...] = acc


def project(x, w_perm, cos_t, sin_t, tm, tn=256):
    n, d = x.shape
    n_pos_blocks = cos_t.shape[0] // tm
    return pl.pallas_call(
        functools.partial(_proj_kernel, n_rope_blocks=N_ROPE_TILES * LANES // tn, tn=tn),
        grid=(n // tm, PROJ_W // tn),
        in_specs=[pl.BlockSpec((tm, d), lambda i, j: (i, 0)),
                  pl.BlockSpec((d, tn), lambda i, j: (0, j)),
                  pl.BlockSpec((tm, LANES), lambda i, j: (i % n_pos_blocks, 0)),
                  pl.BlockSpec((tm, LANES), lambda i, j: (i % n_pos_blocks, 0))],
        out_specs=pl.BlockSpec((tm, tn), lambda i, j: (i, j)),
        out_shape=jax.ShapeDtypeStruct((n, PROJ_W), F32),
        scratch_shapes=[pltpu.VMEM((tm, d), BF16)],
        compiler_params=_params("arbitrary", "arbitrary"),
        name="proj",
    )(x, w_perm, cos_t, sin_t)


def _matmul_kernel(x_ref, w_ref, o_ref):
    o_ref[...] = _dot(x_ref[...].astype(BF16), w_ref[...])


def matmul(x, w, tm, tn):
    n, d = x.shape
    m = w.shape[1]
    return pl.pallas_call(
        _matmul_kernel,
        grid=(n // tm, m // tn),
        in_specs=[pl.BlockSpec((tm, d), lambda i, j: (i, 0)),
                  pl.BlockSpec((d, tn), lambda i, j: (0, j))],
        out_specs=pl.BlockSpec((tm, tn), lambda i, j: (i, j)),
        out_shape=jax.ShapeDtypeStruct((n, m), F32),
        compiler_params=_params("arbitrary", "arbitrary"),
        name="matmul",
    )(x, w)


def _merge_kernel(x_ref, ya_ref, yb_ref, yc_ref, yd_ref, wg0, wg1, wg2, wg3, bg0, bg1, bg2, bg3,
                  wb0, wb1, wb2, wb3, o_ref, xb_ref, yb16_ref):
    j = pl.program_id(1)

    @pl.when(j == 0)
    def _():
        xb_ref[...] = x_ref[...].astype(BF16)
        for b, y in enumerate((ya_ref, yb_ref, yc_ref, yd_ref)):
            yb16_ref[b] = y[...].astype(BF16)

    xb = xb_ref[...]
    acc = None
    for b, (wg, bg, wb) in enumerate(((wg0, bg0, wb0), (wg1, bg1, wb1), (wg2, bg2, wb2), (wg3, bg3, wb3))):
        gate = jax.nn.sigmoid(_dot(xb, wg[...]) + bg[...])
        term = gate * _dot(yb16_ref[b], wb[0])
        acc = term if acc is None else acc + term
    o_ref[...] = acc.astype(BF16)


def gated_merge(x, ys, w_gate, b_gate, w_br, tm, tn=256):
    n, d = x.shape
    nj = d // tn
    in_specs = [pl.BlockSpec((tm, d), lambda i, j: (i, 0))]
    in_specs += [pl.BlockSpec((tm, MIX_W), lambda i, j: (i, 0)) for _ in range(N_BRANCH)]
    in_specs += [pl.BlockSpec((d, tn), functools.partial(lambda i, j, b: (0, b * nj + j), b=b)) for b in range(N_BRANCH)]
    in_specs += [pl.BlockSpec((1, tn), functools.partial(lambda i, j, b: (0, b * nj + j), b=b)) for b in range(N_BRANCH)]
    in_specs += [pl.BlockSpec((1, MIX_W, tn), functools.partial(lambda i, j, b: (b, 0, j), b=b)) for b in range(N_BRANCH)]
    return pl.pallas_call(
        _merge_kernel,
        grid=(n // tm, nj),
        in_specs=in_specs,
        out_specs=pl.BlockSpec((tm, tn), lambda i, j: (i, j)),
        out_shape=jax.ShapeDtypeStruct((n, d), BF16),
        scratch_shapes=[pltpu.VMEM((tm, d), BF16), pltpu.VMEM((N_BRANCH, tm, MIX_W), BF16)],
        compiler_params=_params("arbitrary", "arbitrary"),
        name="gated_merge",
    )(x, *ys, *([w_gate] * N_BRANCH), *([b_gate] * N_BRANCH), *([w_br] * N_BRANCH))


def _matmul_res_ln_kernel(a_ref, w_ref, r_ref, g_ref, b_ref, o_ref, *, alpha):
    y = _dot(a_ref[...].astype(BF16), w_ref[...])
    o_ref[...] = _layer_norm(alpha * r_ref[...] + y, g_ref[...], b_ref[...])


def matmul_res_ln(a, w, res, g, b, alpha, tm):
    n, k = a.shape
    d = w.shape[1]
    return pl.pallas_call(
        functools.partial(_matmul_res_ln_kernel, alpha=alpha),
        grid=(n // tm,),
        in_specs=[pl.BlockSpec((tm, k), lambda i: (i, 0)),
                  pl.BlockSpec((k, d), lambda i: (0, 0)),
                  pl.BlockSpec((tm, d), lambda i: (i, 0)),
                  pl.BlockSpec((1, d), lambda i: (0, 0)),
                  pl.BlockSpec((1, d), lambda i: (0, 0))],
        out_specs=pl.BlockSpec((tm, d), lambda i: (i, 0)),
        out_shape=jax.ShapeDtypeStruct((n, d), F32),
        compiler_params=_params("arbitrary"),
        name="matmul_res_ln",
    )(a, w, res, g.reshape(1, d), b.reshape(1, d))


def _mem_kernel(x_ref, wq_ref, k_ref, v_ref, wo_ref, g_ref, b_ref, o_ref, *, alpha):
    x = x_ref[0]
    q = _dot(x.astype(BF16), wq_ref[...])
    outs = []
    for h in range(MEM_HEADS):
        sl = slice(h * MEM_HEAD_DIM, (h + 1) * MEM_HEAD_DIM)
        z = _dot_nt(q[:, sl].astype(BF16), k_ref[0, 0, :, sl].astype(BF16)) * (MEM_HEAD_DIM ** -0.5)
        m = jnp.max(z, axis=-1, keepdims=True)
        e = jnp.exp(z - m)
        p = e / jnp.sum(e, axis=-1, keepdims=True)
        outs.append(_dot(p.astype(BF16), v_ref[0, 0, :, sl].astype(BF16)))
    o = jnp.concatenate(outs, axis=1)
    y = _dot(o.astype(BF16), wo_ref[...])
    o_ref[0] = _layer_norm(alpha * x + y, g_ref[...], b_ref[...])


def memory_block(x, mem_kv, layer, w_mq, w_mo, g, b, alpha, tm):
    bsz, t, d = x.shape
    return pl.pallas_call(
        functools.partial(_mem_kernel, alpha=alpha),
        grid=(bsz, t // tm),
        in_specs=[pl.BlockSpec((1, tm, d), lambda bi, i: (bi, i, 0)),
                  pl.BlockSpec((d, MEM_W), lambda bi, i: (0, 0)),
                  pl.BlockSpec((1, 1, N_MEM, MEM_W), lambda bi, i: (layer, bi, 0, 0)),
                  pl.BlockSpec((1, 1, N_MEM, MEM_W), lambda bi, i: (layer, bi, 0, 1)),
                  pl.BlockSpec((MEM_W, d), lambda bi, i: (0, 0)),
                  pl.BlockSpec((1, d), lambda bi, i: (0, 0)),
                  pl.BlockSpec((1, d), lambda bi, i: (0, 0))],
        out_specs=pl.BlockSpec((1, tm, d), lambda bi, i: (bi, i, 0)),
        out_shape=jax.ShapeDtypeStruct((bsz, t, d), F32),
        compiler_params=_params("arbitrary", "arbitrary"),
        name="memory_block",
    )(x, w_mq, mem_kv, mem_kv, w_mo, g.reshape(1, d), b.reshape(1, d))


def _moe_kernel(x_ref, wr_ref, br_ref, weg_ref, weu_ref, wed_ref, g_ref, b_ref, o_ref,
                xb_ref, gate_ref, acc_ref, *, alpha):
    e = pl.program_id(1)
    tm = x_ref.shape[0]
    lane = _iota((tm, LANES), 1)

    @pl.when(e == 0)
    def _():
        x = x_ref[...]
        xb_ref[...] = x.astype(BF16)
        acc_ref[...] = jnp.zeros_like(acc_ref)
        logits = _dot3(x, wr_ref[...]) + br_ref[...]
        is_grp = lane < N_GROUPS
        gl = jnp.where(is_grp, logits, NEG)
        gmax = jnp.max(gl, axis=-1, keepdims=True)
        grp = jnp.min(jnp.where(is_grp & (gl == gmax), lane, LANES), axis=-1, keepdims=True)
        p_grp = 1.0 / jnp.sum(jnp.where(is_grp, jnp.exp(gl - gmax), 0.0), axis=-1, keepdims=True)
        in_grp = (lane >= N_GROUPS) & (lane < N_GROUPS + N_EXPERTS) & (
            ((lane - N_GROUPS) >> 2) == grp)
        el = jnp.where(in_grp, logits, NEG)
        v1 = jnp.max(el, axis=-1, keepdims=True)
        i1 = jnp.min(jnp.where(in_grp & (el == v1), lane, LANES), axis=-1, keepdims=True)
        rest = in_grp & (lane != i1)
        el2 = jnp.where(rest, logits, NEG)
        v2 = jnp.max(el2, axis=-1, keepdims=True)
        i2 = jnp.min(jnp.where(rest & (el2 == v2), lane, LANES), axis=-1, keepdims=True)
        e2 = jnp.exp(v2 - v1)
        w1 = p_grp / (1.0 + e2)
        w2 = p_grp * e2 / (1.0 + e2)
        gate_ref[...] = jnp.where(lane == i1, w1, 0.0) + jnp.where(lane == i2, w2, 0.0)

    gate_e = jnp.sum(jnp.where(lane == e + N_GROUPS, gate_ref[...], 0.0), axis=-1, keepdims=True)

    @pl.when(jnp.max(gate_e) > 0.0)
    def _():
        xb = xb_ref[...]
        hid = jax.nn.silu(_dot(xb, weg_ref[0])) * _dot(xb, weu_ref[0])
        acc_ref[...] += _dot((hid * gate_e).astype(BF16), wed_ref[0])

    @pl.when(e == N_EXPERTS - 1)
    def _():
        o_ref[...] = _layer_norm(alpha * x_ref[...] + acc_ref[...], g_ref[...], b_ref[...])


def moe_block(x, w_router, b_router, w_eg, w_eu, w_ed, g, b, alpha, tm):
    n, d = x.shape
    f = w_eg.shape[-1]
    return pl.pallas_call(
        functools.partial(_moe_kernel, alpha=alpha),
        grid=(n // tm, N_EXPERTS),
        in_specs=[pl.BlockSpec((tm, d), lambda i, e: (i, 0)),
                  pl.BlockSpec((d, LANES), lambda i, e: (0, 0)),
                  pl.BlockSpec((1, LANES), lambda i, e: (0, 0)),
                  pl.BlockSpec((1, d, f), lambda i, e: (e, 0, 0)),
                  pl.BlockSpec((1, d, f), lambda i, e: (e, 0, 0)),
                  pl.BlockSpec((1, f, d), lambda i, e: (e, 0, 0)),
                  pl.BlockSpec((1, d), lambda i, e: (0, 0)),
                  pl.BlockSpec((1, d), lambda i, e: (0, 0))],
        out_specs=pl.BlockSpec((tm, d), lambda i, e: (i, 0)),
        out_shape=jax.ShapeDtypeStruct((n, d), F32),
        scratch_shapes=[pltpu.VMEM((tm, d), BF16), pltpu.VMEM((tm, LANES), F32), pltpu.VMEM((tm, d), F32)],
        compiler_params=_params("arbitrary", "arbitrary"),
        name="moe_block",
    )(x, w_router, b_router, w_eg, w_eu, w_ed, g.reshape(1, d), b.reshape(1, d))


TQ = 128
BK = 128


def _log_sigmoid_pair(z):
    l1p = jnp.log1p(jnp.exp(-jnp.abs(z)))
    return jnp.minimum(z, 0.0) - l1p, jnp.minimum(-z, 0.0) - l1p


def _strict_upper_ones(n):
    return jnp.where(_iota((n, n), 0) > _iota((n, n), 1), 1.0, 0.0).astype(BF16)


def _strict_lower_ones(n):
    return jnp.where(_iota((n, n), 0) < _iota((n, n), 1), 1.0, 0.0).astype(BF16)


def _sb_kernel(q_ref, k_ref, v_ref, o_ref):
    qi = pl.program_id(1)
    r_rows = N_Q_HEADS * TQ
    qs = _stack_q(q_ref[0], HEAD_DIM ** -0.5).astype(BF16)
    upper = _strict_upper_ones(BK)
    strictly_causal = _iota((TQ, BK), 1) < _iota((TQ, BK), 0)

    def step(kb, acc, run, mask):
        ks = pl.multiple_of(kb * BK, BK)
        k = k_ref[0, pl.ds(ks, BK), :].astype(BF16)
        v = v_ref[0, pl.ds(ks, BK), :].astype(BF16)
        z = _dot_nt(qs, k)
        lss, lns = [], []
        for h in range(N_Q_HEADS):
            ls, ln = _log_sigmoid_pair(z[h * TQ:(h + 1) * TQ])
            lss.append(ls)
            lns.append(ln if mask is None else jnp.where(mask, ln, 0.0))
        ln = jnp.concatenate(lns, axis=0)
        between = _dot_x_exact(ln, upper)
        weights = []
        for h in range(N_Q_HEADS):
            sl = slice(h * TQ, (h + 1) * TQ)
            a = jnp.exp(lss[h] + between[sl] + run[sl])
            weights.append((a if mask is None else jnp.where(mask, a, 0.0)).astype(BF16))
        acc = acc + _dot(jnp.concatenate(weights, axis=0), v)
        return acc, run + between[:, 0:1] + ln[:, 0:1]

    acc, run = step(qi, jnp.zeros((r_rows, LANES), F32), jnp.zeros((r_rows, 1), F32), strictly_causal)

    def more(state):
        kb, _, run = state
        return jnp.logical_and(kb >= 0, jnp.max(run) > F32_EXP_ZERO)

    def body(state):
        kb, acc, run = state
        acc, run = step(kb, acc, run, None)
        return kb - 1, acc, run

    _, acc, _ = lax.while_loop(more, body, (qi - 1, acc, run))
    _store_heads(o_ref, acc, TQ)


def _h_spec(rows, tile, width=1):
    if rows is None:
        return lambda t: pl.BlockSpec((1, t, width * LANES), lambda b, i: (b, 0, tile // width))
    return pl.BlockSpec((1, rows, width * LANES), lambda b, i: (b, i, tile // width))


def sb_attention(h):
    bsz, t, _ = h.shape
    return pl.pallas_call(
        _sb_kernel,
        grid=(bsz, t // TQ),
        in_specs=[_h_spec(TQ, T_SBQ, 4), _h_spec(None, T_SB_K)(t), _h_spec(None, T_SB_V)(t)],
        out_specs=pl.BlockSpec((1, TQ, MIX_W), lambda b, i: (b, i, 0)),
        out_shape=jax.ShapeDtypeStruct((bsz, t, MIX_W), F32),
        compiler_params=_params("arbitrary", "arbitrary"),
        name="sb_attention",
    )(h, h, h)


def _fox_prep_kernel(m_ref, bf_ref, logf_ref, c_ref, carry_ref):
    i = pl.program_id(1)
    tc = m_ref.shape[1]

    @pl.when(i == 0)
    def _():
        carry_ref[...] = jnp.zeros_like(carry_ref)

    logf, _ = _log_sigmoid_pair(m_ref[0] + bf_ref[...])
    logf_ref[0] = logf
    incl = jnp.where(_iota((tc, tc), 0) >= _iota((tc, tc), 1), 1.0, 0.0).astype(BF16)
    c = _dot_exact_x(incl, logf) + carry_ref[...]
    c_ref[0] = c
    carry_ref[...] = c[tc - 1:tc, :]


def fox_prep(h, fox_b_f, tc=256):
    bsz, t, _ = h.shape
    bias = jnp.zeros((1, LANES), F32).at[0, MISC_FOXF:MISC_FOXF + N_Q_HEADS].set(fox_b_f)
    return pl.pallas_call(
        _fox_prep_kernel,
        grid=(bsz, t // tc),
        in_specs=[pl.BlockSpec((1, tc, LANES), lambda b, i: (b, i, T_MISC)),
                  pl.BlockSpec((1, LANES), lambda b, i: (0, 0))],
        out_specs=[pl.BlockSpec((1, tc, LANES), lambda b, i: (b, i, 0)),
                   pl.BlockSpec((1, tc, LANES), lambda b, i: (b, i, 0))],
        out_shape=[jax.ShapeDtypeStruct((bsz, t, LANES), F32),
                   jax.ShapeDtypeStruct((bsz, t, LANES), F32)],
        scratch_shapes=[pltpu.VMEM((1, LANES), F32)],
        compiler_params=_params("arbitrary", "arbitrary"),
        name="fox_prep",
    )(h, bias)


def _head_columns(x, lane0, stride=1):
    return jnp.concatenate([x[:, lane0 + stride * h:lane0 + stride * h + 1] for h in range(N_Q_HEADS)], axis=0)


def _softmax_init(rows):
    return (jnp.full((rows, 1), NEG, F32), jnp.zeros((rows, 1), F32), jnp.zeros((rows, LANES), F32))


def _transpose_heads(x):
    if x.shape[0] == LANES:
        return jnp.concatenate([x[:, h * TQ:(h + 1) * TQ].T for h in range(N_Q_HEADS)], axis=0)
    return jnp.concatenate([x[h * TQ:(h + 1) * TQ].T for h in range(N_Q_HEADS)], axis=1)


def _softmax_init_t():
    cols = N_Q_HEADS * TQ
    return (jnp.full((1, cols), NEG, F32), jnp.zeros((1, cols), F32), jnp.zeros((LANES, cols), F32))


def _softmax_step_t(zt, masks, v, m, l, acc, biases=None):
    ps, ms, ls, alphas = [], [], [], []
    for h in range(N_Q_HEADS):
        sl = slice(h * TQ, (h + 1) * TQ)
        z = zt[:, sl]
        if biases is not None:
            z = z + biases[h]
        mask = None if masks is None else masks[h]
        if mask is not None:
            z = jnp.where(mask, z, NEG)
        m_new = jnp.maximum(m[:, sl], jnp.max(z, axis=0, keepdims=True))
        p = jnp.exp(z - m_new)
        if mask is not None:
            p = jnp.where(mask, p, 0.0)
        alpha = jnp.exp(m[:, sl] - m_new)
        ls.append(alpha * l[:, sl] + jnp.sum(p, axis=0, keepdims=True))
        ms.append(m_new)
        alphas.append(alpha)
        ps.append(p.astype(BF16))
    cat = lambda xs: jnp.concatenate(xs, axis=1)
    pv = _dot(v.T.astype(BF16), cat(ps))
    return cat(ms), cat(ls), cat(alphas) * acc + pv


def _fox_kernel(q_ref, k_ref, v_ref, cq_ref, c_ref, o_ref):
    qi = pl.program_id(1)
    qst = _transpose_heads(_stack_q(q_ref[0], HEAD_DIM ** -0.5)).astype(BF16)
    cq_t = cq_ref[0].T

    def step(kb, carry, masks):
        ks = pl.multiple_of(kb * BK, BK)
        k = k_ref[0, pl.ds(ks, BK), :].astype(BF16)
        ck = c_ref[0, pl.ds(ks, BK), :]
        biases = [cq_t[h:h + 1, :] - ck[:, h:h + 1] for h in range(N_Q_HEADS)]
        return _softmax_step_t(_dot(k, qst), masks, v_ref[0, pl.ds(ks, BK), :], *carry, biases=biases)

    carry = lax.fori_loop(0, qi, lambda kb, c: step(kb, c, None), _softmax_init_t())
    causal = _iota((BK, TQ), 0) <= _iota((BK, TQ), 1)
    _, l, acc = step(qi, carry, [causal] * N_Q_HEADS)
    _store_heads(o_ref, _transpose_heads(acc / jnp.maximum(l, 1e-30)), TQ)


def fox_attention(h, c_tok):
    bsz, t, _ = h.shape
    return pl.pallas_call(
        _fox_kernel,
        grid=(bsz, t // TQ),
        in_specs=[_h_spec(TQ, T_FOXQ, 4), _h_spec(None, T_FOX_K)(t), _h_spec(None, T_FOX_V)(t),
                  pl.BlockSpec((1, TQ, LANES), lambda b, i: (b, i, 0)),
                  pl.BlockSpec((1, t, LANES), lambda b, i: (b, 0, 0))],
        out_specs=pl.BlockSpec((1, TQ, MIX_W), lambda b, i: (b, i, 0)),
        out_shape=jax.ShapeDtypeStruct((bsz, t, MIX_W), F32),
        compiler_params=_params("arbitrary", "arbitrary"),
        name="fox_attention",
    )(h, h, h, c_tok, c_tok)


def _nsa_cmp_kernel(k_ref, v_ref, pek_ref, pev_ref, wk_ref, wv_ref, ok_ref, ov_ref, *, nblk):
    for src, pe, w, dst in ((k_ref, pek_ref, wk_ref, ok_ref), (v_ref, pev_ref, wv_ref, ov_ref)):
        rows = src[0].reshape(nblk, NSA_BLOCK, LANES) + pe[...][None]
        mean = jnp.sum(rows, axis=1) * (1.0 / NSA_BLOCK)
        dst[0] = jnp.zeros(dst.shape[1:], F32)
        dst[0, 0:nblk, :] = _dot3(mean, w[...])


def nsa_compress(h, pe_k2, pe_v2, w_ck2, w_cv2):
    bsz, t, _ = h.shape
    nblk = t // NSA_BLOCK
    nblk_pad = -(-nblk // LANES) * LANES
    full = lambda shape: pl.BlockSpec(shape, lambda b: (0,) * len(shape))
    return pl.pallas_call(
        functools.partial(_nsa_cmp_kernel, nblk=nblk),
        grid=(bsz,),
        in_specs=[pl.BlockSpec((1, t, LANES), lambda b: (b, 0, T_NSA_KC)),
                  pl.BlockSpec((1, t, LANES), lambda b: (b, 0, T_NSA_VC)),
                  full((NSA_BLOCK, LANES)), full((NSA_BLOCK, LANES)),
                  full((LANES, LANES)), full((LANES, LANES))],
        out_specs=[pl.BlockSpec((1, nblk_pad, LANES), lambda b: (b, 0, 0))] * 2,
        out_shape=[jax.ShapeDtypeStruct((bsz, nblk_pad, LANES), F32)] * 2,
        compiler_params=_params("arbitrary"),
        name="nsa_compress",
    )(h, h, pe_k2, pe_v2, w_ck2, w_cv2)


def _topk_block_mask(imp_t, nblk, k_sel):
    n_idx = _iota(imp_t.shape, 0)
    rank = jnp.zeros(imp_t.shape, F32)
    for m in range(nblk):
        row = imp_t[m:m + 1, :]
        before = jnp.where(n_idx > m, 1.0, 0.0)
        rank = rank + jnp.where(row > imp_t, 1.0, jnp.where(row == imp_t, before, 0.0))
    return jnp.where(rank < k_sel, jnp.where(imp_t > -jnp.inf, 1.0, 0.0), 0.0)


def _nsa_kernel(q_ref, ck_ref, cv_ref, ks_ref, vs_ref, kw_ref, vw_ref, g_ref, o_ref, *, nblk, k_sel):
    qi = pl.program_id(1)
    r_rows = N_Q_HEADS * TQ
    half = GQA_GROUP * TQ
    qf = _stack_q(q_ref[0], HEAD_DIM ** -0.5)
    qpos = qi * TQ + (_iota((r_rows, BK), 0) & (TQ - 1))
    col = _iota((r_rows, BK), 1)

    n_ctile = ck_ref.shape[1] // LANES
    sel_tiles = []
    state = _softmax_init(r_rows)
    zc_tiles, vis_tiles = [], []
    for c in range(n_ctile):
        blk = c * LANES + col
        vis = ((blk + 1) * NSA_BLOCK - 1 <= qpos) & (blk < nblk)
        zc = _dot3_nt(qf, ck_ref[0, c * LANES:(c + 1) * LANES, :])
        zc_tiles.append(zc)
        vis_tiles.append(vis)
        m_run = jnp.maximum(state[0], jnp.max(jnp.where(vis, zc, NEG), axis=-1, keepdims=True))
        state = (m_run,) + state[1:]
    m_c = state[0]
    e_tiles = [jnp.where(vis, jnp.exp(zc - m_c), 0.0) for zc, vis in zip(zc_tiles, vis_tiles)]
    denom = e_tiles[0].sum(axis=-1, keepdims=True)
    for e in e_tiles[1:]:
        denom = denom + e.sum(axis=-1, keepdims=True)
    denom = jnp.maximum(denom, 1e-30)
    o_cmp = jnp.zeros((r_rows, LANES), F32)
    tq_pos = qi * TQ + _iota((TQ, LANES), 0)
    tq_col = _iota((TQ, LANES), 1)
    for c in range(n_ctile):
        pc = e_tiles[c] / denom
        o_cmp = o_cmp + _dot(pc.astype(BF16), cv_ref[0, c * LANES:(c + 1) * LANES, :].astype(BF16))
        blk = c * LANES + tq_col
        vis_t = ((blk + 1) * NSA_BLOCK - 1 <= tq_pos) & (blk < nblk)
        forced = (blk == (tq_pos >> 6)) | (blk == 0)
        per_group = []
        for g in range(N_KV_HEADS):
            imp = pc[g * half:g * half + TQ]
            for r in range(1, GQA_GROUP):
                imp = imp + pc[g * half + r * TQ:g * half + (r + 1) * TQ]
            imp = jnp.where(forced, jnp.inf, jnp.where(vis_t, imp, -jnp.inf))
            per_group.append(imp)
        sel_tiles.append(per_group)

    sel = []
    for g in range(N_KV_HEADS):
        imp_t = jnp.concatenate([sel_tiles[c][g].T for c in range(n_ctile)], axis=0)[0:nblk]
        chosen = _topk_block_mask(imp_t, nblk, k_sel)
        pad = n_ctile * LANES - nblk
        if pad:
            chosen = jnp.concatenate([chosen, jnp.zeros((pad, TQ), F32)], axis=0)
        sel.append([chosen[c * LANES:(c + 1) * LANES].astype(BF16) for c in range(n_ctile)])

    qst = _transpose_heads(qf).astype(BF16)
    blocks_per_tile = BK // NSA_BLOCK
    e_key = _iota((BK, LANES), 0) >> 6
    e_blk = _iota((BK, LANES), 1)
    kq_key = _iota((BK, TQ), 0)
    kq_pos = qi * TQ + _iota((BK, TQ), 1)

    def sel_body(kb, carry):
        ks = pl.multiple_of(kb * BK, BK)
        k = ks_ref[0, pl.ds(ks, BK), :].astype(BF16)
        causal = (ks + kq_key) <= kq_pos
        masks = []
        for g in range(N_KV_HEADS):
            hit = None
            for c in range(n_ctile):
                expand = jnp.where(e_blk + c * LANES == kb * blocks_per_tile + e_key, 1.0, 0.0).astype(BF16)
                part = _dot(expand, sel[g][c])
                hit = part if hit is None else hit + part
            masks += [jnp.where(causal, hit, 0.0) > 0.5] * GQA_GROUP
        return _softmax_step_t(_dot(k, qst), masks, vs_ref[0, pl.ds(ks, BK), :], *carry)

    _, l_s, acc_s = lax.fori_loop(0, qi + 1, sel_body, _softmax_init_t())
    o_sel = _transpose_heads(acc_s / jnp.maximum(l_s, 1e-30))

    def win_body(kb, carry):
        ks = pl.multiple_of(kb * BK, BK)
        k = kw_ref[0, pl.ds(ks, BK), :].astype(BF16)
        kpos = ks + kq_key
        band = jnp.where(kpos <= kq_pos, jnp.where(kpos >= kq_pos - NSA_WINDOW, 1.0, 0.0), 0.0) > 0.5
        return _softmax_step_t(_dot(k, qst), [band] * N_Q_HEADS, vw_ref[0, pl.ds(ks, BK), :], *carry)

    first = jnp.maximum(qi - NSA_WINDOW // BK, 0)
    _, l_w, acc_w = lax.fori_loop(first, qi + 1, win_body, _softmax_init_t())
    o_win = _transpose_heads(acc_w / jnp.maximum(l_w, 1e-30))

    gates = jax.nn.sigmoid(g_ref[0])
    o = (_head_columns(gates, MISC_NSAG, 3) * o_cmp + _head_columns(gates, MISC_NSAG + 1, 3) * o_sel
         + _head_columns(gates, MISC_NSAG + 2, 3) * o_win)
    _store_heads(o_ref, o, TQ)


def nsa_attention(h, cmp_k, cmp_v):
    bsz, t, _ = h.shape
    nblk = t // NSA_BLOCK
    nblk_pad = cmp_k.shape[1]
    return pl.pallas_call(
        functools.partial(_nsa_kernel, nblk=nblk, k_sel=min(NSA_TOPK, nblk)),
        grid=(bsz, t // TQ),
        in_specs=[_h_spec(TQ, T_NSAQ, 4),
                  pl.BlockSpec((1, nblk_pad, LANES), lambda b, i: (b, 0, 0)),
                  pl.BlockSpec((1, nblk_pad, LANES), lambda b, i: (b, 0, 0)),
                  _h_spec(None, T_NSA_KS)(t), _h_spec(None, T_NSA_VS)(t),
                  _h_spec(None, T_NSA_KW)(t), _h_spec(None, T_NSA_VW)(t),
                  _h_spec(TQ, T_MISC)],
        out_specs=pl.BlockSpec((1, TQ, MIX_W), lambda b, i: (b, i, 0)),
        out_shape=jax.ShapeDtypeStruct((bsz, t, MIX_W), F32),
        compiler_params=_params("arbitrary", "arbitrary"),
        name="nsa_attention",
    )(h, cmp_k, cmp_v, h, h, h, h, h)


def _order_key(x):
    bits = lax.bitcast_convert_type(x + 0.0, I32)
    return bits ^ ((bits >> 31) & 0x7FFFFFFF)


INT_MIN = -2 ** 31
NEG_INF_KEY = -2139095041


def _kth_largest_key(count_ge, shape, k):
    def bit_body(it, kappa):
        bit = lax.shift_left(jnp.int32(1), 31 - it)
        cand = kappa | bit
        cnt = count_ge(cand ^ INT_MIN)
        return jnp.where(cnt >= k, cand, kappa)

    kappa = lax.fori_loop(0, 32, bit_body, jnp.zeros(shape, I32))
    return kappa ^ INT_MIN


def _dsa_kernel(q_ref, iq_ref, ik_ref, w_ref, k_ref, v_ref, o_ref, key_ref, *, topk):
    qi = pl.program_id(1)
    qst = _transpose_heads(_stack_q(q_ref[0], HEAD_DIM ** -0.5)).astype(BF16)
    iq = iq_ref[0]
    low = _iota((TQ, LANES), 1) < HEAD_DIM
    iq_t = jnp.concatenate(
        [jnp.where(low if h % 2 == 0 else jnp.logical_not(low), iq[:, (h // 2) * LANES:(h // 2 + 1) * LANES], 0.0).T
         for h in range(IDX_HEADS)], axis=1)
    w_t = w_ref[0].T
    key_pos = _iota((BK, TQ), 0)
    q_pos = qi * TQ + _iota((BK, TQ), 1)
    n_tiles = qi + 1

    def score_body(kb, _):
        ks = pl.multiple_of(kb * BK, BK)
        sc = jnp.maximum(_dot3(ik_ref[0, pl.ds(ks, BK), :], iq_t), 0.0)
        tot = w_t[MISC_IDXW:MISC_IDXW + 1, :] * sc[:, 0:TQ]
        for h in range(1, IDX_HEADS):
            tot = tot + w_t[MISC_IDXW + h:MISC_IDXW + h + 1, :] * sc[:, h * TQ:(h + 1) * TQ]
        tot = jnp.where(ks + key_pos <= q_pos, tot, -jnp.inf)
        key_ref[kb] = _order_key(tot)
        return 0

    lax.fori_loop(0, n_tiles, score_body, 0)

    def count_where(pred):
        def body(kb, acc):
            return acc + jnp.where(pred(key_ref[kb]), 1.0, 0.0)
        return jnp.sum(lax.fori_loop(0, n_tiles, body, jnp.zeros((BK, TQ), F32)), axis=0, keepdims=True)

    kappa = _kth_largest_key(lambda c: count_where(lambda key: key >= c), (1, TQ), float(topk))
    need = float(topk) - count_where(lambda key: key > kappa)
    earlier = jnp.where(_iota((BK, BK), 1) < _iota((BK, BK), 0), 1.0, 0.0).astype(BF16)

    def attn_body(kb, carry):
        ties_before, m, l, acc = carry
        ks = pl.multiple_of(kb * BK, BK)
        key = key_ref[kb]
        tie = jnp.where(key == kappa, 1.0, 0.0)
        rank = _dot(earlier, tie.astype(BF16)) + ties_before
        chosen = jnp.where(key > kappa, 1.0, jnp.where(rank < need, tie, 0.0))
        chosen = jnp.where(key > NEG_INF_KEY, chosen, 0.0)
        ties_before = ties_before + jnp.sum(tie, axis=0, keepdims=True)
        k = k_ref[0, pl.ds(ks, BK), :].astype(BF16)
        return (ties_before,) + _softmax_step_t(_dot(k, qst), [chosen > 0.5] * N_Q_HEADS,
                                                v_ref[0, pl.ds(ks, BK), :], m, l, acc)

    _, _, l, acc = lax.fori_loop(0, n_tiles, attn_body, (jnp.zeros((1, TQ), F32),) + _softmax_init_t())
    _store_heads(o_ref, _transpose_heads(acc / jnp.maximum(l, 1e-30)), TQ)


def dsa_attention(h):
    bsz, t, _ = h.shape
    return pl.pallas_call(
        functools.partial(_dsa_kernel, topk=min(DSA_TOPK_MAX, t // 4)),
        grid=(bsz, t // TQ),
        in_specs=[_h_spec(TQ, T_DSAQ, 4), _h_spec(TQ, T_IDXQ, 2), _h_spec(None, T_IDXK)(t),
                  _h_spec(TQ, T_MISC), _h_spec(None, T_DSA_K)(t), _h_spec(None, T_DSA_V)(t)],
        out_specs=pl.BlockSpec((1, TQ, MIX_W), lambda b, i: (b, i, 0)),
        out_shape=jax.ShapeDtypeStruct((bsz, t, MIX_W), F32),
        scratch_shapes=[pltpu.VMEM((t // BK, TQ, BK), I32)],
        compiler_params=_params("arbitrary", "arbitrary"),
        name="dsa_attention",
    )(h, h, h, h, h, h)


PAGES_PER_STEP = 16


def _page_specs(rows, row_block, layer, n_chunks, reverse, page):
    def spec(p):
        def index(b, c, pt_ref, *_):
            chunk = (n_chunks - 1 - c) if reverse else c
            return (layer, pt_ref[b, chunk * PAGES_PER_STEP + p], row_block, 0)
        return index
    return [pl.BlockSpec((1, 1, rows, page), spec(p)) for p in range(PAGES_PER_STEP)]


def _suffix_sums(x):
    lane = _iota(x.shape, 1)
    y = jnp.where(lane < LANES - 1, pltpu.roll(x, LANES - 1, 1), 0.0)
    shift = 1
    while shift < LANES:
        y = y + jnp.where(lane < LANES - shift, pltpu.roll(y, LANES - shift, 1), 0.0)
        shift *= 2
    return y


def _softmax_pages(zs, masks, vts, m, l, acc):
    if masks is not None:
        zs = [jnp.where(mk, z, NEG) for z, mk in zip(zs, masks)]
    top = zs[0]
    for z in zs[1:]:
        top = jnp.maximum(top, z)
    m_new = jnp.maximum(m, jnp.max(top, axis=-1, keepdims=True))
    alpha = jnp.exp(m - m_new)
    ps = [jnp.exp(z - m_new) for z in zs]
    if masks is not None:
        ps = [jnp.where(mk, p, 0.0) for p, mk in zip(ps, masks)]
    total = ps[0]
    for p in ps[1:]:
        total = total + p
    pv = _dot_nt(ps[0].astype(BF16), vts[0])
    for p, vt in zip(ps[1:], vts[1:]):
        pv = pv + _dot_nt(p.astype(BF16), vt)
    return m_new, alpha * l + jnp.sum(total, axis=-1, keepdims=True), alpha * acc + pv


def _row_of(ref, b):
    return ref[pl.ds(b, 1), :]


def _diag_column(row, lane0, stride):
    wide = jnp.broadcast_to(row, (N_Q_HEADS, LANES))
    pick = _iota((N_Q_HEADS, LANES), 1) == lane0 + stride * _iota((N_Q_HEADS, LANES), 0)
    return jnp.sum(jnp.where(pick, wide, 0.0), axis=-1, keepdims=True)


def _softmax_token(z, v_row, m, l, acc):
    m_new = jnp.maximum(m, z)
    alpha = jnp.exp(m - m_new)
    p = jnp.exp(z - m_new)
    return m_new, alpha * l + p, alpha * acc + p * v_row


def _sb_dec_kernel(pt_ref, q_ref, *refs):
    pages = refs[:PAGES_PER_STEP]
    o_ref, acc_ref, run_ref = refs[PAGES_PER_STEP:]
    c = pl.program_id(1)

    @pl.when(c == 0)
    def _():
        acc_ref[...] = jnp.zeros_like(acc_ref)
        run_ref[...] = jnp.zeros_like(run_ref)

    qs = (q_ref[0] * HEAD_DIM ** -0.5).astype(BF16)
    lss, later, totals = [], [], []
    for p in range(PAGES_PER_STEP):
        ls, ln = _log_sigmoid_pair(_dot(qs, pages[p][0, 0, 0:KV_W, :].astype(BF16)))
        within = _suffix_sums(ln)
        lss.append(ls)
        later.append(within)
        totals.append(within[:, 0:1] + ln[:, 0:1])
    acc = acc_ref[...]
    run = run_ref[:, 0:1]
    for p in reversed(range(PAGES_PER_STEP)):
        a = jnp.exp(lss[p] + later[p] + run)
        acc = acc + _dot_nt(a.astype(BF16), pages[p][0, 0, KV_W:2 * KV_W, :].astype(BF16))
        run = run + totals[p]
    acc_ref[...] = acc
    run_ref[...] = jnp.broadcast_to(run, run_ref.shape)

    @pl.when(c == pl.num_programs(1) - 1)
    def _():
        o_ref[0] = acc


def _decode_call(kernel, name, page_table, n_chunks, in_specs, args, out_specs, out_shape, scratch, n_prefetch=1,
                 grid=None, prefetch=None):
    bsz = page_table.shape[0]
    return pl.pallas_call(
        kernel,
        grid_spec=pltpu.PrefetchScalarGridSpec(
            num_scalar_prefetch=n_prefetch,
            grid=grid or (bsz, n_chunks),
            in_specs=in_specs,
            out_specs=out_specs,
            scratch_shapes=scratch),
        out_shape=out_shape,
        compiler_params=_params("arbitrary", "arbitrary"),
        name=name,
    )(*(prefetch or (page_table,)), *args)


def _q_spec():
    return pl.BlockSpec((1, N_Q_HEADS, LANES), lambda b, c, *_: (b, 0, 0))


def _hs_spec(bsz, tile):
    return pl.BlockSpec((bsz, LANES), lambda b, c, *_: (0, tile))


def _head_out(bsz):
    return (pl.BlockSpec((1, N_Q_HEADS, LANES), lambda b, c, *_: (b, 0, 0)),
            jax.ShapeDtypeStruct((bsz, N_Q_HEADS, LANES), F32))


def sb_decode(q, cache, layer, page_table):
    bsz, n_pages = page_table.shape
    n_chunks = n_pages // PAGES_PER_STEP
    page = cache.shape[3]
    out_spec, out_shape = _head_out(bsz)
    return _decode_call(
        _sb_dec_kernel, "sb_decode", page_table, n_chunks,
        [_q_spec()] + _page_specs(2 * KV_W, 0, layer, n_chunks, True, page),
        [q] + [cache] * PAGES_PER_STEP, out_spec, out_shape,
        [pltpu.VMEM((N_Q_HEADS, LANES), F32), pltpu.VMEM((N_Q_HEADS, LANES), F32)])


def _fox_dec_kernel(pt_ref, q_ref, kn_ref, vn_ref, misc_ref, bf_ref, *refs):
    pages = refs[:PAGES_PER_STEP]
    lf_pages = refs[PAGES_PER_STEP:2 * PAGES_PER_STEP]
    o_ref, lf_out_ref, m_ref, l_ref, acc_ref, run_ref = refs[2 * PAGES_PER_STEP:]
    b = pl.program_id(0)
    c = pl.program_id(1)
    qf = q_ref[0] * HEAD_DIM ** -0.5

    @pl.when(c == 0)
    def _():
        logf_row, _ = _log_sigmoid_pair(_row_of(misc_ref, b) + bf_ref[...])
        lf_out_ref[0] = logf_row
        z_new = jnp.sum(qf * _row_of(kn_ref, b), axis=-1, keepdims=True)
        m_ref[...] = jnp.broadcast_to(z_new, m_ref.shape)
        l_ref[...] = jnp.ones_like(l_ref)
        acc_ref[...] = jnp.broadcast_to(_row_of(vn_ref, b), acc_ref.shape)
        run_ref[...] = jnp.broadcast_to(_diag_column(logf_row, MISC_FOXF, 1), run_ref.shape)

    qs = qf.astype(BF16)
    run = run_ref[:, 0:1]
    zs = [None] * PAGES_PER_STEP
    for p in reversed(range(PAGES_PER_STEP)):
        lf = lf_pages[p][0, 0]
        later = _suffix_sums(lf)
        zs[p] = _dot(qs, pages[p][0, 0, 0:KV_W, :].astype(BF16)) + (later + run)
        run = run + later[:, 0:1] + lf[:, 0:1]
    vts = [pages[p][0, 0, KV_W:2 * KV_W, :].astype(BF16) for p in range(PAGES_PER_STEP)]
    state = _softmax_pages(zs, None, vts, m_ref[:, 0:1], l_ref[:, 0:1], acc_ref[...])
    m_ref[...] = jnp.broadcast_to(state[0], m_ref.shape)
    l_ref[...] = jnp.broadcast_to(state[1], l_ref.shape)
    acc_ref[...] = state[2]
    run_ref[...] = jnp.broadcast_to(run, run_ref.shape)

    @pl.when(c == pl.num_programs(1) - 1)
    def _():
        o_ref[0] = state[2] / state[1]


def fox_decode(q, hs, bias_row, cache, logf_t, layer, page_table):
    bsz, n_pages = page_table.shape
    n_chunks = n_pages // PAGES_PER_STEP
    page = cache.shape[3]
    out_spec, out_shape = _head_out(bsz)
    stat = pltpu.VMEM((N_Q_HEADS, LANES), F32)
    return _decode_call(
        _fox_dec_kernel, "fox_decode", page_table, n_chunks,
        [_q_spec(), _hs_spec(bsz, T_FOX_K), _hs_spec(bsz, T_FOX_V), _hs_spec(bsz, T_MISC),
         pl.BlockSpec((1, LANES), lambda b, c, *_: (0, 0))]
        + _page_specs(2 * KV_W, 0, layer, n_chunks, True, page)
        + _page_specs(N_Q_HEADS, 0, layer, n_chunks, True, page),
        [q, hs, hs, hs, bias_row] + [cache] * PAGES_PER_STEP + [logf_t] * PAGES_PER_STEP,
        [out_spec, pl.BlockSpec((1, 1, LANES), lambda b, c, *_: (b, 0, 0))],
        [out_shape, jax.ShapeDtypeStruct((bsz, 1, LANES), F32)],
        [stat, stat, stat, stat])


def _nsa_dec_cmp_kernel(pt_ref, q_ref, pek_ref, pev_ref, wk_ref, wv_ref, *refs, n_blocks_past, k_sel):
    pages = refs[:PAGES_PER_STEP]
    ocmp_ref, sel_ref, sk_ref, sv_ref = refs[PAGES_PER_STEP:]
    c = pl.program_id(1)
    page = pages[0].shape[3]
    blocks_per_page = page // NSA_BLOCK
    step_blocks = PAGES_PER_STEP * blocks_per_page
    steps_per_tile = LANES // step_blocks

    @pl.when(c == 0)
    def _():
        sk_ref[...] = jnp.zeros_like(sk_ref)
        sv_ref[...] = jnp.zeros_like(sv_ref)

    lane0 = (c % steps_per_tile) * step_blocks
    tok = _iota((page, LANES), 0)
    lane = _iota((page, LANES), 1)
    sum_k = jnp.zeros((KV_W, LANES), F32)
    sum_v = jnp.zeros((KV_W, LANES), F32)
    for p in range(PAGES_PER_STEP):
        place = jnp.where(lane == lane0 + p * blocks_per_page + (tok >> 6), 1.0, 0.0).astype(BF16)
        sum_k = sum_k + _dot_x_exact(pages[p][0, 0, 0:KV_W, :], place)
        sum_v = sum_v + _dot_x_exact(pages[p][0, 0, KV_W:2 * KV_W, :], place)
    sk_ref[c // steps_per_tile] += sum_k
    sv_ref[c // steps_per_tile] += sum_v

    @pl.when(c == pl.num_programs(1) - 1)
    def _():
        n_ptiles = sk_ref.shape[0]
        nb_pad = n_ptiles * LANES
        n_tiles = n_ptiles + 1
        qpos = n_blocks_past * NSA_BLOCK
        qf = q_ref[0] * HEAD_DIM ** -0.5
        pe_mean_k = jnp.mean(pek_ref[...], axis=-1, keepdims=True)
        pe_mean_v = jnp.mean(pev_ref[...], axis=-1, keepdims=True)
        cvs, zcs = [], []
        for t in range(n_ptiles):
            ck_t = _dot3(wk_ref[...], sk_ref[t] * (1.0 / NSA_BLOCK) + pe_mean_k)
            cvs.append(_dot3(wv_ref[...], sv_ref[t] * (1.0 / NSA_BLOCK) + pe_mean_v))
            zcs.append(_dot3(qf, ck_t))
        zc = zcs[0] if n_ptiles == 1 else jnp.concatenate(zcs, axis=1)
        blk = _iota(zc.shape, 1)
        vis = ((blk + 1) * NSA_BLOCK - 1 <= qpos) & (blk < n_blocks_past)
        m = jnp.max(jnp.where(vis, zc, NEG), axis=-1, keepdims=True)
        e = jnp.where(vis, jnp.exp(zc - m), 0.0)
        pc = e / jnp.maximum(jnp.sum(e, axis=-1, keepdims=True), 1e-30)
        o_cmp = _dot_nt(pc[:, 0:LANES].astype(BF16), cvs[0].astype(BF16))
        for t in range(1, n_ptiles):
            o_cmp = o_cmp + _dot_nt(pc[:, t * LANES:(t + 1) * LANES].astype(BF16), cvs[t].astype(BF16))
        ocmp_ref[0] = o_cmp
        n_blocks = n_blocks_past + 1
        rows = []
        for g in range(N_KV_HEADS):
            imp = jnp.sum(pc[g * GQA_GROUP:(g + 1) * GQA_GROUP], axis=0, keepdims=True)
            imp = jnp.where(vis[0:1], imp, -jnp.inf)
            rows.append(jnp.concatenate([imp, jnp.full((1, LANES), -jnp.inf, F32)], axis=1))
        imp_r = jnp.concatenate(rows + [jnp.full((LANES - N_KV_HEADS, n_tiles * LANES), -jnp.inf, F32)], axis=0)
        blk_r = _iota(imp_r.shape, 1)
        forced = (blk_r == qpos // NSA_BLOCK) | (blk_r == 0)
        imp_r = jnp.where(forced, jnp.inf, imp_r)
        imp_c = jnp.concatenate([imp_r[:, t * LANES:(t + 1) * LANES].T for t in range(n_tiles)], axis=0)
        n_col = _iota((n_tiles * LANES, LANES), 0)
        r_lane = _iota((n_tiles * LANES, LANES), 1)
        sel_ref[0] = jnp.zeros(sel_ref.shape[1:], I32)
        for g in range(N_KV_HEADS):
            col = imp_c[:, g:g + 1]
            rank = jnp.zeros((n_tiles * LANES, 1), F32)
            for t in range(n_tiles):
                row = imp_r[g:g + 1, t * LANES:(t + 1) * LANES]
                m_idx = t * LANES + _iota((n_tiles * LANES, LANES), 1)
                ahead = jnp.where(row > col, 1.0, jnp.where(row == col, jnp.where(m_idx < n_col, 1.0, 0.0), 0.0))
                ahead = jnp.where(m_idx < n_blocks, ahead, 0.0)
                rank = rank + jnp.sum(ahead, axis=-1, keepdims=True)
            hit = jnp.where((rank == r_lane.astype(F32)) & (n_col < n_blocks), 1.0, 0.0)
            idx_row = jnp.sum(hit * n_col.astype(F32), axis=0, keepdims=True)
            ok_row = jnp.sum(jnp.where(col > -jnp.inf, hit, 0.0), axis=0, keepdims=True)
            ok_row = jnp.where(_iota((1, LANES), 1) < k_sel, ok_row, 0.0)
            sel_ref[0, g:g + 1, :] = idx_row.astype(I32)
            sel_ref[0, N_KV_HEADS + g:N_KV_HEADS + g + 1, :] = ok_row.astype(I32)


def nsa_decode_compress(q, pe_k_t, pe_v_t, w_ck_t, w_cv_t, cache, layer, page_table):
    bsz, n_pages = page_table.shape
    n_chunks = n_pages // PAGES_PER_STEP
    page = cache.shape[3]
    n_blocks_past = n_pages * page // NSA_BLOCK
    assert n_blocks_past % LANES == 0 and LANES % (PAGES_PER_STEP * page // NSA_BLOCK) == 0
    k_sel = min(NSA_TOPK, n_blocks_past + 1)
    out_spec, out_shape = _head_out(bsz)
    full = lambda shape: pl.BlockSpec(shape, lambda b, c, *_: (0,) * len(shape))
    sums = pltpu.VMEM((n_blocks_past // LANES, KV_W, LANES), F32)
    return _decode_call(
        functools.partial(_nsa_dec_cmp_kernel, n_blocks_past=n_blocks_past, k_sel=k_sel),
        "nsa_decode_compress", page_table, n_chunks,
        [_q_spec(), full((LANES, NSA_BLOCK)), full((LANES, NSA_BLOCK)), full((LANES, LANES)), full((LANES, LANES))]
        + _page_specs(2 * KV_W, 0, layer, n_chunks, False, page),
        [q, pe_k_t, pe_v_t, w_ck_t, w_cv_t] + [cache] * PAGES_PER_STEP,
        [out_spec, pl.BlockSpec((1, N_Q_HEADS, LANES), lambda b, c, *_: (b, 0, 0))],
        [out_shape, jax.ShapeDtypeStruct((bsz, N_Q_HEADS, LANES), I32)],
        [sums, sums])


SELECTED_PER_STEP = 8


def _nsa_dec_sel_kernel(pt_ref, blk_ref, ok_ref, q_ref, kn_ref, vn_ref, *refs, n_blocks_past, k_sel):
    pages = refs[:SELECTED_PER_STEP]
    o_ref, m_ref, l_ref, acc_ref = refs[SELECTED_PER_STEP:]
    b = pl.program_id(0)
    s = pl.program_id(1)
    qf = q_ref[0] * HEAD_DIM ** -0.5

    @pl.when(s == 0)
    def _():
        m_ref[...] = jnp.full(m_ref.shape, NEG, F32)
        l_ref[...] = jnp.zeros_like(l_ref)
        acc_ref[...] = jnp.zeros_like(acc_ref)

    qs = qf.astype(BF16)
    blocks_per_page = pages[0].shape[3] // NSA_BLOCK
    zs, masks, vts = [], [], []
    for j in range(SELECTED_PER_STEP):
        i = s * SELECTED_PER_STEP + j
        blk = blk_ref[b, i]
        valid = (ok_ref[b, i] > 0) & (blk < n_blocks_past)
        z = _dot(qs, pages[j][0, 0, 0:KV_W, :].astype(BF16))
        in_block = (_iota(z.shape, 1) >> 6) == blk % blocks_per_page
        masks.append(jnp.where(in_block, _iota(z.shape, 0) >> 2, -2) == jnp.where(valid, i // k_sel, -1))
        zs.append(z)
        vts.append(pages[j][0, 0, KV_W:2 * KV_W, :].astype(BF16))
    state = _softmax_pages(zs, masks, vts, m_ref[:, 0:1], l_ref[:, 0:1], acc_ref[...])

    @pl.when(s < pl.num_programs(1) - 1)
    def _():
        m_ref[...] = jnp.broadcast_to(state[0], m_ref.shape)
        l_ref[...] = jnp.broadcast_to(state[1], l_ref.shape)
        acc_ref[...] = state[2]

    @pl.when(s == pl.num_programs(1) - 1)
    def _():
        z_new = jnp.sum(qf * _row_of(kn_ref, b), axis=-1, keepdims=True)
        _, l, acc = _softmax_token(z_new, _row_of(vn_ref, b), *state)
        o_ref[0] = acc / l


def nsa_decode_select(q, hs, sel, cache, layer, page_table):
    bsz, n_pages = page_table.shape
    page_len = cache.shape[3]
    n_blocks_past = n_pages * page_len // NSA_BLOCK
    k_sel = min(NSA_TOPK, n_blocks_past + 1)
    blocks_per_page = page_len // NSA_BLOCK
    blk = sel[:, 0:N_KV_HEADS, 0:k_sel].reshape(bsz, N_KV_HEADS * k_sel)
    ok = sel[:, N_KV_HEADS:2 * N_KV_HEADS, 0:k_sel].reshape(bsz, N_KV_HEADS * k_sel)
    out_spec, out_shape = _head_out(bsz)

    assert (N_KV_HEADS * k_sel) % SELECTED_PER_STEP == 0

    def page_index(j):
        def index(b, s, pt_ref, blk_ref, ok_ref):
            page = jnp.minimum(blk_ref[b, s * SELECTED_PER_STEP + j] // blocks_per_page, n_pages - 1)
            return (layer, pt_ref[b, page], 1, 0)
        return index

    stat = pltpu.VMEM((N_Q_HEADS, LANES), F32)
    return _decode_call(
        functools.partial(_nsa_dec_sel_kernel, n_blocks_past=n_blocks_past, k_sel=k_sel),
        "nsa_decode_select", page_table, None,
        [_q_spec(), _hs_spec(bsz, T_NSA_KS), _hs_spec(bsz, T_NSA_VS)]
        + [pl.BlockSpec((1, 1, 2 * KV_W, page_len), page_index(j)) for j in range(SELECTED_PER_STEP)],
        [q, hs, hs] + [cache] * SELECTED_PER_STEP, out_spec, out_shape, [stat, stat, stat],
        n_prefetch=3, grid=(bsz, N_KV_HEADS * k_sel // SELECTED_PER_STEP), prefetch=(page_table, blk, ok))


def _nsa_dec_win_kernel(q_ref, kn_ref, vn_ref, misc_ref, win_ref, ocmp_ref, osel_ref, o_ref):
    b = pl.program_id(0)
    qf = q_ref[0] * HEAD_DIM ** -0.5
    z = _dot(qf.astype(BF16), win_ref[0, 0, 0:KV_W, :].astype(BF16))
    state = _softmax_pages([z], None, [win_ref[0, 0, KV_W:2 * KV_W, :].astype(BF16)], *_softmax_init(N_Q_HEADS))
    z_new = jnp.sum(qf * _row_of(kn_ref, b), axis=-1, keepdims=True)
    _, l, acc = _softmax_token(z_new, _row_of(vn_ref, b), *state)
    gates = jax.nn.sigmoid(_row_of(misc_ref, b))
    o_ref[0] = (_diag_column(gates, MISC_NSAG, 3) * ocmp_ref[0] + _diag_column(gates, MISC_NSAG + 1, 3) * osel_ref[0]
                + _diag_column(gates, MISC_NSAG + 2, 3) * (acc / l))


def nsa_decode_combine(q, hs, win_state, layer, o_cmp, o_sel):
    bsz = q.shape[0]
    w_rows = win_state.shape[3]
    head = pl.BlockSpec((1, N_Q_HEADS, LANES), lambda b: (b, 0, 0))
    tile = lambda t: pl.BlockSpec((bsz, LANES), lambda b: (0, t))
    return pl.pallas_call(
        _nsa_dec_win_kernel,
        grid=(bsz,),
        in_specs=[head, tile(T_NSA_KW), tile(T_NSA_VW), tile(T_MISC),
                  pl.BlockSpec((1, 1, 2 * KV_W, w_rows), lambda b: (layer, b, 0, 0)), head, head],
        out_specs=head,
        out_shape=jax.ShapeDtypeStruct((bsz, N_Q_HEADS, LANES), F32),
        compiler_params=_params("arbitrary"),
        name="nsa_decode_combine",
    )(q, hs, hs, hs, win_state, o_cmp, o_sel)


def _dsa_dec_index_kernel(pt_ref, iq_ref, w_ref, kn_ref, *refs, topk):
    pages = refs[:PAGES_PER_STEP]
    mask_ref, flag_ref, key_ref = refs[PAGES_PER_STEP:]
    b = pl.program_id(0)
    c = pl.program_id(1)
    iq = iq_ref[0]
    w = w_ref[0]
    for p in range(PAGES_PER_STEP):
        sc = jnp.maximum(_dot3(iq, pages[p][0, 0]), 0.0)
        tot = jnp.sum(w * sc, axis=0, keepdims=True)
        key_ref[pl.ds(c * PAGES_PER_STEP + p, 1), :] = _order_key(tot)

    @pl.when(c == pl.num_programs(1) - 1)
    def _():
        k_new = _row_of(kn_ref, b)[:, 0:IDX_DIM]
        sc_new = jnp.maximum(jnp.sum(iq * k_new, axis=-1, keepdims=True), 0.0)
        key_new = _order_key(jnp.sum(w[:, 0:1] * sc_new, axis=0, keepdims=True))
        keys = key_ref[...]

        def total(x):
            return jnp.sum(jnp.sum(x, axis=-1, keepdims=True), axis=0, keepdims=True)

        def count(pred):
            return total(jnp.where(pred(keys), 1.0, 0.0)) + jnp.where(pred(key_new), 1.0, 0.0)

        k = float(topk)
        kappa = _kth_largest_key(lambda cand: count(lambda x: x >= cand), (1, 1), k)
        need = k - count(lambda x: x > kappa)
        tie = jnp.where(keys == kappa, 1.0, 0.0)
        n_rows = keys.shape[0]
        in_row = _dot(tie.astype(BF16), _strict_lower_ones(LANES))
        row_tot = jnp.broadcast_to(jnp.sum(tie, axis=-1, keepdims=True), (n_rows, LANES))
        rows_before = jnp.where(_iota((n_rows, n_rows), 1) < _iota((n_rows, n_rows), 0), 1.0, 0.0).astype(BF16)
        rank = in_row + _dot(rows_before, row_tot.astype(BF16))
        chosen = jnp.where(keys > kappa, 1.0, jnp.where(rank < need, tie, 0.0))
        mask_ref[0] = jnp.where(keys > NEG_INF_KEY, chosen, 0.0)
        new_in = jnp.where(key_new > kappa, 1.0, jnp.where((key_new == kappa) & (total(tie) < need), 1.0, 0.0))
        flag_ref[0] = jnp.broadcast_to(new_in, flag_ref.shape[1:])


def dsa_decode_index(iq, w_rows, hs, cache, layer, page_table):
    bsz, n_pages = page_table.shape
    n_chunks = n_pages // PAGES_PER_STEP
    rows = cache.shape[3]
    assert rows == LANES
    topk = min(DSA_TOPK_MAX, (n_pages * rows + 1) // 4)
    return _decode_call(
        functools.partial(_dsa_dec_index_kernel, topk=topk), "dsa_decode_index", page_table, n_chunks,
        [pl.BlockSpec((1, N_Q_HEADS, IDX_DIM), lambda b, c, *_: (b, 0, 0)), _q_spec(), _hs_spec(bsz, T_IDXK)]
        + _page_specs(IDX_DIM, 0, layer, n_chunks, False, rows),
        [iq, w_rows, hs] + [cache] * PAGES_PER_STEP,
        [pl.BlockSpec((1, n_pages, rows), lambda b, c, *_: (b, 0, 0)),
         pl.BlockSpec((1, N_Q_HEADS, LANES), lambda b, c, *_: (b, 0, 0))],
        [jax.ShapeDtypeStruct((bsz, n_pages, rows), F32), jax.ShapeDtypeStruct((bsz, N_Q_HEADS, LANES), F32)],
        [pltpu.VMEM((n_pages, rows), I32)])


def _dsa_dec_attn_kernel(pt_ref, q_ref, kn_ref, vn_ref, mask_ref, flag_ref, *refs):
    pages = refs[:PAGES_PER_STEP]
    o_ref, m_ref, l_ref, acc_ref = refs[PAGES_PER_STEP:]
    b = pl.program_id(0)
    c = pl.program_id(1)
    qf = q_ref[0] * HEAD_DIM ** -0.5

    @pl.when(c == 0)
    def _():
        m_ref[...] = jnp.full(m_ref.shape, NEG, F32)
        l_ref[...] = jnp.zeros_like(l_ref)
        acc_ref[...] = jnp.zeros_like(acc_ref)

    qs = qf.astype(BF16)
    zs = [_dot(qs, pages[p][0, 0, 0:KV_W, :].astype(BF16)) for p in range(PAGES_PER_STEP)]
    masks = [jnp.broadcast_to(mask_ref[0, p:p + 1, :], (N_Q_HEADS, LANES)) > 0.5 for p in range(PAGES_PER_STEP)]
    vts = [pages[p][0, 0, KV_W:2 * KV_W, :].astype(BF16) for p in range(PAGES_PER_STEP)]
    state = _softmax_pages(zs, masks, vts, m_ref[:, 0:1], l_ref[:, 0:1], acc_ref[...])
    m_ref[...] = jnp.broadcast_to(state[0], m_ref.shape)
    l_ref[...] = jnp.broadcast_to(state[1], l_ref.shape)
    acc_ref[...] = state[2]

    @pl.when(c == pl.num_programs(1) - 1)
    def _():
        z_new = jnp.sum(qf * _row_of(kn_ref, b), axis=-1, keepdims=True)
        z_new = jnp.where(flag_ref[0][:, 0:1] > 0.5, z_new, NEG)
        m, l, acc = state
        m_new = jnp.maximum(m, z_new)
        alpha = jnp.exp(m - m_new)
        p_new = jnp.where(flag_ref[0][:, 0:1] > 0.5, jnp.exp(z_new - m_new), 0.0)
        l = alpha * l + p_new
        acc = alpha * acc + p_new * _row_of(vn_ref, b)
        o_ref[0] = acc / jnp.maximum(l, 1e-30)


def dsa_decode_attention(q, hs, mask, flag, cache, layer, page_table):
    bsz, n_pages = page_table.shape
    n_chunks = n_pages // PAGES_PER_STEP
    rows = cache.shape[3]
    out_spec, out_shape = _head_out(bsz)
    stat = pltpu.VMEM((N_Q_HEADS, LANES), F32)
    return _decode_call(
        _dsa_dec_attn_kernel, "dsa_decode_attention", page_table, n_chunks,
        [_q_spec(), _hs_spec(bsz, T_DSA_K), _hs_spec(bsz, T_DSA_V),
         pl.BlockSpec((1, PAGES_PER_STEP, rows), lambda b, c, *_: (b, c, 0)),
         pl.BlockSpec((1, N_Q_HEADS, LANES), lambda b, c, *_: (b, 0, 0))]
        + _page_specs(2 * KV_W, 0, layer, n_chunks, False, rows),
        [q, hs, hs, mask, flag] + [cache] * PAGES_PER_STEP, out_spec, out_shape, [stat, stat, stat])


def _rope_tables(pos):
    half = HEAD_DIM // 2
    freq = ROPE_THETA ** (-jnp.arange(half, dtype=F32) / half)
    ang = pos.astype(F32)[:, None] * freq[None, :]
    cos = jnp.cos(ang)
    sin = jnp.sin(ang)
    return jnp.tile(cos, (1, 4)), jnp.tile(jnp.concatenate([-sin, sin], axis=1), (1, 2))


def _prepare_weights(p):
    depth = p['w_in'].shape[0]
    w_in = jnp.concatenate([p['w_in'], jnp.zeros((depth, D_MODEL, 1), F32)], axis=-1)
    eye2 = jnp.eye(N_KV_HEADS, dtype=F32)
    w = {
        'w_in': jnp.take(w_in, jnp.asarray(_proj_column_perm()), axis=-1).astype(BF16),
        'w_gate': p['w_gate'].astype(BF16),
        'b_gate': p['b_gate'].reshape(depth, 1, N_BRANCH * D_MODEL),
        'w_br': jnp.take(p['w_br'], jnp.asarray(_y_row_perm()), axis=2).astype(BF16),
        'w_o': p['w_o'].astype(BF16),
        'w_mq': p['w_mq'].astype(BF16),
        'w_mo': p['w_mo'].astype(BF16),
        'w_mkv': p['w_mkv'].astype(BF16),
        'w_router': jnp.concatenate(
            [p['w_rg'], p['w_re'], jnp.zeros((depth, D_MODEL, LANES - N_GROUPS - N_EXPERTS), F32)], axis=-1),
        'b_router': jnp.concatenate(
            [p['b_rg'], p['b_re'], jnp.zeros((depth, LANES - N_GROUPS - N_EXPERTS), F32)], axis=-1)[:, None, :],
        'w_eg': p['w_eg'].astype(BF16),
        'w_eu': p['w_eu'].astype(BF16),
        'w_ed': p['w_ed'].astype(BF16),
        'pe_k': jnp.tile(p['nsa_pe_k'], (1, 1, N_KV_HEADS)),
        'pe_v': jnp.tile(p['nsa_pe_v'], (1, 1, N_KV_HEADS)),
        'w_ck': jnp.einsum('gh,lde->lgdhe', eye2, p['nsa_w_ck']).reshape(depth, LANES, LANES),
        'w_cv': jnp.einsum('gh,lde->lgdhe', eye2, p['nsa_w_cv']).reshape(depth, LANES, LANES),
    }
    for name in ('pe_k', 'pe_v', 'w_ck', 'w_cv'):
        w[name + '_t'] = jnp.swapaxes(w[name], 1, 2)
    for name in ('fox_b_f', 'ln1_g', 'ln1_b', 'ln2_g', 'ln2_b', 'ln3_g', 'ln3_b'):
        w[name] = p[name]
    return w


def _tiles(h, tiles):
    parts = [h[:, :, t * LANES:(t + 1) * LANES] for t in tiles]
    return jnp.stack(parts, axis=2).reshape(h.shape[0], h.shape[1], len(tiles), N_KV_HEADS, HEAD_DIM)


def _new_state(h, logf):
    return {
        'sb_kv': _tiles(h, (T_SB_K, T_SB_V)),
        'nsa_kv': _tiles(h, (T_NSA_KC, T_NSA_VC, T_NSA_KS, T_NSA_VS)),
        'nsa_win_kv': _tiles(h, (T_NSA_KW, T_NSA_VW)),
        'dsa_kv': _tiles(h, (T_DSA_K, T_DSA_V)),
        'dsa_idx_k': h[:, :, T_IDXK * LANES:T_IDXK * LANES + IDX_DIM],
        'fox_kv': _tiles(h, (T_FOX_K, T_FOX_V)),
        'fox_logf': logf[:, :, MISC_FOXF:MISC_FOXF + N_Q_HEADS],
    }


def _dense_tail(x, ys, mem_kv, layer, w, alpha, tm, tm_mem, bsz):
    n = x.shape[0]
    l = layer
    merged = gated_merge(x, ys, w['w_gate'][l], w['b_gate'][l], w['w_br'][l], tm)
    x = matmul_res_ln(merged, w['w_o'][l], x, w['ln1_g'][l], w['ln1_b'][l], alpha, min(tm, 256))
    t = n // bsz
    xm = x.reshape(bsz, t, D_MODEL)
    if t < tm_mem:
        xm = jnp.broadcast_to(xm[:, 0:1], (bsz, tm_mem, D_MODEL))
    x = memory_block(xm, mem_kv, l, w['w_mq'][l], w['w_mo'][l], w['ln2_g'][l], w['ln2_b'][l], alpha,
                     tm_mem)[:, 0:t].reshape(n, D_MODEL)
    return moe_block(x, w['w_router'][l], w['b_router'][l], w['w_eg'][l], w['w_eu'][l], w['w_ed'][l],
                     w['ln3_g'][l], w['ln3_b'][l], alpha, tm)


def _prompt_layer(x, mem_kv, layer, w, rope, alpha, bsz):
    n = x.shape[0]
    t = n // bsz
    l = layer
    tm = min(512, t)
    h = project(x, w['w_in'][l], rope[0], rope[1], tm).reshape(bsz, t, PROJ_W)
    logf, c_tok = fox_prep(h, w['fox_b_f'][l])
    cmp_k, cmp_v = nsa_compress(h, w['pe_k'][l], w['pe_v'][l], w['w_ck'][l], w['w_cv'][l])
    ys = [sb_attention(h), nsa_attention(h, cmp_k, cmp_v), dsa_attention(h), fox_attention(h, c_tok)]
    ys = [y.reshape(n, MIX_W) for y in ys]
    x = _dense_tail(x, ys, mem_kv, l, w, alpha, tm, min(256, t), bsz)
    return x, _new_state(h, logf)


def _feature_major_caches(cache_sb_kv, cache_nsa_kv, state_nsa_win_kv, cache_dsa_kv, cache_dsa_idx_k,
                          cache_fox_kv, cache_fox_logf):
    def rows_last(c):
        t = jnp.moveaxis(c, 2, -1)
        return t.reshape(t.shape[0], t.shape[1], -1, t.shape[-1])
    return {
        'sb_kv': rows_last(cache_sb_kv),
        'nsa_kv': rows_last(cache_nsa_kv),
        'nsa_win': rows_last(state_nsa_win_kv),
        'dsa_kv': rows_last(cache_dsa_kv),
        'dsa_idx_k': rows_last(cache_dsa_idx_k),
        'fox_kv': rows_last(cache_fox_kv),
        'fox_logf_t': rows_last(cache_fox_logf),
    }


def _decode_q(hs, tile0):
    q4 = hs[:, tile0 * LANES:(tile0 + GQA_GROUP) * LANES].reshape(hs.shape[0], GQA_GROUP, LANES)
    low = jnp.arange(LANES) < HEAD_DIM
    return jnp.concatenate([jnp.where(low, q4, 0.0), jnp.where(low, 0.0, q4)], axis=1)


def _decode_y(o):
    low = jnp.arange(LANES) < HEAD_DIM
    return jnp.where(low, o[:, 0:GQA_GROUP], o[:, GQA_GROUP:]).reshape(o.shape[0], MIX_W)


def _sample_layer(x, caches, mem_kv, layer, w, rope, alpha, page_table):
    bsz = x.shape[0]
    l = layer
    hs = project(x, w['w_in'][l], rope[0], rope[1], bsz)
    q_nsa = _decode_q(hs, T_NSAQ)
    o_cmp, sel = nsa_decode_compress(q_nsa, w['pe_k_t'][l], w['pe_v_t'][l], w['w_ck_t'][l], w['w_cv_t'][l],
                                     caches['nsa_kv'], l, page_table)
    o_sel = nsa_decode_select(q_nsa, hs, sel, caches['nsa_kv'], l, page_table)
    y_nsa = nsa_decode_combine(q_nsa, hs, caches['nsa_win'], l, o_cmp, o_sel)
    iq = hs[:, T_IDXQ * LANES:(T_IDXQ + 2) * LANES].reshape(bsz, IDX_HEADS, IDX_DIM)
    iq = jnp.concatenate([iq, jnp.zeros((bsz, N_Q_HEADS - IDX_HEADS, IDX_DIM), F32)], axis=1)
    iw = hs[:, T_MISC * LANES + MISC_IDXW:T_MISC * LANES + MISC_IDXW + IDX_HEADS]
    iw = jnp.concatenate([iw, jnp.zeros((bsz, N_Q_HEADS - IDX_HEADS), F32)], axis=1)
    iw = jnp.broadcast_to(iw[:, :, None], (bsz, N_Q_HEADS, LANES))
    mask, flag = dsa_decode_index(iq, iw, hs, caches['dsa_idx_k'], l, page_table)
    y_dsa = dsa_decode_attention(_decode_q(hs, T_DSAQ), hs, mask, flag, caches['dsa_kv'], l, page_table)
    bias_row = jnp.zeros((1, LANES), F32).at[0, MISC_FOXF:MISC_FOXF + N_Q_HEADS].set(w['fox_b_f'][l])
    y_fox, logf = fox_decode(_decode_q(hs, T_FOXQ), hs, bias_row, caches['fox_kv'], caches['fox_logf_t'], l,
                             page_table)
    y_sb = sb_decode(_decode_q(hs, T_SBQ), caches['sb_kv'], l, page_table)
    ys = [_decode_y(y) for y in (y_sb, y_nsa, y_dsa, y_fox)]
    x = _dense_tail(x, ys, mem_kv, l, w, alpha, bsz, 8, bsz)
    return x, _new_state(hs[:, None, :], logf)


def kernel(x_prompt, x_sample, mem_prompt, cache_sb_kv, cache_nsa_kv, state_nsa_win_kv, cache_dsa_kv,
           cache_dsa_idx_k, cache_fox_kv, cache_fox_logf, cache_mem_kv, page_table,
           ln_in_g, ln_in_b, w_in, fox_b_f, nsa_pe_k, nsa_pe_v, nsa_w_ck, nsa_w_cv, w_br, w_gate, b_gate, w_o,
           ln1_g, ln1_b, w_mq, w_mkv, w_mo, ln2_g, ln2_b, w_rg, b_rg, w_re, b_re, w_eg, w_eu, w_ed, ln3_g, ln3_b):
    depth = w_in.shape[0]
    bsz, seq, d = x_prompt.shape
    dec_b, dec_seq, _ = x_sample.shape
    n_pool, page = cache_sb_kv.shape[1:3]
    past_len = page_table.shape[1] * page
    assert dec_seq == 1 and state_nsa_win_kv.shape[2] == NSA_WINDOW and past_len >= NSA_WINDOW
    alpha = (2.0 * depth) ** 0.25
    w = _prepare_weights(dict(
        w_in=w_in, fox_b_f=fox_b_f, nsa_pe_k=nsa_pe_k, nsa_pe_v=nsa_pe_v, nsa_w_ck=nsa_w_ck, nsa_w_cv=nsa_w_cv,
        w_br=w_br, w_gate=w_gate, b_gate=b_gate, w_o=w_o, ln1_g=ln1_g, ln1_b=ln1_b, w_mq=w_mq, w_mkv=w_mkv,
        w_mo=w_mo, ln2_g=ln2_g, ln2_b=ln2_b, w_rg=w_rg, b_rg=b_rg, w_re=w_re, b_re=b_re, w_eg=w_eg, w_eu=w_eu,
        w_ed=w_ed, ln3_g=ln3_g, ln3_b=ln3_b))

    mem_rows = mem_prompt.reshape(bsz * N_MEM, d)
    mem_kv_p = jnp.stack([matmul(mem_rows, w['w_mkv'][l], 256, 512) for l in range(depth)], axis=0)
    mem_kv_p = mem_kv_p.reshape(depth, bsz, N_MEM, 2 * MEM_W)
    rope_p = _rope_tables(jnp.arange(seq))
    x = layer_norm_rows(x_prompt.reshape(bsz * seq, d), ln_in_g, ln_in_b, 256)
    st_p = []
    for l in range(depth):
        x, new = _prompt_layer(x, mem_kv_p, l, w, rope_p, alpha, bsz)
        st_p.append(new)
    y_prompt = x.reshape(bsz, seq, d)

    caches = _feature_major_caches(cache_sb_kv, cache_nsa_kv, state_nsa_win_kv, cache_dsa_kv, cache_dsa_idx_k,
                                   cache_fox_kv, cache_fox_logf)
    mem_kv_s = cache_mem_kv.reshape(depth, dec_b, N_MEM, 2 * MEM_W)
    rope_s = _rope_tables(jnp.full((dec_b,), past_len))
    x = layer_norm_rows(x_sample.reshape(dec_b, d), ln_in_g, ln_in_b, dec_b)
    st_s = []
    for l in range(depth):
        x, new = _sample_layer(x, caches, mem_kv_s, l, w, rope_s, alpha, page_table)
        new['nsa_win_kv'] = jnp.concatenate([state_nsa_win_kv[l][:, 1:], new['nsa_win_kv']], axis=1)
        st_s.append(new)
    y_sample = x.reshape(dec_b, dec_seq, d)

    def stacked(states, name):
        return jnp.stack([s[name] for s in states], axis=0)

    win_p = stacked(st_p, 'nsa_win_kv')[:, :, seq - min(NSA_WINDOW, seq):]
    return (y_prompt, y_sample,
            stacked(st_p, 'sb_kv'), stacked(st_s, 'sb_kv'),
            stacked(st_p, 'nsa_kv'), stacked(st_s, 'nsa_kv'),
            win_p, stacked(st_s, 'nsa_win_kv'),
            stacked(st_p, 'dsa_kv'), stacked(st_s, 'dsa_kv'),
            stacked(st_p, 'dsa_idx_k'), stacked(st_s, 'dsa_idx_k'),
            stacked(st_p, 'fox_kv'), stacked(st_s, 'fox_kv'),
            stacked(st_p, 'fox_logf'), stacked(st_s, 'fox_logf'),
            mem_kv_p.reshape(depth, bsz, N_MEM, 2, MEM_HEADS, MEM_HEAD_DIM))
```

```python
import functools
import math

import jax
import jax.numpy as jnp
import numpy as np
from jax import lax
from jax.experimental import pallas as pl
from jax.experimental.pallas import tpu as pltpu

F32 = jnp.float32
BF16 = jnp.bfloat16
I32 = jnp.int32

D_MODEL = 2048
HEAD_DIM = 64
N_Q_HEADS = 8
N_KV_HEADS = 2
GQA_GROUP = 4
MIX_W = 512
KV_W = 128
N_BRANCH = 4
ROPE_THETA = 10000.0
LN_EPS = 1e-5
NSA_BLOCK = 64
NSA_TOPK = 16
NSA_WINDOW = 512
IDX_HEADS = 4
IDX_DIM = 64
DSA_TOPK_MAX = 256
N_MEM = 256
MEM_HEADS = 4
MEM_HEAD_DIM = 128
MEM_W = 512
N_GROUPS = 4
EXPERTS_PER_GROUP = 4
N_EXPERTS = 16
D_FF_EXPERT = 256

LANES = 128
NEG = -1e30
F32_EXP_ZERO = -104.0
VMEM_LIMIT = 48 * 1024 * 1024

IN_SPLITS = (
    ('sb_q', MIX_W), ('sb_k', KV_W), ('sb_v', KV_W),
    ('nsa_q', MIX_W), ('nsa_kc', KV_W), ('nsa_vc', KV_W), ('nsa_ks', KV_W), ('nsa_vs', KV_W),
    ('nsa_kw', KV_W), ('nsa_vw', KV_W), ('nsa_g', 3 * N_Q_HEADS),
    ('dsa_q', MIX_W), ('dsa_k', KV_W), ('dsa_v', KV_W),
    ('idx_q', IDX_HEADS * IDX_DIM), ('idx_k', IDX_DIM), ('idx_w', IDX_HEADS),
    ('fox_q', MIX_W), ('fox_k', KV_W), ('fox_v', KV_W), ('fox_f', N_Q_HEADS),
)
N_IN = sum(w for _, w in IN_SPLITS)

T_NSAQ, T_DSAQ, T_IDXQ, T_IDXK, T_NSA_KC, T_NSA_KS, T_NSA_KW, T_DSA_K = 0, 4, 8, 10, 11, 12, 13, 14
N_ROPE_TILES = 16
T_SBQ, T_FOXQ, T_SB_K, T_SB_V, T_NSA_VC, T_NSA_VS, T_NSA_VW, T_DSA_V, T_FOX_K, T_FOX_V, T_MISC = (
    16, 20, 24, 25, 26, 27, 28, 29, 30, 31, 32)
N_TILES = 34
PROJ_W = N_TILES * LANES
MISC_FOXF, MISC_NSAG, MISC_IDXW = 0, 8, 32


def _in_offsets():
    off, out = 0, {}
    for name, width in IN_SPLITS:
        out[name] = off
        off += width
    return out


def _proj_column_perm():
    off = _in_offsets()
    perm = np.full((PROJ_W,), N_IN, np.int32)

    def put(tile, lane, name, start, width):
        base = tile * LANES + lane
        perm[base:base + width] = off[name] + start + np.arange(width)

    def put_q(tile0, name):
        for r in range(GQA_GROUP):
            put(tile0 + r, 0, name, r * HEAD_DIM, HEAD_DIM)
            put(tile0 + r, HEAD_DIM, name, (GQA_GROUP + r) * HEAD_DIM, HEAD_DIM)

    put_q(T_NSAQ, 'nsa_q')
    put_q(T_DSAQ, 'dsa_q')
    put_q(T_SBQ, 'sb_q')
    put_q(T_FOXQ, 'fox_q')
    put(T_IDXQ, 0, 'idx_q', 0, 128)
    put(T_IDXQ + 1, 0, 'idx_q', 128, 128)
    put(T_IDXK, 0, 'idx_k', 0, 64)
    put(T_IDXK, 64, 'idx_k', 0, 64)
    for tile, name in ((T_NSA_KC, 'nsa_kc'), (T_NSA_KS, 'nsa_ks'), (T_NSA_KW, 'nsa_kw'), (T_DSA_K, 'dsa_k'),
                       (T_SB_K, 'sb_k'), (T_SB_V, 'sb_v'), (T_NSA_VC, 'nsa_vc'), (T_NSA_VS, 'nsa_vs'),
                       (T_NSA_VW, 'nsa_vw'), (T_DSA_V, 'dsa_v'), (T_FOX_K, 'fox_k'), (T_FOX_V, 'fox_v')):
        put(tile, 0, name, 0, 128)
    put(T_MISC, MISC_FOXF, 'fox_f', 0, 8)
    put(T_MISC, MISC_NSAG, 'nsa_g', 0, 24)
    put(T_MISC, MISC_IDXW, 'idx_w', 0, 4)
    return perm


def _y_row_perm():
    perm = np.zeros((MIX_W,), np.int32)
    for r in range(GQA_GROUP):
        for half in range(2):
            h = half * GQA_GROUP + r
            perm[r * LANES + half * HEAD_DIM:r * LANES + (half + 1) * HEAD_DIM] = h * HEAD_DIM + np.arange(HEAD_DIM)
    return perm


def _dot(a, b):
    return jnp.dot(a, b, preferred_element_type=F32)


def _dot_nt(a, b):
    return lax.dot_general(a, b, (((1,), (1,)), ((), ())), preferred_element_type=F32)


def _split2(x):
    hi = x.astype(BF16)
    lo = (x - hi.astype(F32)).astype(BF16)
    return hi, lo


def _dot_x_exact(x, m_bf16):
    hi, lo = _split2(x)
    return _dot(hi, m_bf16) + _dot(lo, m_bf16)


def _dot_exact_x(m_bf16, x):
    hi, lo = _split2(x)
    return _dot(m_bf16, hi) + _dot(m_bf16, lo)


def _dot3(a, b):
    ah, al = _split2(a)
    bh, bl = _split2(b)
    return _dot(ah, bh) + (_dot(ah, bl) + _dot(al, bh))


def _dot3_nt(a, b):
    ah, al = _split2(a)
    bh, bl = _split2(b)
    return _dot_nt(ah, bh) + (_dot_nt(ah, bl) + _dot_nt(al, bh))


def _iota(shape, dim):
    return lax.broadcasted_iota(I32, shape, dim)


def _layer_norm(v, g, b):
    mu = jnp.mean(v, axis=-1, keepdims=True)
    c = v - mu
    var = jnp.mean(c * c, axis=-1, keepdims=True)
    return c * lax.rsqrt(var + LN_EPS) * g + b


def _stack_q(q, scale):
    tq = q.shape[0]
    low = _iota((tq, LANES), 1) < HEAD_DIM
    parts = []
    for g in range(N_KV_HEADS):
        for r in range(GQA_GROUP):
            t = q[:, r * LANES:(r + 1) * LANES]
            parts.append(jnp.where(low if g == 0 else jnp.logical_not(low), t * scale, 0.0))
    return jnp.concatenate(parts, axis=0)


def _store_heads(o_ref, o, tq):
    low = _iota((tq, LANES), 1) < HEAD_DIM
    for r in range(GQA_GROUP):
        lo_head = o[r * tq:(r + 1) * tq]
        hi_head = o[(GQA_GROUP + r) * tq:(GQA_GROUP + r + 1) * tq]
        o_ref[0, :, r * LANES:(r + 1) * LANES] = jnp.where(low, lo_head, hi_head)


def _rep_rows(x, n):
    return jnp.concatenate([x] * n, axis=0)


def _params(*sem):
    return pltpu.CompilerParams(dimension_semantics=sem, vmem_limit_bytes=VMEM_LIMIT)


def _ln_kernel(x_ref, g_ref, b_ref, o_ref):
    o_ref[...] = _layer_norm(x_ref[...], g_ref[...], b_ref[...])


def layer_norm_rows(x, g, b, tm):
    n, d = x.shape
    return pl.pallas_call(
        _ln_kernel,
        grid=(n // tm,),
        in_specs=[pl.BlockSpec((tm, d), lambda i: (i, 0)),
                  pl.BlockSpec((1, d), lambda i: (0, 0)),
                  pl.BlockSpec((1, d), lambda i: (0, 0))],
        out_specs=pl.BlockSpec((tm, d), lambda i: (i, 0)),
        out_shape=jax.ShapeDtypeStruct((n, d), F32),
        compiler_params=_params("arbitrary"),
        name="ln_in",
    )(x, g.reshape(1, d), b.reshape(1, d))


def _proj_kernel(x_ref, w_ref, cos_ref, sin_ref, o_ref, xb_ref, *, n_rope_blocks, tn):
    j = pl.program_id(1)

    @pl.when(j == 0)
    def _():
        xb_ref[...] = x_ref[...].astype(BF16)

    acc = _dot(xb_ref[...], w_ref[...])

    @pl.when(j < n_rope_blocks)
    def _():
        cos = cos_ref[...]
        sin = sin_ref[...]
        first = (_iota(cos.shape, 1) & (HEAD_DIM - 1)) < HEAD_DIM // 2
        for s in range(tn // LANES):
            a = acc[:, s * LANES:(s + 1) * LANES]
            rot = jnp.where(first, pltpu.roll(a, LANES - HEAD_DIM // 2, 1), pltpu.roll(a, HEAD_DIM // 2, 1))
            o_ref[:, s * LANES:(s + 1) * LANES] = a * cos + rot * sin

    @pl.when(j >= n_rope_blocks)
    def _():
        o_ref[...] = acc


def project(x, w_perm, cos_t, sin_t, tm, tn=256):
    n, d = x.shape
    n_pos_blocks = cos_t.shape[0] // tm
    return pl.pallas_call(
        functools.partial(_proj_kernel, n_rope_blocks=N_ROPE_TILES * LANES // tn, tn=tn),
        grid=(n // tm, PROJ_W // tn),
        in_specs=[pl.BlockSpec((tm, d), lambda i, j: (i, 0)),
                  pl.BlockSpec((d, tn), lambda i, j: (0, j)),
                  pl.BlockSpec((tm, LANES), lambda i, j: (i % n_pos_blocks, 0)),
                  pl.BlockSpec((tm, LANES), lambda i, j: (i % n_pos_blocks, 0))],
        out_specs=pl.BlockSpec((tm, tn), lambda i, j: (i, j)),
        out_shape=jax.ShapeDtypeStruct((n, PROJ_W), F32),
        scratch_shapes=[pltpu.VMEM((tm, d), BF16)],
        compiler_params=_params("arbitrary", "arbitrary"),
        name="proj",
    )(x, w_perm, cos_t, sin_t)


def _matmul_kernel(x_ref, w_ref, o_ref):
    o_ref[...] = _dot(x_ref[...].astype(BF16), w_ref[...])


def matmul(x, w, tm, tn):
    n, d = x.shape
    m = w.shape[1]
    return pl.pallas_call(
        _matmul_kernel,
        grid=(n // tm, m // tn),
        in_specs=[pl.BlockSpec((tm, d), lambda i, j: (i, 0)),
                  pl.BlockSpec((d, tn), lambda i, j: (0, j))],
        out_specs=pl.BlockSpec((tm, tn), lambda i, j: (i, j)),
        out_shape=jax.ShapeDtypeStruct((n, m), F32),
        compiler_params=_params("arbitrary", "arbitrary"),
        name="matmul",
    )(x, w)


def _merge_kernel(x_ref, ya_ref, yb_ref, yc_ref, yd_ref, wg0, wg1, wg2, wg3, bg0, bg1, bg2, bg3,
                  wb0, wb1, wb2, wb3, o_ref, xb_ref, yb16_ref):
    j = pl.program_id(1)

    @pl.when(j == 0)
    def _():
        xb_ref[...] = x_ref[...].astype(BF16)
        for b, y in enumerate((ya_ref, yb_ref, yc_ref, yd_ref)):
            yb16_ref[b] = y[...].astype(BF16)

    xb = xb_ref[...]
    acc = None
    for b, (wg, bg, wb) in enumerate(((wg0, bg0, wb0), (wg1, bg1, wb1), (wg2, bg2, wb2), (wg3, bg3, wb3))):
        gate = jax.nn.sigmoid(_dot(xb, wg[...]) + bg[...])
        term = gate * _dot(yb16_ref[b], wb[0])
        acc = term if acc is None else acc + term
    o_ref[...] = acc.astype(BF16)


def gated_merge(x, ys, w_gate, b_gate, w_br, tm, tn=256):
    n, d = x.shape
    nj = d // tn
    in_specs = [pl.BlockSpec((tm, d), lambda i, j: (i, 0))]
    in_specs += [pl.BlockSpec((tm, MIX_W), lambda i, j: (i, 0)) for _ in range(N_BRANCH)]
    in_specs += [pl.BlockSpec((d, tn), functools.partial(lambda i, j, b: (0, b * nj + j), b=b)) for b in range(N_BRANCH)]
    in_specs += [pl.BlockSpec((1, tn), functools.partial(lambda i, j, b: (0, b * nj + j), b=b)) for b in range(N_BRANCH)]
    in_specs += [pl.BlockSpec((1, MIX_W, tn), functools.partial(lambda i, j, b: (b, 0, j), b=b)) for b in range(N_BRANCH)]
    return pl.pallas_call(
        _merge_kernel,
        grid=(n // tm, nj),
        in_specs=in_specs,
        out_specs=pl.BlockSpec((tm, tn), lambda i, j: (i, j)),
        out_shape=jax.ShapeDtypeStruct((n, d), BF16),
        scratch_shapes=[pltpu.VMEM((tm, d), BF16), pltpu.VMEM((N_BRANCH, tm, MIX_W), BF16)],
        compiler_params=_params("arbitrary", "arbitrary"),
        name="gated_merge",
    )(x, *ys, *([w_gate] * N_BRANCH), *([b_gate] * N_BRANCH), *([w_br] * N_BRANCH))


def _matmul_res_ln_kernel(a_ref, w_ref, r_ref, g_ref, b_ref, o_ref, *, alpha):
    y = _dot(a_ref[...].astype(BF16), w_ref[...])
    o_ref[...] = _layer_norm(alpha * r_ref[...] + y, g_ref[...], b_ref[...])


def matmul_res_ln(a, w, res, g, b, alpha, tm):
    n, k = a.shape
    d = w.shape[1]
    return pl.pallas_call(
        functools.partial(_matmul_res_ln_kernel, alpha=alpha),
        grid=(n // tm,),
        in_specs=[pl.BlockSpec((tm, k), lambda i: (i, 0)),
                  pl.BlockSpec((k, d), lambda i: (0, 0)),
                  pl.BlockSpec((tm, d), lambda i: (i, 0)),
                  pl.BlockSpec((1, d), lambda i: (0, 0)),
                  pl.BlockSpec((1, d), lambda i: (0, 0))],
        out_specs=pl.BlockSpec((tm, d), lambda i: (i, 0)),
        out_shape=jax.ShapeDtypeStruct((n, d), F32),
        compiler_params=_params("arbitrary"),
        name="matmul_res_ln",
    )(a, w, res, g.reshape(1, d), b.reshape(1, d))


def _mem_kernel(x_ref, wq_ref, k_ref, v_ref, wo_ref, g_ref, b_ref, o_ref, *, alpha):
    x = x_ref[0]
    q = _dot(x.astype(BF16), wq_ref[...])
    outs = []
    for h in range(MEM_HEADS):
        sl = slice(h * MEM_HEAD_DIM, (h + 1) * MEM_HEAD_DIM)
        z = _dot_nt(q[:, sl].astype(BF16), k_ref[0, 0, :, sl].astype(BF16)) * (MEM_HEAD_DIM ** -0.5)
        m = jnp.max(z, axis=-1, keepdims=True)
        e = jnp.exp(z - m)
        p = e / jnp.sum(e, axis=-1, keepdims=True)
        outs.append(_dot(p.astype(BF16), v_ref[0, 0, :, sl].astype(BF16)))
    o = jnp.concatenate(outs, axis=1)
    y = _dot(o.astype(BF16), wo_ref[...])
    o_ref[0] = _layer_norm(alpha * x + y, g_ref[...], b_ref[...])


def memory_block(x, mem_kv, layer, w_mq, w_mo, g, b, alpha, tm):
    bsz, t, d = x.shape
    return pl.pallas_call(
        functools.partial(_mem_kernel, alpha=alpha),
        grid=(bsz, t // tm),
        in_specs=[pl.BlockSpec((1, tm, d), lambda bi, i: (bi, i, 0)),
                  pl.BlockSpec((d, MEM_W), lambda bi, i: (0, 0)),
                  pl.BlockSpec((1, 1, N_MEM, MEM_W), lambda bi, i: (layer, bi, 0, 0)),
                  pl.BlockSpec((1, 1, N_MEM, MEM_W), lambda bi, i: (layer, bi, 0, 1)),
                  pl.BlockSpec((MEM_W, d), lambda bi, i: (0, 0)),
                  pl.BlockSpec((1, d), lambda bi, i: (0, 0)),
                  pl.BlockSpec((1, d), lambda bi, i: (0, 0))],
        out_specs=pl.BlockSpec((1, tm, d), lambda bi, i: (bi, i, 0)),
        out_shape=jax.ShapeDtypeStruct((bsz, t, d), F32),
        compiler_params=_params("arbitrary", "arbitrary"),
        name="memory_block",
    )(x, w_mq, mem_kv, mem_kv, w_mo, g.reshape(1, d), b.reshape(1, d))


def _router_kernel(x_ref, wr_ref, br_ref, o_ref):
    tm = x_ref.shape[0]
    lane = _iota((tm, LANES), 1)
    logits = _dot3(x_ref[...], wr_ref[...]) + br_ref[...]
    is_grp = lane < N_GROUPS
    gl = jnp.where(is_grp, logits, NEG)
    gmax = jnp.max(gl, axis=-1, keepdims=True)
    grp = jnp.min(jnp.where(is_grp & (gl == gmax), lane, LANES), axis=-1, keepdims=True)
    p_grp = 1.0 / jnp.sum(jnp.where(is_grp, jnp.exp(gl - gmax), 0.0), axis=-1, keepdims=True)
    in_grp = (lane >= N_GROUPS) & (lane < N_GROUPS + N_EXPERTS) & (
        ((lane - N_GROUPS) >> 2) == grp)
    el = jnp.where(in_grp, logits, NEG)
    v1 = jnp.max(el, axis=-1, keepdims=True)
    i1 = jnp.min(jnp.where(in_grp & (el == v1), lane, LANES), axis=-1, keepdims=True)
    rest = in_grp & (lane != i1)
    el2 = jnp.where(rest, logits, NEG)
    v2 = jnp.max(el2, axis=-1, keepdims=True)
    i2 = jnp.min(jnp.where(rest & (el2 == v2), lane, LANES), axis=-1, keepdims=True)
    e2 = jnp.exp(v2 - v1)
    w1 = p_grp / (1.0 + e2)
    w2 = p_grp * e2 / (1.0 + e2)
    gate = jnp.where(lane == i1, w1, 0.0) + jnp.where(lane == i2, w2, 0.0)
    o_ref[...] = jnp.where(lane == 0, grp.astype(F32), gate)


def moe_route(x, w_router, b_router, tm):
    n, d = x.shape
    return pl.pallas_call(
        _router_kernel,
        grid=(n // tm,),
        in_specs=[pl.BlockSpec((tm, d), lambda i: (i, 0)),
                  pl.BlockSpec((d, LANES), lambda i: (0, 0)),
                  pl.BlockSpec((1, LANES), lambda i: (0, 0))],
        out_specs=pl.BlockSpec((tm, LANES), lambda i: (i, 0)),
        out_shape=jax.ShapeDtypeStruct((n, LANES), F32),
        compiler_params=_params("arbitrary"),
        name="moe_route",
    )(x, w_router, b_router)


def _moe_kernel(fetch_ref, x_ref, gate_ref, weg_ref, weu_ref, wed_ref, g_ref, b_ref, o_ref, xb_ref, acc_ref,
                *, alpha):
    e = pl.program_id(1)
    tm = x_ref.shape[0]
    lane = _iota((tm, LANES), 1)

    @pl.when(e == 0)
    def _():
        xb_ref[...] = x_ref[...].astype(BF16)
        acc_ref[...] = jnp.zeros_like(acc_ref)

    gate_e = jnp.sum(jnp.where(lane == e + N_GROUPS, gate_ref[...], 0.0), axis=-1, keepdims=True)

    @pl.when(jnp.max(gate_e) > 0.0)
    def _():
        xb = xb_ref[...]
        hid = jax.nn.silu(_dot(xb, weg_ref[0])) * _dot(xb, weu_ref[0])
        acc_ref[...] += _dot((hid * gate_e).astype(BF16), wed_ref[0])

    @pl.when(e == N_EXPERTS - 1)
    def _():
        o_ref[...] = _layer_norm(alpha * x_ref[...] + acc_ref[...], g_ref[...], b_ref[...])


def moe_block(x, w_router, b_router, w_eg, w_eu, w_ed, g, b, alpha, tm):
    n, d = x.shape
    f = w_eg.shape[-1]
    n_tiles = n // tm
    routed = moe_route(x, w_router, b_router, tm)
    order = jnp.argsort(routed[:, 0].astype(I32), stable=True)
    xs = jnp.take(x, order, axis=0)
    gs = jnp.take(routed, order, axis=0)
    used = jnp.any(gs[:, N_GROUPS:N_GROUPS + N_EXPERTS].reshape(n_tiles, tm, N_EXPERTS) > 0.0, axis=1).reshape(-1)
    steps = jnp.arange(n_tiles * N_EXPERTS, dtype=I32)
    fetch = (lax.cummax(jnp.where(used, steps, 0), axis=0) % N_EXPERTS).astype(I32)
    weight = lambda shape: pl.BlockSpec(shape, lambda i, e, fetch_ref: (fetch_ref[i * N_EXPERTS + e], 0, 0))
    ys = pl.pallas_call(
        functools.partial(_moe_kernel, alpha=alpha),
        grid_spec=pltpu.PrefetchScalarGridSpec(
            num_scalar_prefetch=1,
            grid=(n_tiles, N_EXPERTS),
            in_specs=[pl.BlockSpec((tm, d), lambda i, e, *_: (i, 0)),
                      pl.BlockSpec((tm, LANES), lambda i, e, *_: (i, 0)),
                      weight((1, d, f)), weight((1, d, f)), weight((1, f, d)),
                      pl.BlockSpec((1, d), lambda i, e, *_: (0, 0)),
                      pl.BlockSpec((1, d), lambda i, e, *_: (0, 0))],
            out_specs=pl.BlockSpec((tm, d), lambda i, e, *_: (i, 0)),
            scratch_shapes=[pltpu.VMEM((tm, d), BF16), pltpu.VMEM((tm, d), F32)]),
        out_shape=jax.ShapeDtypeStruct((n, d), F32),
        compiler_params=_params("arbitrary", "arbitrary"),
        name="moe_block",
    )(fetch, xs, gs, w_eg, w_eu, w_ed, g.reshape(1, d), b.reshape(1, d))
    inverse = jnp.zeros((n,), I32).at[order].set(jnp.arange(n, dtype=I32))
    return jnp.take(ys, inverse, axis=0)


TQ = 128
BK = 128


def _log_sigmoid_pair(z):
    l1p = jnp.log1p(jnp.exp(-jnp.abs(z)))
    return jnp.minimum(z, 0.0) - l1p, jnp.minimum(-z, 0.0) - l1p


def _strict_upper_ones(n):
    return jnp.where(_iota((n, n), 0) > _iota((n, n), 1), 1.0, 0.0).astype(BF16)


def _strict_lower_ones(n):
    return jnp.where(_iota((n, n), 0) < _iota((n, n), 1), 1.0, 0.0).astype(BF16)


def _sb_kernel(q_ref, k_ref, v_ref, o_ref):
    qi = pl.program_id(1)
    r_rows = N_Q_HEADS * TQ
    qs = _stack_q(q_ref[0], HEAD_DIM ** -0.5).astype(BF16)
    upper = _strict_upper_ones(BK)
    strictly_causal = _iota((TQ, BK), 1) < _iota((TQ, BK), 0)

    def step(kb, acc, run, mask):
        ks = pl.multiple_of(kb * BK, BK)
        k = k_ref[0, pl.ds(ks, BK), :].astype(BF16)
        v = v_ref[0, pl.ds(ks, BK), :].astype(BF16)
        z = _dot_nt(qs, k)
        lss, lns = [], []
        for h in range(N_Q_HEADS):
            ls, ln = _log_sigmoid_pair(z[h * TQ:(h + 1) * TQ])
            lss.append(ls)
            lns.append(ln if mask is None else jnp.where(mask, ln, 0.0))
        ln = jnp.concatenate(lns, axis=0)
        between = _dot_x_exact(ln, upper)
        weights = []
        for h in range(N_Q_HEADS):
            sl = slice(h * TQ, (h + 1) * TQ)
            a = jnp.exp(lss[h] + between[sl] + run[sl])
            weights.append((a if mask is None else jnp.where(mask, a, 0.0)).astype(BF16))
        acc = acc + _dot(jnp.concatenate(weights, axis=0), v)
        return acc, run + between[:, 0:1] + ln[:, 0:1]

    acc, run = step(qi, jnp.zeros((r_rows, LANES), F32), jnp.zeros((r_rows, 1), F32), strictly_causal)

    def more(state):
        kb, _, run = state
        return jnp.logical_and(kb >= 0, jnp.max(run) > F32_EXP_ZERO)

    def body(state):
        kb, acc, run = state
        acc, run = step(kb, acc, run, None)
        return kb - 1, acc, run

    _, acc, _ = lax.while_loop(more, body, (qi - 1, acc, run))
    _store_heads(o_ref, acc, TQ)


def _h_spec(rows, tile, width=1):
    if rows is None:
        return lambda t: pl.BlockSpec((1, t, width * LANES), lambda b, i: (b, 0, tile // width))
    return pl.BlockSpec((1, rows, width * LANES), lambda b, i: (b, i, tile // width))


def sb_attention(h):
    bsz, t, _ = h.shape
    return pl.pallas_call(
        _sb_kernel,
        grid=(bsz, t // TQ),
        in_specs=[_h_spec(TQ, T_SBQ, 4), _h_spec(None, T_SB_K)(t), _h_spec(None, T_SB_V)(t)],
        out_specs=pl.BlockSpec((1, TQ, MIX_W), lambda b, i: (b, i, 0)),
        out_shape=jax.ShapeDtypeStruct((bsz, t, MIX_W), F32),
        compiler_params=_params("arbitrary", "arbitrary"),
        name="sb_attention",
    )(h, h, h)


def _fox_prep_kernel(m_ref, bf_ref, logf_ref, c_ref, carry_ref):
    i = pl.program_id(1)
    tc = m_ref.shape[1]

    @pl.when(i == 0)
    def _():
        carry_ref[...] = jnp.zeros_like(carry_ref)

    logf, _ = _log_sigmoid_pair(m_ref[0] + bf_ref[...])
    logf_ref[0] = logf
    incl = jnp.where(_iota((tc, tc), 0) >= _iota((tc, tc), 1), 1.0, 0.0).astype(BF16)
    c = _dot_exact_x(incl, logf) + carry_ref[...]
    c_ref[0] = c
    carry_ref[...] = c[tc - 1:tc, :]


def fox_prep(h, fox_b_f, tc=256):
    bsz, t, _ = h.shape
    bias = jnp.zeros((1, LANES), F32).at[0, MISC_FOXF:MISC_FOXF + N_Q_HEADS].set(fox_b_f)
    return pl.pallas_call(
        _fox_prep_kernel,
        grid=(bsz, t // tc),
        in_specs=[pl.BlockSpec((1, tc, LANES), lambda b, i: (b, i, T_MISC)),
                  pl.BlockSpec((1, LANES), lambda b, i: (0, 0))],
        out_specs=[pl.BlockSpec((1, tc, LANES), lambda b, i: (b, i, 0)),
                   pl.BlockSpec((1, tc, LANES), lambda b, i: (b, i, 0))],
        out_shape=[jax.ShapeDtypeStruct((bsz, t, LANES), F32),
                   jax.ShapeDtypeStruct((bsz, t, LANES), F32)],
        scratch_shapes=[pltpu.VMEM((1, LANES), F32)],
        compiler_params=_params("arbitrary", "arbitrary"),
        name="fox_prep",
    )(h, bias)


def _head_columns(x, lane0, stride=1):
    return jnp.concatenate([x[:, lane0 + stride * h:lane0 + stride * h + 1] for h in range(N_Q_HEADS)], axis=0)


def _softmax_init(rows):
    return (jnp.full((rows, 1), NEG, F32), jnp.zeros((rows, 1), F32), jnp.zeros((rows, LANES), F32))


def _transpose_heads(x):
    if x.shape[0] == LANES:
        return jnp.concatenate([x[:, h * TQ:(h + 1) * TQ].T for h in range(N_Q_HEADS)], axis=0)
    return jnp.concatenate([x[h * TQ:(h + 1) * TQ].T for h in range(N_Q_HEADS)], axis=1)


def _softmax_init_t():
    cols = N_Q_HEADS * TQ
    return (jnp.full((1, cols), NEG, F32), jnp.zeros((1, cols), F32), jnp.zeros((LANES, cols), F32))


def _softmax_step_t(zt, masks, v, m, l, acc, biases=None):
    ps, ms, ls, alphas = [], [], [], []
    for h in range(N_Q_HEADS):
        sl = slice(h * TQ, (h + 1) * TQ)
        z = zt[:, sl]
        if biases is not None:
            z = z + biases[h]
        mask = None if masks is None else masks[h]
        if mask is not None:
            z = jnp.where(mask, z, NEG)
        m_new = jnp.maximum(m[:, sl], jnp.max(z, axis=0, keepdims=True))
        p = jnp.exp(z - m_new)
        if mask is not None:
            p = jnp.where(mask, p, 0.0)
        alpha = jnp.exp(m[:, sl] - m_new)
        ls.append(alpha * l[:, sl] + jnp.sum(p, axis=0, keepdims=True))
        ms.append(m_new)
        alphas.append(alpha)
        ps.append(p.astype(BF16))
    cat = lambda xs: jnp.concatenate(xs, axis=1)
    pv = _dot(v.T.astype(BF16), cat(ps))
    return cat(ms), cat(ls), cat(alphas) * acc + pv


def _fox_kernel(q_ref, k_ref, v_ref, c_ref, o_ref):
    qi = pl.program_id(1)
    qst = _transpose_heads(_stack_q(q_ref[0], HEAD_DIM ** -0.5)).astype(BF16)

    def step(kb, carry, masks):
        ks = pl.multiple_of(kb * BK, BK)
        k = k_ref[0, pl.ds(ks, BK), :].astype(BF16)
        ck = c_ref[0, pl.ds(ks, BK), :]
        biases = [-ck[:, h:h + 1] for h in range(N_Q_HEADS)]
        return _softmax_step_t(_dot(k, qst), masks, v_ref[0, pl.ds(ks, BK), :], *carry, biases=biases)

    carry = lax.fori_loop(0, qi, lambda kb, c: step(kb, c, None), _softmax_init_t())
    causal = _iota((BK, TQ), 0) <= _iota((BK, TQ), 1)
    _, l, acc = step(qi, carry, [causal] * N_Q_HEADS)
    _store_heads(o_ref, _transpose_heads(acc / jnp.maximum(l, 1e-30)), TQ)


def fox_attention(h, c_tok):
    bsz, t, _ = h.shape
    return pl.pallas_call(
        _fox_kernel,
        grid=(bsz, t // TQ),
        in_specs=[_h_spec(TQ, T_FOXQ, 4), _h_spec(None, T_FOX_K)(t), _h_spec(None, T_FOX_V)(t),
                  pl.BlockSpec((1, t, LANES), lambda b, i: (b, 0, 0))],
        out_specs=pl.BlockSpec((1, TQ, MIX_W), lambda b, i: (b, i, 0)),
        out_shape=jax.ShapeDtypeStruct((bsz, t, MIX_W), F32),
        compiler_params=_params("arbitrary", "arbitrary"),
        name="fox_attention",
    )(h, h, h, c_tok)


def _nsa_cmp_kernel(k_ref, v_ref, pek_ref, pev_ref, wk_ref, wv_ref, ok_ref, ov_ref, *, nblk):
    for src, pe, w, dst in ((k_ref, pek_ref, wk_ref, ok_ref), (v_ref, pev_ref, wv_ref, ov_ref)):
        rows = src[0].reshape(nblk, NSA_BLOCK, LANES) + pe[...][None]
        mean = jnp.sum(rows, axis=1) * (1.0 / NSA_BLOCK)
        dst[0] = jnp.zeros(dst.shape[1:], F32)
        dst[0, 0:nblk, :] = _dot3(mean, w[...])


def nsa_compress(h, pe_k2, pe_v2, w_ck2, w_cv2):
    bsz, t, _ = h.shape
    nblk = t // NSA_BLOCK
    nblk_pad = -(-nblk // LANES) * LANES
    full = lambda shape: pl.BlockSpec(shape, lambda b: (0,) * len(shape))
    return pl.pallas_call(
        functools.partial(_nsa_cmp_kernel, nblk=nblk),
        grid=(bsz,),
        in_specs=[pl.BlockSpec((1, t, LANES), lambda b: (b, 0, T_NSA_KC)),
                  pl.BlockSpec((1, t, LANES), lambda b: (b, 0, T_NSA_VC)),
                  full((NSA_BLOCK, LANES)), full((NSA_BLOCK, LANES)),
                  full((LANES, LANES)), full((LANES, LANES))],
        out_specs=[pl.BlockSpec((1, nblk_pad, LANES), lambda b: (b, 0, 0))] * 2,
        out_shape=[jax.ShapeDtypeStruct((bsz, nblk_pad, LANES), F32)] * 2,
        compiler_params=_params("arbitrary"),
        name="nsa_compress",
    )(h, h, pe_k2, pe_v2, w_ck2, w_cv2)


def _topk_block_mask(imp_t, nblk, k_sel):
    n_idx = _iota(imp_t.shape, 0)
    rank = jnp.zeros(imp_t.shape, F32)
    for m in range(nblk):
        row = imp_t[m:m + 1, :]
        before = jnp.where(n_idx > m, 1.0, 0.0)
        rank = rank + jnp.where(row > imp_t, 1.0, jnp.where(row == imp_t, before, 0.0))
    return jnp.where(rank < k_sel, jnp.where(imp_t > -jnp.inf, 1.0, 0.0), 0.0)


def _nsa_kernel(q_ref, ck_ref, cv_ref, ks_ref, vs_ref, kw_ref, vw_ref, g_ref, o_ref, *, nblk, k_sel):
    qi = pl.program_id(1)
    r_rows = N_Q_HEADS * TQ
    half = GQA_GROUP * TQ
    qf = _stack_q(q_ref[0], HEAD_DIM ** -0.5)
    qpos = qi * TQ + (_iota((r_rows, BK), 0) & (TQ - 1))
    col = _iota((r_rows, BK), 1)

    n_ctile = ck_ref.shape[1] // LANES
    sel_tiles = []
    state = _softmax_init(r_rows)
    zc_tiles, vis_tiles = [], []
    for c in range(n_ctile):
        blk = c * LANES + col
        vis = ((blk + 1) * NSA_BLOCK - 1 <= qpos) & (blk < nblk)
        zc = _dot3_nt(qf, ck_ref[0, c * LANES:(c + 1) * LANES, :])
        zc_tiles.append(zc)
        vis_tiles.append(vis)
        m_run = jnp.maximum(state[0], jnp.max(jnp.where(vis, zc, NEG), axis=-1, keepdims=True))
        state = (m_run,) + state[1:]
    m_c = state[0]
    e_tiles = [jnp.where(vis, jnp.exp(zc - m_c), 0.0) for zc, vis in zip(zc_tiles, vis_tiles)]
    denom = e_tiles[0].sum(axis=-1, keepdims=True)
    for e in e_tiles[1:]:
        denom = denom + e.sum(axis=-1, keepdims=True)
    denom = jnp.maximum(denom, 1e-30)
    o_cmp = jnp.zeros((r_rows, LANES), F32)
    tq_pos = qi * TQ + _iota((TQ, LANES), 0)
    tq_col = _iota((TQ, LANES), 1)
    for c in range(n_ctile):
        pc = e_tiles[c] / denom
        o_cmp = o_cmp + _dot(pc.astype(BF16), cv_ref[0, c * LANES:(c + 1) * LANES, :].astype(BF16))
        blk = c * LANES + tq_col
        vis_t = ((blk + 1) * NSA_BLOCK - 1 <= tq_pos) & (blk < nblk)
        forced = (blk == (tq_pos >> 6)) | (blk == 0)
        per_group = []
        for g in range(N_KV_HEADS):
            imp = pc[g * half:g * half + TQ]
            for r in range(1, GQA_GROUP):
                imp = imp + pc[g * half + r * TQ:g * half + (r + 1) * TQ]
            imp = jnp.where(forced, jnp.inf, jnp.where(vis_t, imp, -jnp.inf))
            per_group.append(imp)
        sel_tiles.append(per_group)

    sel = []
    for g in range(N_KV_HEADS):
        imp_t = jnp.concatenate([sel_tiles[c][g].T for c in range(n_ctile)], axis=0)[0:nblk]
        chosen = _topk_block_mask(imp_t, nblk, k_sel)
        pad = n_ctile * LANES - nblk
        if pad:
            chosen = jnp.concatenate([chosen, jnp.zeros((pad, TQ), F32)], axis=0)
        sel.append([chosen[c * LANES:(c + 1) * LANES].astype(BF16) for c in range(n_ctile)])

    qst = _transpose_heads(qf).astype(BF16)
    blocks_per_tile = BK // NSA_BLOCK
    e_key = _iota((BK, LANES), 0) >> 6
    e_blk = _iota((BK, LANES), 1)
    kq_key = _iota((BK, TQ), 0)
    kq_pos = qi * TQ + _iota((BK, TQ), 1)

    def sel_body(kb, carry):
        ks = pl.multiple_of(kb * BK, BK)
        k = ks_ref[0, pl.ds(ks, BK), :].astype(BF16)
        causal = (ks + kq_key) <= kq_pos
        masks = []
        for g in range(N_KV_HEADS):
            hit = None
            for c in range(n_ctile):
                expand = jnp.where(e_blk + c * LANES == kb * blocks_per_tile + e_key, 1.0, 0.0).astype(BF16)
                part = _dot(expand, sel[g][c])
                hit = part if hit is None else hit + part
            masks += [jnp.where(causal, hit, 0.0) > 0.5] * GQA_GROUP
        return _softmax_step_t(_dot(k, qst), masks, vs_ref[0, pl.ds(ks, BK), :], *carry)

    _, l_s, acc_s = lax.fori_loop(0, qi + 1, sel_body, _softmax_init_t())
    o_sel = _transpose_heads(acc_s / jnp.maximum(l_s, 1e-30))

    def win_body(kb, carry):
        ks = pl.multiple_of(kb * BK, BK)
        k = kw_ref[0, pl.ds(ks, BK), :].astype(BF16)
        kpos = ks + kq_key
        band = jnp.where(kpos <= kq_pos, jnp.where(kpos >= kq_pos - NSA_WINDOW, 1.0, 0.0), 0.0) > 0.5
        return _softmax_step_t(_dot(k, qst), [band] * N_Q_HEADS, vw_ref[0, pl.ds(ks, BK), :], *carry)

    first = jnp.maximum(qi - NSA_WINDOW // BK, 0)
    _, l_w, acc_w = lax.fori_loop(first, qi + 1, win_body, _softmax_init_t())
    o_win = _transpose_heads(acc_w / jnp.maximum(l_w, 1e-30))

    gates = jax.nn.sigmoid(g_ref[0])
    o = (_head_columns(gates, MISC_NSAG, 3) * o_cmp + _head_columns(gates, MISC_NSAG + 1, 3) * o_sel
         + _head_columns(gates, MISC_NSAG + 2, 3) * o_win)
    _store_heads(o_ref, o, TQ)


def nsa_attention(h, cmp_k, cmp_v):
    bsz, t, _ = h.shape
    nblk = t // NSA_BLOCK
    nblk_pad = cmp_k.shape[1]
    return pl.pallas_call(
        functools.partial(_nsa_kernel, nblk=nblk, k_sel=min(NSA_TOPK, nblk)),
        grid=(bsz, t // TQ),
        in_specs=[_h_spec(TQ, T_NSAQ, 4),
                  pl.BlockSpec((1, nblk_pad, LANES), lambda b, i: (b, 0, 0)),
                  pl.BlockSpec((1, nblk_pad, LANES), lambda b, i: (b, 0, 0)),
                  _h_spec(None, T_NSA_KS)(t), _h_spec(None, T_NSA_VS)(t),
                  _h_spec(None, T_NSA_KW)(t), _h_spec(None, T_NSA_VW)(t),
                  _h_spec(TQ, T_MISC)],
        out_specs=pl.BlockSpec((1, TQ, MIX_W), lambda b, i: (b, i, 0)),
        out_shape=jax.ShapeDtypeStruct((bsz, t, MIX_W), F32),
        compiler_params=_params("arbitrary", "arbitrary"),
        name="nsa_attention",
    )(h, cmp_k, cmp_v, h, h, h, h, h)


def _order_key(x):
    bits = lax.bitcast_convert_type(x + 0.0, I32)
    return bits ^ ((bits >> 31) & 0x7FFFFFFF)


INT_MIN = -2 ** 31
NEG_INF_KEY = -2139095041


def _kth_largest_key(count_ge, shape, k):
    def unsettled(state):
        it, _, settled = state
        return jnp.logical_and(it < 32, jnp.min(settled) < 0.5)

    def bit_body(state):
        it, kappa, settled = state
        bit = lax.shift_left(jnp.int32(1), 31 - it)
        cand = kappa | bit
        cnt = count_ge(cand ^ INT_MIN)
        take = jnp.where(settled < 0.5, jnp.where(cnt >= k, 1.0, 0.0), 0.0)
        kappa = jnp.where(take > 0.5, cand, kappa)
        settled = jnp.where(cnt == k, jnp.maximum(settled, take), settled)
        return it + 1, kappa, settled

    _, kappa, _ = lax.while_loop(unsettled, bit_body, (jnp.int32(0), jnp.zeros(shape, I32), jnp.zeros(shape, F32)))
    return kappa ^ INT_MIN


def _dsa_kernel(q_ref, iq_ref, ik_ref, w_ref, k_ref, v_ref, o_ref, key_ref, *, topk):
    qi = pl.program_id(1)
    qst = _transpose_heads(_stack_q(q_ref[0], HEAD_DIM ** -0.5)).astype(BF16)
    iq = iq_ref[0]
    low = _iota((TQ, LANES), 1) < HEAD_DIM
    iq_t = jnp.concatenate(
        [jnp.where(low if h % 2 == 0 else jnp.logical_not(low), iq[:, (h // 2) * LANES:(h // 2 + 1) * LANES], 0.0).T
         for h in range(IDX_HEADS)], axis=1)
    w_t = w_ref[0].T
    key_pos = _iota((BK, TQ), 0)
    q_pos = qi * TQ + _iota((BK, TQ), 1)
    n_tiles = qi + 1

    def score_body(kb, _):
        ks = pl.multiple_of(kb * BK, BK)
        sc = jnp.maximum(_dot3(ik_ref[0, pl.ds(ks, BK), :], iq_t), 0.0)
        tot = w_t[MISC_IDXW:MISC_IDXW + 1, :] * sc[:, 0:TQ]
        for h in range(1, IDX_HEADS):
            tot = tot + w_t[MISC_IDXW + h:MISC_IDXW + h + 1, :] * sc[:, h * TQ:(h + 1) * TQ]
        tot = jnp.where(ks + key_pos <= q_pos, tot, -jnp.inf)
        key_ref[kb] = _order_key(tot)
        return 0

    lax.fori_loop(0, n_tiles, score_body, 0)

    def count_where(pred):
        def body(kb, acc):
            return acc + jnp.where(pred(key_ref[kb]), 1.0, 0.0)
        return jnp.sum(lax.fori_loop(0, n_tiles, body, jnp.zeros((BK, TQ), F32)), axis=0, keepdims=True)

    kappa = _kth_largest_key(lambda c: count_where(lambda key: key >= c), (1, TQ), float(topk))
    need = float(topk) - count_where(lambda key: key > kappa)
    earlier = jnp.where(_iota((BK, BK), 1) < _iota((BK, BK), 0), 1.0, 0.0).astype(BF16)

    def attn_body(kb, carry):
        ties_before, m, l, acc = carry
        ks = pl.multiple_of(kb * BK, BK)
        key = key_ref[kb]
        tie = jnp.where(key == kappa, 1.0, 0.0)
        rank = _dot(earlier, tie.astype(BF16)) + ties_before
        chosen = jnp.where(key > kappa, 1.0, jnp.where(rank < need, tie, 0.0))
        chosen = jnp.where(key > NEG_INF_KEY, chosen, 0.0)
        ties_before = ties_before + jnp.sum(tie, axis=0, keepdims=True)
        k = k_ref[0, pl.ds(ks, BK), :].astype(BF16)
        return (ties_before,) + _softmax_step_t(_dot(k, qst), [chosen > 0.5] * N_Q_HEADS,
                                                v_ref[0, pl.ds(ks, BK), :], m, l, acc)

    _, _, l, acc = lax.fori_loop(0, n_tiles, attn_body, (jnp.zeros((1, TQ), F32),) + _softmax_init_t())
    _store_heads(o_ref, _transpose_heads(acc / jnp.maximum(l, 1e-30)), TQ)


def dsa_attention(h):
    bsz, t, _ = h.shape
    return pl.pallas_call(
        functools.partial(_dsa_kernel, topk=min(DSA_TOPK_MAX, t // 4)),
        grid=(bsz, t // TQ),
        in_specs=[_h_spec(TQ, T_DSAQ, 4), _h_spec(TQ, T_IDXQ, 2), _h_spec(None, T_IDXK)(t),
                  _h_spec(TQ, T_MISC), _h_spec(None, T_DSA_K)(t), _h_spec(None, T_DSA_V)(t)],
        out_specs=pl.BlockSpec((1, TQ, MIX_W), lambda b, i: (b, i, 0)),
        out_shape=jax.ShapeDtypeStruct((bsz, t, MIX_W), F32),
        scratch_shapes=[pltpu.VMEM((t // BK, TQ, BK), I32)],
        compiler_params=_params("arbitrary", "arbitrary"),
        name="dsa_attention",
    )(h, h, h, h, h, h)


PAGES_PER_STEP = 16


def _page_specs(rows, row_block, layer, n_chunks, reverse, page):
    def spec(p):
        def index(b, c, pt_ref, *_):
            chunk = (n_chunks - 1 - c) if reverse else c
            return (layer, pt_ref[b, chunk * PAGES_PER_STEP + p], row_block, 0)
        return index
    return [pl.BlockSpec((1, 1, rows, page), spec(p)) for p in range(PAGES_PER_STEP)]


def _suffix_sums_pages(xs):
    return _dot_x_exact(jnp.concatenate(xs, axis=0), _strict_upper_ones(LANES))


def _softmax_pages(zs, masks, vts, m, l, acc):
    if masks is not None:
        zs = [jnp.where(mk, z, NEG) for z, mk in zip(zs, masks)]
    top = zs[0]
    for z in zs[1:]:
        top = jnp.maximum(top, z)
    m_new = jnp.maximum(m, jnp.max(top, axis=-1, keepdims=True))
    alpha = jnp.exp(m - m_new)
    ps = [jnp.exp(z - m_new) for z in zs]
    if masks is not None:
        ps = [jnp.where(mk, p, 0.0) for p, mk in zip(ps, masks)]
    total = ps[0]
    for p in ps[1:]:
        total = total + p
    pv = _dot_nt(ps[0].astype(BF16), vts[0])
    for p, vt in zip(ps[1:], vts[1:]):
        pv = pv + _dot_nt(p.astype(BF16), vt)
    return m_new, alpha * l + jnp.sum(total, axis=-1, keepdims=True), alpha * acc + pv


def _row_of(ref, b):
    return ref[pl.ds(b, 1), :]


def _diag_column(row, lane0, stride):
    wide = jnp.broadcast_to(row, (N_Q_HEADS, LANES))
    pick = _iota((N_Q_HEADS, LANES), 1) == lane0 + stride * _iota((N_Q_HEADS, LANES), 0)
    return jnp.sum(jnp.where(pick, wide, 0.0), axis=-1, keepdims=True)


def _softmax_token(z, v_row, m, l, acc):
    m_new = jnp.maximum(m, z)
    alpha = jnp.exp(m - m_new)
    p = jnp.exp(z - m_new)
    return m_new, alpha * l + p, alpha * acc + p * v_row


def _sb_dec_kernel(pt_ref, q_ref, *refs):
    pages = refs[:PAGES_PER_STEP]
    o_ref, acc_ref, run_ref = refs[PAGES_PER_STEP:]
    c = pl.program_id(1)

    @pl.when(c == 0)
    def _():
        acc_ref[...] = jnp.zeros_like(acc_ref)
        run_ref[...] = jnp.zeros_like(run_ref)

    qs = (q_ref[0] * HEAD_DIM ** -0.5).astype(BF16)
    lss, lns = [], []
    for p in range(PAGES_PER_STEP):
        ls, ln = _log_sigmoid_pair(_dot(qs, pages[p][0, 0, 0:KV_W, :].astype(BF16)))
        lss.append(ls)
        lns.append(ln)
    within = _suffix_sums_pages(lns)
    later = [within[p * N_Q_HEADS:(p + 1) * N_Q_HEADS] for p in range(PAGES_PER_STEP)]
    totals = [later[p][:, 0:1] + lns[p][:, 0:1] for p in range(PAGES_PER_STEP)]
    acc = acc_ref[...]
    run = run_ref[:, 0:1]
    for p in reversed(range(PAGES_PER_STEP)):
        a = jnp.exp(lss[p] + later[p] + run)
        acc = acc + _dot_nt(a.astype(BF16), pages[p][0, 0, KV_W:2 * KV_W, :].astype(BF16))
        run = run + totals[p]
    acc_ref[...] = acc
    run_ref[...] = jnp.broadcast_to(run, run_ref.shape)

    @pl.when(c == pl.num_programs(1) - 1)
    def _():
        o_ref[0] = acc


def _decode_call(kernel, name, page_table, n_chunks, in_specs, args, out_specs, out_shape, scratch, n_prefetch=1,
                 grid=None, prefetch=None):
    bsz = page_table.shape[0]
    return pl.pallas_call(
        kernel,
        grid_spec=pltpu.PrefetchScalarGridSpec(
            num_scalar_prefetch=n_prefetch,
            grid=grid or (bsz, n_chunks),
            in_specs=in_specs,
            out_specs=out_specs,
            scratch_shapes=scratch),
        out_shape=out_shape,
        compiler_params=_params("arbitrary", "arbitrary"),
        name=name,
    )(*(prefetch or (page_table,)), *args)


def _q_spec():
    return pl.BlockSpec((1, N_Q_HEADS, LANES), lambda b, c, *_: (b, 0, 0))


def _hs_spec(bsz, tile):
    return pl.BlockSpec((bsz, LANES), lambda b, c, *_: (0, tile))


def _head_out(bsz):
    return (pl.BlockSpec((1, N_Q_HEADS, LANES), lambda b, c, *_: (b, 0, 0)),
            jax.ShapeDtypeStruct((bsz, N_Q_HEADS, LANES), F32))


def sb_decode(q, cache, layer, page_table):
    bsz, n_pages = page_table.shape
    n_chunks = n_pages // PAGES_PER_STEP
    page = cache.shape[3]
    out_spec, out_shape = _head_out(bsz)
    return _decode_call(
        _sb_dec_kernel, "sb_decode", page_table, n_chunks,
        [_q_spec()] + _page_specs(2 * KV_W, 0, layer, n_chunks, True, page),
        [q] + [cache] * PAGES_PER_STEP, out_spec, out_shape,
        [pltpu.VMEM((N_Q_HEADS, LANES), F32), pltpu.VMEM((N_Q_HEADS, LANES), F32)])


def _fox_dec_kernel(pt_ref, q_ref, kn_ref, vn_ref, misc_ref, bf_ref, *refs):
    pages = refs[:PAGES_PER_STEP]
    lf_pages = refs[PAGES_PER_STEP:2 * PAGES_PER_STEP]
    o_ref, lf_out_ref, m_ref, l_ref, acc_ref, run_ref = refs[2 * PAGES_PER_STEP:]
    b = pl.program_id(0)
    c = pl.program_id(1)
    qf = q_ref[0] * HEAD_DIM ** -0.5

    @pl.when(c == 0)
    def _():
        logf_row, _ = _log_sigmoid_pair(_row_of(misc_ref, b) + bf_ref[...])
        lf_out_ref[0] = logf_row
        z_new = jnp.sum(qf * _row_of(kn_ref, b), axis=-1, keepdims=True)
        m_ref[...] = jnp.broadcast_to(z_new, m_ref.shape)
        l_ref[...] = jnp.ones_like(l_ref)
        acc_ref[...] = jnp.broadcast_to(_row_of(vn_ref, b), acc_ref.shape)
        run_ref[...] = jnp.broadcast_to(_diag_column(logf_row, MISC_FOXF, 1), run_ref.shape)

    qs = qf.astype(BF16)
    run = run_ref[:, 0:1]
    zs = [None] * PAGES_PER_STEP
    lfs = [lf_pages[p][0, 0] for p in range(PAGES_PER_STEP)]
    within = _suffix_sums_pages(lfs)
    for p in reversed(range(PAGES_PER_STEP)):
        later = within[p * N_Q_HEADS:(p + 1) * N_Q_HEADS]
        zs[p] = _dot(qs, pages[p][0, 0, 0:KV_W, :].astype(BF16)) + (later + run)
        run = run + later[:, 0:1] + lfs[p][:, 0:1]
    vts = [pages[p][0, 0, KV_W:2 * KV_W, :].astype(BF16) for p in range(PAGES_PER_STEP)]
    state = _softmax_pages(zs, None, vts, m_ref[:, 0:1], l_ref[:, 0:1], acc_ref[...])
    m_ref[...] = jnp.broadcast_to(state[0], m_ref.shape)
    l_ref[...] = jnp.broadcast_to(state[1], l_ref.shape)
    acc_ref[...] = state[2]
    run_ref[...] = jnp.broadcast_to(run, run_ref.shape)

    @pl.when(c == pl.num_programs(1) - 1)
    def _():
        o_ref[0] = state[2] / state[1]


def fox_decode(q, hs, bias_row, cache, logf_t, layer, page_table):
    bsz, n_pages = page_table.shape
    n_chunks = n_pages // PAGES_PER_STEP
    page = cache.shape[3]
    out_spec, out_shape = _head_out(bsz)
    stat = pltpu.VMEM((N_Q_HEADS, LANES), F32)
    return _decode_call(
        _fox_dec_kernel, "fox_decode", page_table, n_chunks,
        [_q_spec(), _hs_spec(bsz, T_FOX_K), _hs_spec(bsz, T_FOX_V), _hs_spec(bsz, T_MISC),
         pl.BlockSpec((1, LANES), lambda b, c, *_: (0, 0))]
        + _page_specs(2 * KV_W, 0, layer, n_chunks, True, page)
        + _page_specs(N_Q_HEADS, 0, layer, n_chunks, True, page),
        [q, hs, hs, hs, bias_row] + [cache] * PAGES_PER_STEP + [logf_t] * PAGES_PER_STEP,
        [out_spec, pl.BlockSpec((1, 1, LANES), lambda b, c, *_: (b, 0, 0))],
        [out_shape, jax.ShapeDtypeStruct((bsz, 1, LANES), F32)],
        [stat, stat, stat, stat])


def _nsa_dec_cmp_kernel(pt_ref, q_ref, pek_ref, pev_ref, wk_ref, wv_ref, *refs, n_blocks_past, k_sel):
    pages = refs[:PAGES_PER_STEP]
    ocmp_ref, sel_ref, sk_ref, sv_ref = refs[PAGES_PER_STEP:]
    c = pl.program_id(1)
    page = pages[0].shape[3]
    blocks_per_page = page // NSA_BLOCK
    step_blocks = PAGES_PER_STEP * blocks_per_page
    steps_per_tile = LANES // step_blocks

    @pl.when(c == 0)
    def _():
        sk_ref[...] = jnp.zeros_like(sk_ref)
        sv_ref[...] = jnp.zeros_like(sv_ref)

    lane0 = (c % steps_per_tile) * step_blocks
    tok = _iota((page, LANES), 0)
    lane = _iota((page, LANES), 1)
    sum_k = jnp.zeros((KV_W, LANES), F32)
    sum_v = jnp.zeros((KV_W, LANES), F32)
    for p in range(PAGES_PER_STEP):
        place = jnp.where(lane == lane0 + p * blocks_per_page + (tok >> 6), 1.0, 0.0).astype(BF16)
        sum_k = sum_k + _dot_x_exact(pages[p][0, 0, 0:KV_W, :], place)
        sum_v = sum_v + _dot_x_exact(pages[p][0, 0, KV_W:2 * KV_W, :], place)
    sk_ref[c // steps_per_tile] += sum_k
    sv_ref[c // steps_per_tile] += sum_v

    @pl.when(c == pl.num_programs(1) - 1)
    def _():
        n_ptiles = sk_ref.shape[0]
        nb_pad = n_ptiles * LANES
        n_tiles = n_ptiles + 1
        qpos = n_blocks_past * NSA_BLOCK
        qf = q_ref[0] * HEAD_DIM ** -0.5
        pe_mean_k = jnp.mean(pek_ref[...], axis=-1, keepdims=True)
        pe_mean_v = jnp.mean(pev_ref[...], axis=-1, keepdims=True)
        cvs, zcs = [], []
        for t in range(n_ptiles):
            ck_t = _dot3(wk_ref[...], sk_ref[t] * (1.0 / NSA_BLOCK) + pe_mean_k)
            cvs.append(_dot3(wv_ref[...], sv_ref[t] * (1.0 / NSA_BLOCK) + pe_mean_v))
            zcs.append(_dot3(qf, ck_t))
        zc = zcs[0] if n_ptiles == 1 else jnp.concatenate(zcs, axis=1)
        blk = _iota(zc.shape, 1)
        vis = ((blk + 1) * NSA_BLOCK - 1 <= qpos) & (blk < n_blocks_past)
        m = jnp.max(jnp.where(vis, zc, NEG), axis=-1, keepdims=True)
        e = jnp.where(vis, jnp.exp(zc - m), 0.0)
        pc = e / jnp.maximum(jnp.sum(e, axis=-1, keepdims=True), 1e-30)
        o_cmp = _dot_nt(pc[:, 0:LANES].astype(BF16), cvs[0].astype(BF16))
        for t in range(1, n_ptiles):
            o_cmp = o_cmp + _dot_nt(pc[:, t * LANES:(t + 1) * LANES].astype(BF16), cvs[t].astype(BF16))
        ocmp_ref[0] = o_cmp
        n_blocks = n_blocks_past + 1
        rows = []
        for g in range(N_KV_HEADS):
            imp = jnp.sum(pc[g * GQA_GROUP:(g + 1) * GQA_GROUP], axis=0, keepdims=True)
            imp = jnp.where(vis[0:1], imp, -jnp.inf)
            rows.append(jnp.concatenate([imp, jnp.full((1, LANES), -jnp.inf, F32)], axis=1))
        imp_r = jnp.concatenate(rows + [jnp.full((LANES - N_KV_HEADS, n_tiles * LANES), -jnp.inf, F32)], axis=0)
        blk_r = _iota(imp_r.shape, 1)
        forced = (blk_r == qpos // NSA_BLOCK) | (blk_r == 0)
        imp_r = jnp.where(forced, jnp.inf, imp_r)
        imp_c = jnp.concatenate([imp_r[:, t * LANES:(t + 1) * LANES].T for t in range(n_tiles)], axis=0)
        n_col = _iota((n_tiles * LANES, LANES), 0)
        r_lane = _iota((n_tiles * LANES, LANES), 1)
        sel_ref[0] = jnp.zeros(sel_ref.shape[1:], I32)
        for g in range(N_KV_HEADS):
            col = imp_c[:, g:g + 1]
            rank = jnp.zeros((n_tiles * LANES, 1), F32)
            for t in range(n_tiles):
                row = imp_r[g:g + 1, t * LANES:(t + 1) * LANES]
                m_idx = t * LANES + _iota((n_tiles * LANES, LANES), 1)
                ahead = jnp.where(row > col, 1.0, jnp.where(row == col, jnp.where(m_idx < n_col, 1.0, 0.0), 0.0))
                ahead = jnp.where(m_idx < n_blocks, ahead, 0.0)
                rank = rank + jnp.sum(ahead, axis=-1, keepdims=True)
            hit = jnp.where((rank == r_lane.astype(F32)) & (n_col < n_blocks), 1.0, 0.0)
            idx_row = jnp.sum(hit * n_col.astype(F32), axis=0, keepdims=True)
            ok_row = jnp.sum(jnp.where(col > -jnp.inf, hit, 0.0), axis=0, keepdims=True)
            ok_row = jnp.where(_iota((1, LANES), 1) < k_sel, ok_row, 0.0)
            sel_ref[0, g:g + 1, :] = idx_row.astype(I32)
            sel_ref[0, N_KV_HEADS + g:N_KV_HEADS + g + 1, :] = ok_row.astype(I32)


def nsa_decode_compress(q, pe_k_t, pe_v_t, w_ck_t, w_cv_t, cache, layer, page_table):
    bsz, n_pages = page_table.shape
    n_chunks = n_pages // PAGES_PER_STEP
    page = cache.shape[3]
    n_blocks_past = n_pages * page // NSA_BLOCK
    assert n_blocks_past % LANES == 0 and LANES % (PAGES_PER_STEP * page // NSA_BLOCK) == 0
    k_sel = min(NSA_TOPK, n_blocks_past + 1)
    out_spec, out_shape = _head_out(bsz)
    full = lambda shape: pl.BlockSpec(shape, lambda b, c, *_: (0,) * len(shape))
    sums = pltpu.VMEM((n_blocks_past // LANES, KV_W, LANES), F32)
    return _decode_call(
        functools.partial(_nsa_dec_cmp_kernel, n_blocks_past=n_blocks_past, k_sel=k_sel),
        "nsa_decode_compress", page_table, n_chunks,
        [_q_spec(), full((LANES, NSA_BLOCK)), full((LANES, NSA_BLOCK)), full((LANES, LANES)), full((LANES, LANES))]
        + _page_specs(2 * KV_W, 0, layer, n_chunks, False, page),
        [q, pe_k_t, pe_v_t, w_ck_t, w_cv_t] + [cache] * PAGES_PER_STEP,
        [out_spec, pl.BlockSpec((1, N_Q_HEADS, LANES), lambda b, c, *_: (b, 0, 0))],
        [out_shape, jax.ShapeDtypeStruct((bsz, N_Q_HEADS, LANES), I32)],
        [sums, sums])


SELECTED_PER_STEP = 8


def _nsa_dec_sel_kernel(pt_ref, blk_ref, ok_ref, q_ref, kn_ref, vn_ref, *refs, n_blocks_past, k_sel):
    pages = refs[:SELECTED_PER_STEP]
    o_ref, m_ref, l_ref, acc_ref = refs[SELECTED_PER_STEP:]
    b = pl.program_id(0)
    s = pl.program_id(1)
    qf = q_ref[0] * HEAD_DIM ** -0.5

    @pl.when(s == 0)
    def _():
        m_ref[...] = jnp.full(m_ref.shape, NEG, F32)
        l_ref[...] = jnp.zeros_like(l_ref)
        acc_ref[...] = jnp.zeros_like(acc_ref)

    qs = qf.astype(BF16)
    blocks_per_page = pages[0].shape[3] // NSA_BLOCK
    zs, masks, vts = [], [], []
    for j in range(SELECTED_PER_STEP):
        i = s * SELECTED_PER_STEP + j
        blk = blk_ref[b, i]
        valid = (ok_ref[b, i] > 0) & (blk < n_blocks_past)
        z = _dot(qs, pages[j][0, 0, 0:KV_W, :].astype(BF16))
        in_block = (_iota(z.shape, 1) >> 6) == blk % blocks_per_page
        masks.append(jnp.where(in_block, _iota(z.shape, 0) >> 2, -2) == jnp.where(valid, i // k_sel, -1))
        zs.append(z)
        vts.append(pages[j][0, 0, KV_W:2 * KV_W, :].astype(BF16))
    state = _softmax_pages(zs, masks, vts, m_ref[:, 0:1], l_ref[:, 0:1], acc_ref[...])

    @pl.when(s < pl.num_programs(1) - 1)
    def _():
        m_ref[...] = jnp.broadcast_to(state[0], m_ref.shape)
        l_ref[...] = jnp.broadcast_to(state[1], l_ref.shape)
        acc_ref[...] = state[2]

    @pl.when(s == pl.num_programs(1) - 1)
    def _():
        z_new = jnp.sum(qf * _row_of(kn_ref, b), axis=-1, keepdims=True)
        _, l, acc = _softmax_token(z_new, _row_of(vn_ref, b), *state)
        o_ref[0] = acc / l


def nsa_decode_select(q, hs, sel, cache, layer, page_table):
    bsz, n_pages = page_table.shape
    page_len = cache.shape[3]
    n_blocks_past = n_pages * page_len // NSA_BLOCK
    k_sel = min(NSA_TOPK, n_blocks_past + 1)
    blocks_per_page = page_len // NSA_BLOCK
    blk = sel[:, 0:N_KV_HEADS, 0:k_sel].reshape(bsz, N_KV_HEADS * k_sel)
    ok = sel[:, N_KV_HEADS:2 * N_KV_HEADS, 0:k_sel].reshape(bsz, N_KV_HEADS * k_sel)
    out_spec, out_shape = _head_out(bsz)

    assert (N_KV_HEADS * k_sel) % SELECTED_PER_STEP == 0

    def page_index(j):
        def index(b, s, pt_ref, blk_ref, ok_ref):
            page = jnp.minimum(blk_ref[b, s * SELECTED_PER_STEP + j] // blocks_per_page, n_pages - 1)
            return (layer, pt_ref[b, page], 1, 0)
        return index

    stat = pltpu.VMEM((N_Q_HEADS, LANES), F32)
    return _decode_call(
        functools.partial(_nsa_dec_sel_kernel, n_blocks_past=n_blocks_past, k_sel=k_sel),
        "nsa_decode_select", page_table, None,
        [_q_spec(), _hs_spec(bsz, T_NSA_KS), _hs_spec(bsz, T_NSA_VS)]
        + [pl.BlockSpec((1, 1, 2 * KV_W, page_len), page_index(j)) for j in range(SELECTED_PER_STEP)],
        [q, hs, hs] + [cache] * SELECTED_PER_STEP, out_spec, out_shape, [stat, stat, stat],
        n_prefetch=3, grid=(bsz, N_KV_HEADS * k_sel // SELECTED_PER_STEP), prefetch=(page_table, blk, ok))


def _nsa_dec_win_kernel(q_ref, kn_ref, vn_ref, misc_ref, win_ref, ocmp_ref, osel_ref, o_ref):
    b = pl.program_id(0)
    qf = q_ref[0] * HEAD_DIM ** -0.5
    z = _dot(qf.astype(BF16), win_ref[0, 0, 0:KV_W, :].astype(BF16))
    state = _softmax_pages([z], None, [win_ref[0, 0, KV_W:2 * KV_W, :].astype(BF16)], *_softmax_init(N_Q_HEADS))
    z_new = jnp.sum(qf * _row_of(kn_ref, b), axis=-1, keepdims=True)
    _, l, acc = _softmax_token(z_new, _row_of(vn_ref, b), *state)
    gates = jax.nn.sigmoid(_row_of(misc_ref, b))
    o_ref[0] = (_diag_column(gates, MISC_NSAG, 3) * ocmp_ref[0] + _diag_column(gates, MISC_NSAG + 1, 3) * osel_ref[0]
                + _diag_column(gates, MISC_NSAG + 2, 3) * (acc / l))


def nsa_decode_combine(q, hs, win_state, layer, o_cmp, o_sel):
    bsz = q.shape[0]
    w_rows = win_state.shape[3]
    head = pl.BlockSpec((1, N_Q_HEADS, LANES), lambda b: (b, 0, 0))
    tile = lambda t: pl.BlockSpec((bsz, LANES), lambda b: (0, t))
    return pl.pallas_call(
        _nsa_dec_win_kernel,
        grid=(bsz,),
        in_specs=[head, tile(T_NSA_KW), tile(T_NSA_VW), tile(T_MISC),
                  pl.BlockSpec((1, 1, 2 * KV_W, w_rows), lambda b: (layer, b, 0, 0)), head, head],
        out_specs=head,
        out_shape=jax.ShapeDtypeStruct((bsz, N_Q_HEADS, LANES), F32),
        compiler_params=_params("arbitrary"),
        name="nsa_decode_combine",
    )(q, hs, hs, hs, win_state, o_cmp, o_sel)


def _dsa_dec_index_kernel(pt_ref, iq_ref, w_ref, kn_ref, *refs, topk):
    pages = refs[:PAGES_PER_STEP]
    mask_ref, flag_ref, key_ref = refs[PAGES_PER_STEP:]
    b = pl.program_id(0)
    c = pl.program_id(1)
    iq = iq_ref[0]
    w = w_ref[0]
    for p in range(PAGES_PER_STEP):
        sc = jnp.maximum(_dot3(iq, pages[p][0, 0]), 0.0)
        tot = jnp.sum(w * sc, axis=0, keepdims=True)
        key_ref[pl.ds(c * PAGES_PER_STEP + p, 1), :] = _order_key(tot)

    @pl.when(c == pl.num_programs(1) - 1)
    def _():
        k_new = _row_of(kn_ref, b)[:, 0:IDX_DIM]
        sc_new = jnp.maximum(jnp.sum(iq * k_new, axis=-1, keepdims=True), 0.0)
        key_new = _order_key(jnp.sum(w[:, 0:1] * sc_new, axis=0, keepdims=True))
        keys = key_ref[...]

        def total(x):
            return jnp.sum(jnp.sum(x, axis=-1, keepdims=True), axis=0, keepdims=True)

        def count(pred):
            return total(jnp.where(pred(keys), 1.0, 0.0)) + jnp.where(pred(key_new), 1.0, 0.0)

        k = float(topk)
        kappa = _kth_largest_key(lambda cand: count(lambda x: x >= cand), (1, 1), k)
        need = k - count(lambda x: x > kappa)
        tie = jnp.where(keys == kappa, 1.0, 0.0)
        n_rows = keys.shape[0]
        in_row = _dot(tie.astype(BF16), _strict_lower_ones(LANES))
        row_tot = jnp.broadcast_to(jnp.sum(tie, axis=-1, keepdims=True), (n_rows, LANES))
        rows_before = jnp.where(_iota((n_rows, n_rows), 1) < _iota((n_rows, n_rows), 0), 1.0, 0.0).astype(BF16)
        rank = in_row + _dot(rows_before, row_tot.astype(BF16))
        chosen = jnp.where(keys > kappa, 1.0, jnp.where(rank < need, tie, 0.0))
        mask_ref[0] = jnp.where(keys > NEG_INF_KEY, chosen, 0.0)
        new_in = jnp.where(key_new > kappa, 1.0, jnp.where((key_new == kappa) & (total(tie) < need), 1.0, 0.0))
        flag_ref[0] = jnp.broadcast_to(new_in, flag_ref.shape[1:])


def dsa_decode_index(iq, w_rows, hs, cache, layer, page_table):
    bsz, n_pages = page_table.shape
    n_chunks = n_pages // PAGES_PER_STEP
    rows = cache.shape[3]
    assert rows == LANES
    topk = min(DSA_TOPK_MAX, (n_pages * rows + 1) // 4)
    return _decode_call(
        functools.partial(_dsa_dec_index_kernel, topk=topk), "dsa_decode_index", page_table, n_chunks,
        [pl.BlockSpec((1, N_Q_HEADS, IDX_DIM), lambda b, c, *_: (b, 0, 0)), _q_spec(), _hs_spec(bsz, T_IDXK)]
        + _page_specs(IDX_DIM, 0, layer, n_chunks, False, rows),
        [iq, w_rows, hs] + [cache] * PAGES_PER_STEP,
        [pl.BlockSpec((1, n_pages, rows), lambda b, c, *_: (b, 0, 0)),
         pl.BlockSpec((1, N_Q_HEADS, LANES), lambda b, c, *_: (b, 0, 0))],
        [jax.ShapeDtypeStruct((bsz, n_pages, rows), F32), jax.ShapeDtypeStruct((bsz, N_Q_HEADS, LANES), F32)],
        [pltpu.VMEM((n_pages, rows), I32)])


def _dsa_dec_attn_kernel(pt_ref, q_ref, kn_ref, vn_ref, mask_ref, flag_ref, *refs):
    pages = refs[:PAGES_PER_STEP]
    o_ref, m_ref, l_ref, acc_ref = refs[PAGES_PER_STEP:]
    b = pl.program_id(0)
    c = pl.program_id(1)
    qf = q_ref[0] * HEAD_DIM ** -0.5

    @pl.when(c == 0)
    def _():
        m_ref[...] = jnp.full(m_ref.shape, NEG, F32)
        l_ref[...] = jnp.zeros_like(l_ref)
        acc_ref[...] = jnp.zeros_like(acc_ref)

    qs = qf.astype(BF16)
    zs = [_dot(qs, pages[p][0, 0, 0:KV_W, :].astype(BF16)) for p in range(PAGES_PER_STEP)]
    masks = [jnp.broadcast_to(mask_ref[0, p:p + 1, :], (N_Q_HEADS, LANES)) > 0.5 for p in range(PAGES_PER_STEP)]
    vts = [pages[p][0, 0, KV_W:2 * KV_W, :].astype(BF16) for p in range(PAGES_PER_STEP)]
    state = _softmax_pages(zs, masks, vts, m_ref[:, 0:1], l_ref[:, 0:1], acc_ref[...])
    m_ref[...] = jnp.broadcast_to(state[0], m_ref.shape)
    l_ref[...] = jnp.broadcast_to(state[1], l_ref.shape)
    acc_ref[...] = state[2]

    @pl.when(c == pl.num_programs(1) - 1)
    def _():
        z_new = jnp.sum(qf * _row_of(kn_ref, b), axis=-1, keepdims=True)
        z_new = jnp.where(flag_ref[0][:, 0:1] > 0.5, z_new, NEG)
        m, l, acc = state
        m_new = jnp.maximum(m, z_new)
        alpha = jnp.exp(m - m_new)
        p_new = jnp.where(flag_ref[0][:, 0:1] > 0.5, jnp.exp(z_new - m_new), 0.0)
        l = alpha * l + p_new
        acc = alpha * acc + p_new * _row_of(vn_ref, b)
        o_ref[0] = acc / jnp.maximum(l, 1e-30)


def dsa_decode_attention(q, hs, mask, flag, cache, layer, page_table):
    bsz, n_pages = page_table.shape
    n_chunks = n_pages // PAGES_PER_STEP
    rows = cache.shape[3]
    out_spec, out_shape = _head_out(bsz)
    stat = pltpu.VMEM((N_Q_HEADS, LANES), F32)
    return _decode_call(
        _dsa_dec_attn_kernel, "dsa_decode_attention", page_table, n_chunks,
        [_q_spec(), _hs_spec(bsz, T_DSA_K), _hs_spec(bsz, T_DSA_V),
         pl.BlockSpec((1, PAGES_PER_STEP, rows), lambda b, c, *_: (b, c, 0)),
         pl.BlockSpec((1, N_Q_HEADS, LANES), lambda b, c, *_: (b, 0, 0))]
        + _page_specs(2 * KV_W, 0, layer, n_chunks, False, rows),
        [q, hs, hs, mask, flag] + [cache] * PAGES_PER_STEP, out_spec, out_shape, [stat, stat, stat])


def _rope_tables(pos):
    half = HEAD_DIM // 2
    freq = ROPE_THETA ** (-jnp.arange(half, dtype=F32) / half)
    ang = pos.astype(F32)[:, None] * freq[None, :]
    cos = jnp.cos(ang)
    sin = jnp.sin(ang)
    return jnp.tile(cos, (1, 4)), jnp.tile(jnp.concatenate([-sin, sin], axis=1), (1, 2))


def _prepare_weights(p):
    depth = p['w_in'].shape[0]
    w_in = jnp.concatenate([p['w_in'], jnp.zeros((depth, D_MODEL, 1), F32)], axis=-1)
    eye2 = jnp.eye(N_KV_HEADS, dtype=F32)
    w = {
        'w_in': jnp.take(w_in, jnp.asarray(_proj_column_perm()), axis=-1).astype(BF16),
        'w_gate': p['w_gate'].astype(BF16),
        'b_gate': p['b_gate'].reshape(depth, 1, N_BRANCH * D_MODEL),
        'w_br': jnp.take(p['w_br'], jnp.asarray(_y_row_perm()), axis=2).astype(BF16),
        'w_o': p['w_o'].astype(BF16),
        'w_mq': p['w_mq'].astype(BF16),
        'w_mo': p['w_mo'].astype(BF16),
        'w_mkv': p['w_mkv'].astype(BF16),
        'w_router': jnp.concatenate(
            [p['w_rg'], p['w_re'], jnp.zeros((depth, D_MODEL, LANES - N_GROUPS - N_EXPERTS), F32)], axis=-1),
        'b_router': jnp.concatenate(
            [p['b_rg'], p['b_re'], jnp.zeros((depth, LANES - N_GROUPS - N_EXPERTS), F32)], axis=-1)[:, None, :],
        'w_eg': p['w_eg'].astype(BF16),
        'w_eu': p['w_eu'].astype(BF16),
        'w_ed': p['w_ed'].astype(BF16),
        'pe_k': jnp.tile(p['nsa_pe_k'], (1, 1, N_KV_HEADS)),
        'pe_v': jnp.tile(p['nsa_pe_v'], (1, 1, N_KV_HEADS)),
        'w_ck': jnp.einsum('gh,lde->lgdhe', eye2, p['nsa_w_ck']).reshape(depth, LANES, LANES),
        'w_cv': jnp.einsum('gh,lde->lgdhe', eye2, p['nsa_w_cv']).reshape(depth, LANES, LANES),
    }
    for name in ('pe_k', 'pe_v', 'w_ck', 'w_cv'):
        w[name + '_t'] = jnp.swapaxes(w[name], 1, 2)
    for name in ('fox_b_f', 'ln1_g', 'ln1_b', 'ln2_g', 'ln2_b', 'ln3_g', 'ln3_b'):
        w[name] = p[name]
    return w


def _tiles(h, tiles):
    parts = [h[:, :, t * LANES:(t + 1) * LANES] for t in tiles]
    return jnp.stack(parts, axis=2).reshape(h.shape[0], h.shape[1], len(tiles), N_KV_HEADS, HEAD_DIM)


def _new_state(h, logf):
    return {
        'sb_kv': _tiles(h, (T_SB_K, T_SB_V)),
        'nsa_kv': _tiles(h, (T_NSA_KC, T_NSA_VC, T_NSA_KS, T_NSA_VS)),
        'nsa_win_kv': _tiles(h, (T_NSA_KW, T_NSA_VW)),
        'dsa_kv': _tiles(h, (T_DSA_K, T_DSA_V)),
        'dsa_idx_k': h[:, :, T_IDXK * LANES:T_IDXK * LANES + IDX_DIM],
        'fox_kv': _tiles(h, (T_FOX_K, T_FOX_V)),
        'fox_logf': logf[:, :, MISC_FOXF:MISC_FOXF + N_Q_HEADS],
    }


def _dense_tail(x, ys, mem_kv, layer, w, alpha, tm, tm_mem, bsz):
    n = x.shape[0]
    l = layer
    merged = gated_merge(x, ys, w['w_gate'][l], w['b_gate'][l], w['w_br'][l], tm)
    x = matmul_res_ln(merged, w['w_o'][l], x, w['ln1_g'][l], w['ln1_b'][l], alpha, min(tm, 256))
    t = n // bsz
    xm = x.reshape(bsz, t, D_MODEL)
    if t < tm_mem:
        xm = jnp.broadcast_to(xm[:, 0:1], (bsz, tm_mem, D_MODEL))
    x = memory_block(xm, mem_kv, l, w['w_mq'][l], w['w_mo'][l], w['ln2_g'][l], w['ln2_b'][l], alpha,
                     tm_mem)[:, 0:t].reshape(n, D_MODEL)
    return moe_block(x, w['w_router'][l], w['b_router'][l], w['w_eg'][l], w['w_eu'][l], w['w_ed'][l],
                     w['ln3_g'][l], w['ln3_b'][l], alpha, tm)


def _prompt_layer(x, mem_kv, layer, w, rope, alpha, bsz):
    n = x.shape[0]
    t = n // bsz
    l = layer
    tm = min(512, t)
    h = project(x, w['w_in'][l], rope[0], rope[1], min(1024, t)).reshape(bsz, t, PROJ_W)
    logf, c_tok = fox_prep(h, w['fox_b_f'][l])
    cmp_k, cmp_v = nsa_compress(h, w['pe_k'][l], w['pe_v'][l], w['w_ck'][l], w['w_cv'][l])
    ys = [sb_attention(h), nsa_attention(h, cmp_k, cmp_v), dsa_attention(h), fox_attention(h, c_tok)]
    ys = [y.reshape(n, MIX_W) for y in ys]
    x = _dense_tail(x, ys, mem_kv, l, w, alpha, tm, min(256, t), bsz)
    return x, _new_state(h, logf)


def _feature_major_caches(cache_sb_kv, cache_nsa_kv, state_nsa_win_kv, cache_dsa_kv, cache_dsa_idx_k,
                          cache_fox_kv, cache_fox_logf):
    def rows_last(c):
        t = jnp.moveaxis(c, 2, -1)
        return t.reshape(t.shape[0], t.shape[1], -1, t.shape[-1])
    return {
        'sb_kv': rows_last(cache_sb_kv),
        'nsa_kv': rows_last(cache_nsa_kv),
        'nsa_win': rows_last(state_nsa_win_kv),
        'dsa_kv': rows_last(cache_dsa_kv),
        'dsa_idx_k': rows_last(cache_dsa_idx_k),
        'fox_kv': rows_last(cache_fox_kv),
        'fox_logf_t': rows_last(cache_fox_logf),
    }


def _decode_q(hs, tile0):
    q4 = hs[:, tile0 * LANES:(tile0 + GQA_GROUP) * LANES].reshape(hs.shape[0], GQA_GROUP, LANES)
    low = jnp.arange(LANES) < HEAD_DIM
    return jnp.concatenate([jnp.where(low, q4, 0.0), jnp.where(low, 0.0, q4)], axis=1)


def _decode_y(o):
    low = jnp.arange(LANES) < HEAD_DIM
    return jnp.where(low, o[:, 0:GQA_GROUP], o[:, GQA_GROUP:]).reshape(o.shape[0], MIX_W)


def _sample_layer(x, caches, mem_kv, layer, w, rope, alpha, page_table):
    bsz = x.shape[0]
    l = layer
    hs = project(x, w['w_in'][l], rope[0], rope[1], bsz)
    q_nsa = _decode_q(hs, T_NSAQ)
    o_cmp, sel = nsa_decode_compress(q_nsa, w['pe_k_t'][l], w['pe_v_t'][l], w['w_ck_t'][l], w['w_cv_t'][l],
                                     caches['nsa_kv'], l, page_table)
    o_sel = nsa_decode_select(q_nsa, hs, sel, caches['nsa_kv'], l, page_table)
    y_nsa = nsa_decode_combine(q_nsa, hs, caches['nsa_win'], l, o_cmp, o_sel)
    iq = hs[:, T_IDXQ * LANES:(T_IDXQ + 2) * LANES].reshape(bsz, IDX_HEADS, IDX_DIM)
    iq = jnp.concatenate([iq, jnp.zeros((bsz, N_Q_HEADS - IDX_HEADS, IDX_DIM), F32)], axis=1)
    iw = hs[:, T_MISC * LANES + MISC_IDXW:T_MISC * LANES + MISC_IDXW + IDX_HEADS]
    iw = jnp.concatenate([iw, jnp.zeros((bsz, N_Q_HEADS - IDX_HEADS), F32)], axis=1)
    iw = jnp.broadcast_to(iw[:, :, None], (bsz, N_Q_HEADS, LANES))
    mask, flag = dsa_decode_index(iq, iw, hs, caches['dsa_idx_k'], l, page_table)
    y_dsa = dsa_decode_attention(_decode_q(hs, T_DSAQ), hs, mask, flag, caches['dsa_kv'], l, page_table)
    bias_row = jnp.zeros((1, LANES), F32).at[0, MISC_FOXF:MISC_FOXF + N_Q_HEADS].set(w['fox_b_f'][l])
    y_fox, logf = fox_decode(_decode_q(hs, T_FOXQ), hs, bias_row, caches['fox_kv'], caches['fox_logf_t'], l,
                             page_table)
    y_sb = sb_decode(_decode_q(hs, T_SBQ), caches['sb_kv'], l, page_table)
    ys = [_decode_y(y) for y in (y_sb, y_nsa, y_dsa, y_fox)]
    x = _dense_tail(x, ys, mem_kv, l, w, alpha, bsz, 8, bsz)
    return x, _new_state(hs[:, None, :], logf)


def kernel(x_prompt, x_sample, mem_prompt, cache_sb_kv, cache_nsa_kv, state_nsa_win_kv, cache_dsa_kv,
           cache_dsa_idx_k, cache_fox_kv, cache_fox_logf, cache_mem_kv, page_table,
           ln_in_g, ln_in_b, w_in, fox_b_f, nsa_pe_k, nsa_pe_v, nsa_w_ck, nsa_w_cv, w_br, w_gate, b_gate, w_o,
           ln1_g, ln1_b, w_mq, w_mkv, w_mo, ln2_g, ln2_b, w_rg, b_rg, w_re, b_re, w_eg, w_eu, w_ed, ln3_g, ln3_b):
    depth = w_in.shape[0]
    bsz, seq, d = x_prompt.shape
    dec_b, dec_seq, _ = x_sample.shape
    n_pool, page = cache_sb_kv.shape[1:3]
    past_len = page_table.shape[1] * page
    assert dec_seq == 1 and state_nsa_win_kv.shape[2] == NSA_WINDOW and past_len >= NSA_WINDOW
    alpha = (2.0 * depth) ** 0.25
    w = _prepare_weights(dict(
        w_in=w_in, fox_b_f=fox_b_f, nsa_pe_k=nsa_pe_k, nsa_pe_v=nsa_pe_v, nsa_w_ck=nsa_w_ck, nsa_w_cv=nsa_w_cv,
        w_br=w_br, w_gate=w_gate, b_gate=b_gate, w_o=w_o, ln1_g=ln1_g, ln1_b=ln1_b, w_mq=w_mq, w_mkv=w_mkv,
        w_mo=w_mo, ln2_g=ln2_g, ln2_b=ln2_b, w_rg=w_rg, b_rg=b_rg, w_re=w_re, b_re=b_re, w_eg=w_eg, w_eu=w_eu,
        w_ed=w_ed, ln3_g=ln3_g, ln3_b=ln3_b))

    mem_rows = mem_prompt.reshape(bsz * N_MEM, d)
    mem_kv_p = jnp.stack([matmul(mem_rows, w['w_mkv'][l], 256, 512) for l in range(depth)], axis=0)
    mem_kv_p = mem_kv_p.reshape(depth, bsz, N_MEM, 2 * MEM_W)
    rope_p = _rope_tables(jnp.arange(seq))
    x = layer_norm_rows(x_prompt.reshape(bsz * seq, d), ln_in_g, ln_in_b, 256)
    st_p = []
    for l in range(depth):
        x, new = _prompt_layer(x, mem_kv_p, l, w, rope_p, alpha, bsz)
        st_p.append(new)
    y_prompt = x.reshape(bsz, seq, d)

    caches = _feature_major_caches(cache_sb_kv, cache_nsa_kv, state_nsa_win_kv, cache_dsa_kv, cache_dsa_idx_k,
                                   cache_fox_kv, cache_fox_logf)
    mem_kv_s = cache_mem_kv.reshape(depth, dec_b, N_MEM, 2 * MEM_W)
    rope_s = _rope_tables(jnp.full((dec_b,), past_len))
    x = layer_norm_rows(x_sample.reshape(dec_b, d), ln_in_g, ln_in_b, dec_b)
    st_s = []
    for l in range(depth):
        x, new = _sample_layer(x, caches, mem_kv_s, l, w, rope_s, alpha, page_table)
        new['nsa_win_kv'] = jnp.concatenate([state_nsa_win_kv[l][:, 1:], new['nsa_win_kv']], axis=1)
        st_s.append(new)
    y_sample = x.reshape(dec_b, dec_seq, d)

    def stacked(states, name):
        return jnp.stack([s[name] for s in states], axis=0)

    win_p = stacked(st_p, 'nsa_win_kv')[:, :, seq - min(NSA_WINDOW, seq):]
    return (y_prompt, y_sample,
            stacked(st_p, 'sb_kv'), stacked(st_s, 'sb_kv'),
            stacked(st_p, 'nsa_kv'), stacked(st_s, 'nsa_kv'),
            win_p, stacked(st_s, 'nsa_win_kv'),
            stacked(st_p, 'dsa_kv'), stacked(st_s, 'dsa_kv'),
            stacked(st_p, 'dsa_idx_k'), stacked(st_s, 'dsa_idx_k'),
            stacked(st_p, 'fox_kv'), stacked(st_s, 'fox_kv'),
            stacked(st_p, 'fox_logf'), stacked(st_s, 'fox_logf'),
            mem_kv_p.reshape(depth, bsz, N_MEM, 2, MEM_HEADS, MEM_HEAD_DIM))
```

```python
import functools
import math

import jax
import jax.numpy as jnp
import numpy as np
from jax import lax
from jax.experimental import pallas as pl
from jax.experimental.pallas import tpu as pltpu

F32 = jnp.float32
BF16 = jnp.bfloat16
I32 = jnp.int32

D_MODEL = 2048
HEAD_DIM = 64
N_Q_HEADS = 8
N_KV_HEADS = 2
GQA_GROUP = 4
MIX_W = 512
KV_W = 128
N_BRANCH = 4
ROPE_THETA = 10000.0
LN_EPS = 1e-5
NSA_BLOCK = 64
NSA_TOPK = 16
NSA_WINDOW = 512
IDX_HEADS = 4
IDX_DIM = 64
DSA_TOPK_MAX = 256
N_MEM = 256
MEM_HEADS = 4
MEM_HEAD_DIM = 128
MEM_W = 512
N_GROUPS = 4
EXPERTS_PER_GROUP = 4
N_EXPERTS = 16
D_FF_EXPERT = 256

LANES = 128
NEG = -1e30
F32_EXP_ZERO = -104.0
VMEM_LIMIT = 48 * 1024 * 1024

IN_SPLITS = (
    ('sb_q', MIX_W), ('sb_k', KV_W), ('sb_v', KV_W),
    ('nsa_q', MIX_W), ('nsa_kc', KV_W), ('nsa_vc', KV_W), ('nsa_ks', KV_W), ('nsa_vs', KV_W),
    ('nsa_kw', KV_W), ('nsa_vw', KV_W), ('nsa_g', 3 * N_Q_HEADS),
    ('dsa_q', MIX_W), ('dsa_k', KV_W), ('dsa_v', KV_W),
    ('idx_q', IDX_HEADS * IDX_DIM), ('idx_k', IDX_DIM), ('idx_w', IDX_HEADS),
    ('fox_q', MIX_W), ('fox_k', KV_W), ('fox_v', KV_W), ('fox_f', N_Q_HEADS),
)
N_IN = sum(w for _, w in IN_SPLITS)

T_NSAQ, T_DSAQ, T_IDXQ, T_IDXK, T_NSA_KC, T_NSA_KS, T_NSA_KW, T_DSA_K = 0, 4, 8, 10, 11, 12, 13, 14
N_ROPE_TILES = 16
T_SBQ, T_FOXQ, T_SB_K, T_SB_V, T_NSA_VC, T_NSA_VS, T_NSA_VW, T_DSA_V, T_FOX_K, T_FOX_V, T_MISC = (
    16, 20, 24, 25, 26, 27, 28, 29, 30, 31, 32)
N_TILES = 34
PROJ_W = N_TILES * LANES
MISC_FOXF, MISC_NSAG, MISC_IDXW = 0, 8, 32


def _in_offsets():
    off, out = 0, {}
    for name, width in IN_SPLITS:
        out[name] = off
        off += width
    return out


def _proj_column_perm():
    off = _in_offsets()
    perm = np.full((PROJ_W,), N_IN, np.int32)

    def put(tile, lane, name, start, width):
        base = tile * LANES + lane
        perm[base:base + width] = off[name] + start + np.arange(width)

    def put_q(tile0, name):
        for r in range(GQA_GROUP):
            put(tile0 + r, 0, name, r * HEAD_DIM, HEAD_DIM)
            put(tile0 + r, HEAD_DIM, name, (GQA_GROUP + r) * HEAD_DIM, HEAD_DIM)

    put_q(T_NSAQ, 'nsa_q')
    put_q(T_DSAQ, 'dsa_q')
    put_q(T_SBQ, 'sb_q')
    put_q(T_FOXQ, 'fox_q')
    put(T_IDXQ, 0, 'idx_q', 0, 128)
    put(T_IDXQ + 1, 0, 'idx_q', 128, 128)
    put(T_IDXK, 0, 'idx_k', 0, 64)
    put(T_IDXK, 64, 'idx_k', 0, 64)
    for tile, name in ((T_NSA_KC, 'nsa_kc'), (T_NSA_KS, 'nsa_ks'), (T_NSA_KW, 'nsa_kw'), (T_DSA_K, 'dsa_k'),
                       (T_SB_K, 'sb_k'), (T_SB_V, 'sb_v'), (T_NSA_VC, 'nsa_vc'), (T_NSA_VS, 'nsa_vs'),
                       (T_NSA_VW, 'nsa_vw'), (T_DSA_V, 'dsa_v'), (T_FOX_K, 'fox_k'), (T_FOX_V, 'fox_v')):
        put(tile, 0, name, 0, 128)
    put(T_MISC, MISC_FOXF, 'fox_f', 0, 8)
    put(T_MISC, MISC_NSAG, 'nsa_g', 0, 24)
    put(T_MISC, MISC_IDXW, 'idx_w', 0, 4)
    return perm


def _y_row_perm():
    perm = np.zeros((MIX_W,), np.int32)
    for r in range(GQA_GROUP):
        for half in range(2):
            h = half * GQA_GROUP + r
            perm[r * LANES + half * HEAD_DIM:r * LANES + (half + 1) * HEAD_DIM] = h * HEAD_DIM + np.arange(HEAD_DIM)
    return perm


def _dot(a, b):
    return jnp.dot(a, b, preferred_element_type=F32)


def _dot_nt(a, b):
    return lax.dot_general(a, b, (((1,), (1,)), ((), ())), preferred_element_type=F32)


def _split2(x):
    hi = x.astype(BF16)
    lo = (x - hi.astype(F32)).astype(BF16)
    return hi, lo


def _dot_x_exact(x, m_bf16):
    hi, lo = _split2(x)
    return _dot(hi, m_bf16) + _dot(lo, m_bf16)


def _dot_exact_x(m_bf16, x):
    hi, lo = _split2(x)
    return _dot(m_bf16, hi) + _dot(m_bf16, lo)


def _dot3(a, b):
    ah, al = _split2(a)
    bh, bl = _split2(b)
    return _dot(ah, bh) + (_dot(ah, bl) + _dot(al, bh))


def _dot3_nt(a, b):
    ah, al = _split2(a)
    bh, bl = _split2(b)
    return _dot_nt(ah, bh) + (_dot_nt(ah, bl) + _dot_nt(al, bh))


def _iota(shape, dim):
    return lax.broadcasted_iota(I32, shape, dim)


def _layer_norm(v, g, b):
    mu = jnp.mean(v, axis=-1, keepdims=True)
    c = v - mu
    var = jnp.mean(c * c, axis=-1, keepdims=True)
    return c * lax.rsqrt(var + LN_EPS) * g + b


def _stack_q(q, scale):
    tq = q.shape[0]
    low = _iota((tq, LANES), 1) < HEAD_DIM
    parts = []
    for g in range(N_KV_HEADS):
        for r in range(GQA_GROUP):
            t = q[:, r * LANES:(r + 1) * LANES]
            parts.append(jnp.where(low if g == 0 else jnp.logical_not(low), t * scale, 0.0))
    return jnp.concatenate(parts, axis=0)


def _store_heads(o_ref, o, tq):
    low = _iota((tq, LANES), 1) < HEAD_DIM
    for r in range(GQA_GROUP):
        lo_head = o[r * tq:(r + 1) * tq]
        hi_head = o[(GQA_GROUP + r) * tq:(GQA_GROUP + r + 1) * tq]
        o_ref[0, :, r * LANES:(r + 1) * LANES] = jnp.where(low, lo_head, hi_head)


def _rep_rows(x, n):
    return jnp.concatenate([x] * n, axis=0)


def _params(*sem):
    return pltpu.CompilerParams(dimension_semantics=sem, vmem_limit_bytes=VMEM_LIMIT)


def _ln_kernel(x_ref, g_ref, b_ref, o_ref):
    o_ref[...] = _layer_norm(x_ref[...], g_ref[...], b_ref[...])


def layer_norm_rows(x, g, b, tm):
    n, d = x.shape
    return pl.pallas_call(
        _ln_kernel,
        grid=(n // tm,),
        in_specs=[pl.BlockSpec((tm, d), lambda i: (i, 0)),
                  pl.BlockSpec((1, d), lambda i: (0, 0)),
                  pl.BlockSpec((1, d), lambda i: (0, 0))],
        out_specs=pl.BlockSpec((tm, d), lambda i: (i, 0)),
        out_shape=jax.ShapeDtypeStruct((n, d), F32),
        compiler_params=_params("arbitrary"),
        name="ln_in",
    )(x, g.reshape(1, d), b.reshape(1, d))


def _proj_kernel(x_ref, w_ref, cos_ref, sin_ref, o_ref, xb_ref, *, n_rope_blocks, tn):
    j = pl.program_id(1)

    @pl.when(j == 0)
    def _():
        xb_ref[...] = x_ref[...].astype(BF16)

    acc = _dot(xb_ref[...], w_ref[...])

    @pl.when(j < n_rope_blocks)
    def _():
        cos = cos_ref[...]
        sin = sin_ref[...]
        first = (_iota(cos.shape, 1) & (HEAD_DIM - 1)) < HEAD_DIM // 2
        for s in range(tn // LANES):
            a = acc[:, s * LANES:(s + 1) * LANES]
            rot = jnp.where(first, pltpu.roll(a, LANES - HEAD_DIM // 2, 1), pltpu.roll(a, HEAD_DIM // 2, 1))
            o_ref[:, s * LANES:(s + 1) * LANES] = a * cos + rot * sin

    @pl.when(j >= n_rope_blocks)
    def _():
        o_ref[...] = acc


def project(x, w_perm, cos_t, sin_t, tm, tn=256):
    n, d = x.shape
    n_pos_blocks = cos_t.shape[0] // tm
    return pl.pallas_call(
        functools.partial(_proj_kernel, n_rope_blocks=N_ROPE_TILES * LANES // tn, tn=tn),
        grid=(n // tm, PROJ_W // tn),
        in_specs=[pl.BlockSpec((tm, d), lambda i, j: (i, 0)),
                  pl.BlockSpec((d, tn), lambda i, j: (0, j)),
                  pl.BlockSpec((tm, LANES), lambda i, j: (i % n_pos_blocks, 0)),
                  pl.BlockSpec((tm, LANES), lambda i, j: (i % n_pos_blocks, 0))],
        out_specs=pl.BlockSpec((tm, tn), lambda i, j: (i, j)),
        out_shape=jax.ShapeDtypeStruct((n, PROJ_W), F32),
        scratch_shapes=[pltpu.VMEM((tm, d), BF16)],
        compiler_params=_params("arbitrary", "arbitrary"),
        name="proj",
    )(x, w_perm, cos_t, sin_t)


def _matmul_kernel(x_ref, w_ref, o_ref):
    o_ref[...] = _dot(x_ref[...].astype(BF16), w_ref[...])


def matmul(x, w, tm, tn):
    n, d = x.shape
    m = w.shape[1]
    return pl.pallas_call(
        _matmul_kernel,
        grid=(n // tm, m // tn),
        in_specs=[pl.BlockSpec((tm, d), lambda i, j: (i, 0)),
                  pl.BlockSpec((d, tn), lambda i, j: (0, j))],
        out_specs=pl.BlockSpec((tm, tn), lambda i, j: (i, j)),
        out_shape=jax.ShapeDtypeStruct((n, m), F32),
        compiler_params=_params("arbitrary", "arbitrary"),
        name="matmul",
    )(x, w)


def _merge_kernel(x_ref, ya_ref, yb_ref, yc_ref, yd_ref, wg0, wg1, wg2, wg3, bg0, bg1, bg2, bg3,
                  wb0, wb1, wb2, wb3, o_ref, xb_ref, yb16_ref):
    j = pl.program_id(1)

    @pl.when(j == 0)
    def _():
        xb_ref[...] = x_ref[...].astype(BF16)
        for b, y in enumerate((ya_ref, yb_ref, yc_ref, yd_ref)):
            yb16_ref[b] = y[...].astype(BF16)

    xb = xb_ref[...]
    acc = None
    for b, (wg, bg, wb) in enumerate(((wg0, bg0, wb0), (wg1, bg1, wb1), (wg2, bg2, wb2), (wg3, bg3, wb3))):
        gate = jax.nn.sigmoid(_dot(xb, wg[...]) + bg[...])
        term = gate * _dot(yb16_ref[b], wb[0])
        acc = term if acc is None else acc + term
    o_ref[...] = acc.astype(BF16)


def gated_merge(x, ys, w_gate, b_gate, w_br, tm, tn=256):
    n, d = x.shape
    nj = d // tn
    in_specs = [pl.BlockSpec((tm, d), lambda i, j: (i, 0))]
    in_specs += [pl.BlockSpec((tm, MIX_W), lambda i, j: (i, 0)) for _ in range(N_BRANCH)]
    in_specs += [pl.BlockSpec((d, tn), functools.partial(lambda i, j, b: (0, b * nj + j), b=b)) for b in range(N_BRANCH)]
    in_specs += [pl.BlockSpec((1, tn), functools.partial(lambda i, j, b: (0, b * nj + j), b=b)) for b in range(N_BRANCH)]
    in_specs += [pl.BlockSpec((1, MIX_W, tn), functools.partial(lambda i, j, b: (b, 0, j), b=b)) for b in range(N_BRANCH)]
    return pl.pallas_call(
        _merge_kernel,
        grid=(n // tm, nj),
        in_specs=in_specs,
        out_specs=pl.BlockSpec((tm, tn), lambda i, j: (i, j)),
        out_shape=jax.ShapeDtypeStruct((n, d), BF16),
        scratch_shapes=[pltpu.VMEM((tm, d), BF16), pltpu.VMEM((N_BRANCH, tm, MIX_W), BF16)],
        compiler_params=_params("arbitrary", "arbitrary"),
        name="gated_merge",
    )(x, *ys, *([w_gate] * N_BRANCH), *([b_gate] * N_BRANCH), *([w_br] * N_BRANCH))


def _matmul_res_ln_kernel(a_ref, w_ref, r_ref, g_ref, b_ref, o_ref, *, alpha):
    y = _dot(a_ref[...].astype(BF16), w_ref[...])
    o_ref[...] = _layer_norm(alpha * r_ref[...] + y, g_ref[...], b_ref[...])


def matmul_res_ln(a, w, res, g, b, alpha, tm):
    n, k = a.shape
    d = w.shape[1]
    return pl.pallas_call(
        functools.partial(_matmul_res_ln_kernel, alpha=alpha),
        grid=(n // tm,),
        in_specs=[pl.BlockSpec((tm, k), lambda i: (i, 0)),
                  pl.BlockSpec((k, d), lambda i: (0, 0)),
                  pl.BlockSpec((tm, d), lambda i: (i, 0)),
                  pl.BlockSpec((1, d), lambda i: (0, 0)),
                  pl.BlockSpec((1, d), lambda i: (0, 0))],
        out_specs=pl.BlockSpec((tm, d), lambda i: (i, 0)),
        out_shape=jax.ShapeDtypeStruct((n, d), F32),
        compiler_params=_params("arbitrary"),
        name="matmul_res_ln",
    )(a, w, res, g.reshape(1, d), b.reshape(1, d))


def _mem_kernel(x_ref, wq_ref, k_ref, v_ref, wo_ref, g_ref, b_ref, o_ref, *, alpha):
    x = x_ref[0]
    q = _dot(x.astype(BF16), wq_ref[...])
    outs = []
    for h in range(MEM_HEADS):
        sl = slice(h * MEM_HEAD_DIM, (h + 1) * MEM_HEAD_DIM)
        z = _dot_nt(q[:, sl].astype(BF16), k_ref[0, 0, :, sl].astype(BF16)) * (MEM_HEAD_DIM ** -0.5)
        m = jnp.max(z, axis=-1, keepdims=True)
        e = jnp.exp(z - m)
        p = e / jnp.sum(e, axis=-1, keepdims=True)
        outs.append(_dot(p.astype(BF16), v_ref[0, 0, :, sl].astype(BF16)))
    o = jnp.concatenate(outs, axis=1)
    y = _dot(o.astype(BF16), wo_ref[...])
    o_ref[0] = _layer_norm(alpha * x + y, g_ref[...], b_ref[...])


def memory_block(x, mem_kv, layer, w_mq, w_mo, g, b, alpha, tm):
    bsz, t, d = x.shape
    return pl.pallas_call(
        functools.partial(_mem_kernel, alpha=alpha),
        grid=(bsz, t // tm),
        in_specs=[pl.BlockSpec((1, tm, d), lambda bi, i: (bi, i, 0)),
                  pl.BlockSpec((d, MEM_W), lambda bi, i: (0, 0)),
                  pl.BlockSpec((1, 1, N_MEM, MEM_W), lambda bi, i: (layer, bi, 0, 0)),
                  pl.BlockSpec((1, 1, N_MEM, MEM_W), lambda bi, i: (layer, bi, 0, 1)),
                  pl.BlockSpec((MEM_W, d), lambda bi, i: (0, 0)),
                  pl.BlockSpec((1, d), lambda bi, i: (0, 0)),
                  pl.BlockSpec((1, d), lambda bi, i: (0, 0))],
        out_specs=pl.BlockSpec((1, tm, d), lambda bi, i: (bi, i, 0)),
        out_shape=jax.ShapeDtypeStruct((bsz, t, d), F32),
        compiler_params=_params("arbitrary", "arbitrary"),
        name="memory_block",
    )(x, w_mq, mem_kv, mem_kv, w_mo, g.reshape(1, d), b.reshape(1, d))


def _router_kernel(x_ref, wr_ref, br_ref, o_ref):
    tm = x_ref.shape[0]
    lane = _iota((tm, LANES), 1)
    logits = _dot3(x_ref[...], wr_ref[...]) + br_ref[...]
    is_grp = lane < N_GROUPS
    gl = jnp.where(is_grp, logits, NEG)
    gmax = jnp.max(gl, axis=-1, keepdims=True)
    grp = jnp.min(jnp.where(is_grp & (gl == gmax), lane, LANES), axis=-1, keepdims=True)
    p_grp = 1.0 / jnp.sum(jnp.where(is_grp, jnp.exp(gl - gmax), 0.0), axis=-1, keepdims=True)
    in_grp = (lane >= N_GROUPS) & (lane < N_GROUPS + N_EXPERTS) & (
        ((lane - N_GROUPS) >> 2) == grp)
    el = jnp.where(in_grp, logits, NEG)
    v1 = jnp.max(el, axis=-1, keepdims=True)
    i1 = jnp.min(jnp.where(in_grp & (el == v1), lane, LANES), axis=-1, keepdims=True)
    rest = in_grp & (lane != i1)
    el2 = jnp.where(rest, logits, NEG)
    v2 = jnp.max(el2, axis=-1, keepdims=True)
    i2 = jnp.min(jnp.where(rest & (el2 == v2), lane, LANES), axis=-1, keepdims=True)
    e2 = jnp.exp(v2 - v1)
    w1 = p_grp / (1.0 + e2)
    w2 = p_grp * e2 / (1.0 + e2)
    gate = jnp.where(lane == i1, w1, 0.0) + jnp.where(lane == i2, w2, 0.0)
    o_ref[...] = jnp.where(lane == 0, grp.astype(F32), gate)


def moe_route(x, w_router, b_router, tm):
    n, d = x.shape
    return pl.pallas_call(
        _router_kernel,
        grid=(n // tm,),
        in_specs=[pl.BlockSpec((tm, d), lambda i: (i, 0)),
                  pl.BlockSpec((d, LANES), lambda i: (0, 0)),
                  pl.BlockSpec((1, LANES), lambda i: (0, 0))],
        out_specs=pl.BlockSpec((tm, LANES), lambda i: (i, 0)),
        out_shape=jax.ShapeDtypeStruct((n, LANES), F32),
        compiler_params=_params("arbitrary"),
        name="moe_route",
    )(x, w_router, b_router)


def _moe_kernel(fetch_ref, x_ref, gate_ref, weg_ref, weu_ref, wed_ref, g_ref, b_ref, o_ref, xb_ref, acc_ref,
                *, alpha):
    e = pl.program_id(1)
    tm = x_ref.shape[0]
    lane = _iota((tm, LANES), 1)

    @pl.when(e == 0)
    def _():
        xb_ref[...] = x_ref[...].astype(BF16)
        acc_ref[...] = jnp.zeros_like(acc_ref)

    gate_e = jnp.sum(jnp.where(lane == e + N_GROUPS, gate_ref[...], 0.0), axis=-1, keepdims=True)

    @pl.when(jnp.max(gate_e) > 0.0)
    def _():
        xb = xb_ref[...]
        hid = jax.nn.silu(_dot(xb, weg_ref[0])) * _dot(xb, weu_ref[0])
        acc_ref[...] += _dot((hid * gate_e).astype(BF16), wed_ref[0])

    @pl.when(e == N_EXPERTS - 1)
    def _():
        o_ref[...] = _layer_norm(alpha * x_ref[...] + acc_ref[...], g_ref[...], b_ref[...])


def moe_block(x, w_router, b_router, w_eg, w_eu, w_ed, g, b, alpha, tm):
    n, d = x.shape
    f = w_eg.shape[-1]
    n_tiles = n // tm
    routed = moe_route(x, w_router, b_router, tm)
    sort_tokens = n_tiles > 1
    if sort_tokens:
        order = jnp.argsort(routed[:, 0].astype(I32), stable=True)
        xs = jnp.take(x, order, axis=0, mode="clip")
        gs = jnp.take(routed, order, axis=0, mode="clip")
    else:
        xs, gs = x, routed
    used = jnp.any(gs[:, N_GROUPS:N_GROUPS + N_EXPERTS].reshape(n_tiles, tm, N_EXPERTS) > 0.0, axis=1).reshape(-1)
    steps = jnp.arange(n_tiles * N_EXPERTS, dtype=I32)
    fetch = (lax.cummax(jnp.where(used, steps, 0), axis=0) % N_EXPERTS).astype(I32)
    weight = lambda shape: pl.BlockSpec(shape, lambda i, e, fetch_ref: (fetch_ref[i * N_EXPERTS + e], 0, 0))
    ys = pl.pallas_call(
        functools.partial(_moe_kernel, alpha=alpha),
        grid_spec=pltpu.PrefetchScalarGridSpec(
            num_scalar_prefetch=1,
            grid=(n_tiles, N_EXPERTS),
            in_specs=[pl.BlockSpec((tm, d), lambda i, e, *_: (i, 0)),
                      pl.BlockSpec((tm, LANES), lambda i, e, *_: (i, 0)),
                      weight((1, d, f)), weight((1, d, f)), weight((1, f, d)),
                      pl.BlockSpec((1, d), lambda i, e, *_: (0, 0)),
                      pl.BlockSpec((1, d), lambda i, e, *_: (0, 0))],
            out_specs=pl.BlockSpec((tm, d), lambda i, e, *_: (i, 0)),
            scratch_shapes=[pltpu.VMEM((tm, d), BF16), pltpu.VMEM((tm, d), F32)]),
        out_shape=jax.ShapeDtypeStruct((n, d), F32),
        compiler_params=_params("arbitrary", "arbitrary"),
        name="moe_block",
    )(fetch, xs, gs, w_eg, w_eu, w_ed, g.reshape(1, d), b.reshape(1, d))
    if not sort_tokens:
        return ys
    inverse = jnp.zeros((n,), I32).at[order].set(jnp.arange(n, dtype=I32))
    return jnp.take(ys, inverse, axis=0, mode="clip")


TQ = 128
BK = 128


def _log_sigmoid_pair(z):
    l1p = jnp.log1p(jnp.exp(-jnp.abs(z)))
    return jnp.minimum(z, 0.0) - l1p, jnp.minimum(-z, 0.0) - l1p


def _strict_upper_ones(n):
    return jnp.where(_iota((n, n), 0) > _iota((n, n), 1), 1.0, 0.0).astype(BF16)


def _strict_lower_ones(n):
    return jnp.where(_iota((n, n), 0) < _iota((n, n), 1), 1.0, 0.0).astype(BF16)


def _sb_kernel(q_ref, k_ref, v_ref, o_ref):
    qi = pl.program_id(1)
    r_rows = N_Q_HEADS * TQ
    qs = _stack_q(q_ref[0], HEAD_DIM ** -0.5).astype(BF16)
    upper = _strict_upper_ones(BK)
    strictly_causal = _iota((TQ, BK), 1) < _iota((TQ, BK), 0)

    def step(kb, acc, run, mask):
        ks = pl.multiple_of(kb * BK, BK)
        k = k_ref[0, pl.ds(ks, BK), :].astype(BF16)
        v = v_ref[0, pl.ds(ks, BK), :].astype(BF16)
        z = _dot_nt(qs, k)
        lss, lns = [], []
        for h in range(N_Q_HEADS):
            ls, ln = _log_sigmoid_pair(z[h * TQ:(h + 1) * TQ])
            lss.append(ls)
            lns.append(ln if mask is None else jnp.where(mask, ln, 0.0))
        ln = jnp.concatenate(lns, axis=0)
        between = _dot_x_exact(ln, upper)
        weights = []
        for h in range(N_Q_HEADS):
            sl = slice(h * TQ, (h + 1) * TQ)
            a = jnp.exp(lss[h] + between[sl] + run[sl])
            weights.append((a if mask is None else jnp.where(mask, a, 0.0)).astype(BF16))
        acc = acc + _dot(jnp.concatenate(weights, axis=0), v)
        return acc, run + between[:, 0:1] + ln[:, 0:1]

    acc, run = step(qi, jnp.zeros((r_rows, LANES), F32), jnp.zeros((r_rows, 1), F32), strictly_causal)

    def more(state):
        kb, _, run = state
        return jnp.logical_and(kb >= 0, jnp.max(run) > F32_EXP_ZERO)

    def body(state):
        kb, acc, run = state
        acc, run = step(kb, acc, run, None)
        return kb - 1, acc, run

    _, acc, _ = lax.while_loop(more, body, (qi - 1, acc, run))
    _store_heads(o_ref, acc, TQ)


def _h_spec(rows, tile, width=1):
    if rows is None:
        return lambda t: pl.BlockSpec((1, t, width * LANES), lambda b, i: (b, 0, tile // width))
    return pl.BlockSpec((1, rows, width * LANES), lambda b, i: (b, i, tile // width))


def sb_attention(h):
    bsz, t, _ = h.shape
    return pl.pallas_call(
        _sb_kernel,
        grid=(bsz, t // TQ),
        in_specs=[_h_spec(TQ, T_SBQ, 4), _h_spec(None, T_SB_K)(t), _h_spec(None, T_SB_V)(t)],
        out_specs=pl.BlockSpec((1, TQ, MIX_W), lambda b, i: (b, i, 0)),
        out_shape=jax.ShapeDtypeStruct((bsz, t, MIX_W), F32),
        compiler_params=_params("arbitrary", "arbitrary"),
        name="sb_attention",
    )(h, h, h)


def _fox_prep_kernel(m_ref, bf_ref, logf_ref, c_ref, carry_ref):
    i = pl.program_id(1)
    tc = m_ref.shape[1]

    @pl.when(i == 0)
    def _():
        carry_ref[...] = jnp.zeros_like(carry_ref)

    logf, _ = _log_sigmoid_pair(m_ref[0] + bf_ref[...])
    logf_ref[0] = logf
    incl = jnp.where(_iota((tc, tc), 0) >= _iota((tc, tc), 1), 1.0, 0.0).astype(BF16)
    c = _dot_exact_x(incl, logf) + carry_ref[...]
    c_ref[0] = c
    carry_ref[...] = c[tc - 1:tc, :]


def fox_prep(h, fox_b_f, tc=256):
    bsz, t, _ = h.shape
    bias = jnp.zeros((1, LANES), F32).at[0, MISC_FOXF:MISC_FOXF + N_Q_HEADS].set(fox_b_f)
    return pl.pallas_call(
        _fox_prep_kernel,
        grid=(bsz, t // tc),
        in_specs=[pl.BlockSpec((1, tc, LANES), lambda b, i: (b, i, T_MISC)),
                  pl.BlockSpec((1, LANES), lambda b, i: (0, 0))],
        out_specs=[pl.BlockSpec((1, tc, LANES), lambda b, i: (b, i, 0)),
                   pl.BlockSpec((1, tc, LANES), lambda b, i: (b, i, 0))],
        out_shape=[jax.ShapeDtypeStruct((bsz, t, LANES), F32),
                   jax.ShapeDtypeStruct((bsz, t, LANES), F32)],
        scratch_shapes=[pltpu.VMEM((1, LANES), F32)],
        compiler_params=_params("arbitrary", "arbitrary"),
        name="fox_prep",
    )(h, bias)


def _head_columns(x, lane0, stride=1):
    return jnp.concatenate([x[:, lane0 + stride * h:lane0 + stride * h + 1] for h in range(N_Q_HEADS)], axis=0)


def _softmax_init(rows):
    return (jnp.full((rows, 1), NEG, F32), jnp.zeros((rows, 1), F32), jnp.zeros((rows, LANES), F32))


def _transpose_heads(x):
    if x.shape[0] == LANES:
        return jnp.concatenate([x[:, h * TQ:(h + 1) * TQ].T for h in range(N_Q_HEADS)], axis=0)
    return jnp.concatenate([x[h * TQ:(h + 1) * TQ].T for h in range(N_Q_HEADS)], axis=1)


def _softmax_init_t():
    cols = N_Q_HEADS * TQ
    return (jnp.full((1, cols), NEG, F32), jnp.zeros((1, cols), F32), jnp.zeros((LANES, cols), F32))


LOG2E = 1.4426950408889634


def _softmax_step_t(zt, biases, v, m, l, acc):
    ps, ms, ls, alphas = [], [], [], []
    for h in range(N_Q_HEADS):
        sl = slice(h * TQ, (h + 1) * TQ)
        z = zt[:, sl]
        if biases is not None and biases[h] is not None:
            z = z + biases[h]
        m_new = jnp.maximum(m[:, sl], jnp.max(z, axis=0, keepdims=True))
        p = jnp.exp2(z - m_new)
        alpha = jnp.exp2(m[:, sl] - m_new)
        ls.append(alpha * l[:, sl] + jnp.sum(p, axis=0, keepdims=True))
        ms.append(m_new)
        alphas.append(alpha)
        ps.append(p.astype(BF16))
    cat = lambda xs: jnp.concatenate(xs, axis=1)
    pv = _dot(v.T.astype(BF16), cat(ps))
    return cat(ms), cat(ls), cat(alphas) * acc + pv


def _fox_kernel(q_ref, k_ref, v_ref, c_ref, o_ref):
    qi = pl.program_id(1)
    qst = _transpose_heads(_stack_q(q_ref[0], HEAD_DIM ** -0.5 * LOG2E)).astype(BF16)

    def step(kb, carry, mask_bias):
        ks = pl.multiple_of(kb * BK, BK)
        k = k_ref[0, pl.ds(ks, BK), :].astype(BF16)
        ck = c_ref[0, pl.ds(ks, BK), :] * (-LOG2E)
        biases = [ck[:, h:h + 1] if mask_bias is None else ck[:, h:h + 1] + mask_bias for h in range(N_Q_HEADS)]
        return _softmax_step_t(_dot(k, qst), biases, v_ref[0, pl.ds(ks, BK), :], *carry)

    carry = lax.fori_loop(0, qi, lambda kb, c: step(kb, c, None), _softmax_init_t())
    causal = jnp.where(_iota((BK, TQ), 0) <= _iota((BK, TQ), 1), 0.0, NEG)
    _, l, acc = step(qi, carry, causal)
    _store_heads(o_ref, _transpose_heads(acc / jnp.maximum(l, 1e-30)), TQ)


def fox_attention(h, c_tok):
    bsz, t, _ = h.shape
    return pl.pallas_call(
        _fox_kernel,
        grid=(bsz, t // TQ),
        in_specs=[_h_spec(TQ, T_FOXQ, 4), _h_spec(None, T_FOX_K)(t), _h_spec(None, T_FOX_V)(t),
                  pl.BlockSpec((1, t, LANES), lambda b, i: (b, 0, 0))],
        out_specs=pl.BlockSpec((1, TQ, MIX_W), lambda b, i: (b, i, 0)),
        out_shape=jax.ShapeDtypeStruct((bsz, t, MIX_W), F32),
        compiler_params=_params("arbitrary", "arbitrary"),
        name="fox_attention",
    )(h, h, h, c_tok)


def _nsa_cmp_kernel(k_ref, v_ref, pek_ref, pev_ref, wk_ref, wv_ref, ok_ref, ov_ref, *, nblk):
    for src, pe, w, dst in ((k_ref, pek_ref, wk_ref, ok_ref), (v_ref, pev_ref, wv_ref, ov_ref)):
        rows = src[0].reshape(nblk, NSA_BLOCK, LANES) + pe[...][None]
        mean = jnp.sum(rows, axis=1) * (1.0 / NSA_BLOCK)
        dst[0] = jnp.zeros(dst.shape[1:], F32)
        dst[0, 0:nblk, :] = _dot3(mean, w[...])


def nsa_compress(h, pe_k2, pe_v2, w_ck2, w_cv2):
    bsz, t, _ = h.shape
    nblk = t // NSA_BLOCK
    nblk_pad = -(-nblk // LANES) * LANES
    full = lambda shape: pl.BlockSpec(shape, lambda b: (0,) * len(shape))
    return pl.pallas_call(
        functools.partial(_nsa_cmp_kernel, nblk=nblk),
        grid=(bsz,),
        in_specs=[pl.BlockSpec((1, t, LANES), lambda b: (b, 0, T_NSA_KC)),
                  pl.BlockSpec((1, t, LANES), lambda b: (b, 0, T_NSA_VC)),
                  full((NSA_BLOCK, LANES)), full((NSA_BLOCK, LANES)),
                  full((LANES, LANES)), full((LANES, LANES))],
        out_specs=[pl.BlockSpec((1, nblk_pad, LANES), lambda b: (b, 0, 0))] * 2,
        out_shape=[jax.ShapeDtypeStruct((bsz, nblk_pad, LANES), F32)] * 2,
        compiler_params=_params("arbitrary"),
        name="nsa_compress",
    )(h, h, pe_k2, pe_v2, w_ck2, w_cv2)


def _topk_block_mask(imp_t, nblk, k_sel):
    n_idx = _iota(imp_t.shape, 0)
    rank = jnp.zeros(imp_t.shape, F32)
    for m in range(nblk):
        row = imp_t[m:m + 1, :]
        before = jnp.where(n_idx > m, 1.0, 0.0)
        rank = rank + jnp.where(row > imp_t, 1.0, jnp.where(row == imp_t, before, 0.0))
    return jnp.where(rank < k_sel, jnp.where(imp_t > -jnp.inf, 1.0, 0.0), 0.0)


def _nsa_kernel(q_ref, ck_ref, cv_ref, ks_ref, vs_ref, kw_ref, vw_ref, g_ref, o_ref, *, nblk, k_sel):
    qi = pl.program_id(1)
    r_rows = N_Q_HEADS * TQ
    half = GQA_GROUP * TQ
    qf = _stack_q(q_ref[0], HEAD_DIM ** -0.5)
    qpos = qi * TQ + (_iota((r_rows, BK), 0) & (TQ - 1))
    col = _iota((r_rows, BK), 1)

    n_ctile = ck_ref.shape[1] // LANES
    sel_tiles = []
    state = _softmax_init(r_rows)
    zc_tiles, vis_tiles = [], []
    for c in range(n_ctile):
        blk = c * LANES + col
        vis = ((blk + 1) * NSA_BLOCK - 1 <= qpos) & (blk < nblk)
        zc = _dot3_nt(qf, ck_ref[0, c * LANES:(c + 1) * LANES, :])
        zc_tiles.append(zc)
        vis_tiles.append(vis)
        m_run = jnp.maximum(state[0], jnp.max(jnp.where(vis, zc, NEG), axis=-1, keepdims=True))
        state = (m_run,) + state[1:]
    m_c = state[0]
    e_tiles = [jnp.where(vis, jnp.exp(zc - m_c), 0.0) for zc, vis in zip(zc_tiles, vis_tiles)]
    denom = e_tiles[0].sum(axis=-1, keepdims=True)
    for e in e_tiles[1:]:
        denom = denom + e.sum(axis=-1, keepdims=True)
    denom = jnp.maximum(denom, 1e-30)
    o_cmp = jnp.zeros((r_rows, LANES), F32)
    tq_pos = qi * TQ + _iota((TQ, LANES), 0)
    tq_col = _iota((TQ, LANES), 1)
    for c in range(n_ctile):
        pc = e_tiles[c] / denom
        o_cmp = o_cmp + _dot(pc.astype(BF16), cv_ref[0, c * LANES:(c + 1) * LANES, :].astype(BF16))
        blk = c * LANES + tq_col
        vis_t = ((blk + 1) * NSA_BLOCK - 1 <= tq_pos) & (blk < nblk)
        forced = (blk == (tq_pos >> 6)) | (blk == 0)
        per_group = []
        for g in range(N_KV_HEADS):
            imp = pc[g * half:g * half + TQ]
            for r in range(1, GQA_GROUP):
                imp = imp + pc[g * half + r * TQ:g * half + (r + 1) * TQ]
            imp = jnp.where(forced, jnp.inf, jnp.where(vis_t, imp, -jnp.inf))
            per_group.append(imp)
        sel_tiles.append(per_group)

    sel = []
    for g in range(N_KV_HEADS):
        imp_t = jnp.concatenate([sel_tiles[c][g].T for c in range(n_ctile)], axis=0)[0:nblk]
        chosen = _topk_block_mask(imp_t, nblk, k_sel)
        pad = n_ctile * LANES - nblk
        if pad:
            chosen = jnp.concatenate([chosen, jnp.zeros((pad, TQ), F32)], axis=0)
        sel.append([chosen[c * LANES:(c + 1) * LANES].astype(BF16) for c in range(n_ctile)])

    qst = _transpose_heads(qf * LOG2E).astype(BF16)
    blocks_per_tile = BK // NSA_BLOCK
    e_key = _iota((BK, LANES), 0) >> 6
    e_blk = _iota((BK, LANES), 1)
    kq_key = _iota((BK, TQ), 0)
    kq_pos = qi * TQ + _iota((BK, TQ), 1)

    def sel_body(kb, carry):
        ks = pl.multiple_of(kb * BK, BK)
        k = ks_ref[0, pl.ds(ks, BK), :].astype(BF16)
        causal = (ks + kq_key) <= kq_pos
        masks = []
        for g in range(N_KV_HEADS):
            hit = None
            for c in range(n_ctile):
                expand = jnp.where(e_blk + c * LANES == kb * blocks_per_tile + e_key, 1.0, 0.0).astype(BF16)
                part = _dot(expand, sel[g][c])
                hit = part if hit is None else hit + part
            masks += [jnp.where(jnp.where(causal, hit, 0.0) > 0.5, 0.0, NEG)] * GQA_GROUP
        return _softmax_step_t(_dot(k, qst), masks, vs_ref[0, pl.ds(ks, BK), :], *carry)

    _, l_s, acc_s = lax.fori_loop(0, qi + 1, sel_body, _softmax_init_t())
    o_sel = _transpose_heads(acc_s / jnp.maximum(l_s, 1e-30))

    def win_body(kb, carry):
        ks = pl.multiple_of(kb * BK, BK)
        k = kw_ref[0, pl.ds(ks, BK), :].astype(BF16)
        kpos = ks + kq_key
        band = jnp.where(kpos <= kq_pos, jnp.where(kpos >= kq_pos - NSA_WINDOW, 0.0, NEG), NEG)
        return _softmax_step_t(_dot(k, qst), [band] * N_Q_HEADS, vw_ref[0, pl.ds(ks, BK), :], *carry)

    first = jnp.maximum(qi - NSA_WINDOW // BK, 0)
    _, l_w, acc_w = lax.fori_loop(first, qi + 1, win_body, _softmax_init_t())
    o_win = _transpose_heads(acc_w / jnp.maximum(l_w, 1e-30))

    gates = jax.nn.sigmoid(g_ref[0])
    o = (_head_columns(gates, MISC_NSAG, 3) * o_cmp + _head_columns(gates, MISC_NSAG + 1, 3) * o_sel
         + _head_columns(gates, MISC_NSAG + 2, 3) * o_win)
    _store_heads(o_ref, o, TQ)


def nsa_attention(h, cmp_k, cmp_v):
    bsz, t, _ = h.shape
    nblk = t // NSA_BLOCK
    nblk_pad = cmp_k.shape[1]
    return pl.pallas_call(
        functools.partial(_nsa_kernel, nblk=nblk, k_sel=min(NSA_TOPK, nblk)),
        grid=(bsz, t // TQ),
        in_specs=[_h_spec(TQ, T_NSAQ, 4),
                  pl.BlockSpec((1, nblk_pad, LANES), lambda b, i: (b, 0, 0)),
                  pl.BlockSpec((1, nblk_pad, LANES), lambda b, i: (b, 0, 0)),
                  _h_spec(None, T_NSA_KS)(t), _h_spec(None, T_NSA_VS)(t),
                  _h_spec(None, T_NSA_KW)(t), _h_spec(None, T_NSA_VW)(t),
                  _h_spec(TQ, T_MISC)],
        out_specs=pl.BlockSpec((1, TQ, MIX_W), lambda b, i: (b, i, 0)),
        out_shape=jax.ShapeDtypeStruct((bsz, t, MIX_W), F32),
        compiler_params=_params("arbitrary", "arbitrary"),
        name="nsa_attention",
    )(h, cmp_k, cmp_v, h, h, h, h, h)


def _order_key(x):
    bits = lax.bitcast_convert_type(x + 0.0, I32)
    return bits ^ ((bits >> 31) & 0x7FFFFFFF)


INT_MIN = -2 ** 31
NEG_INF_KEY = -2139095041


def _kth_largest_key(count_ge, shape, k):
    def bit_body(it, kappa):
        bit = lax.shift_left(jnp.int32(1), 31 - it)
        cand = kappa | bit
        cnt = count_ge(cand ^ INT_MIN)
        return jnp.where(cnt >= k, cand, kappa)

    kappa = lax.fori_loop(0, 32, bit_body, jnp.zeros(shape, I32))
    return kappa ^ INT_MIN


def _dsa_kernel(q_ref, iq_ref, ik_ref, w_ref, k_ref, v_ref, o_ref, key_ref, *, topk):
    qi = pl.program_id(1)
    qst = _transpose_heads(_stack_q(q_ref[0], HEAD_DIM ** -0.5 * LOG2E)).astype(BF16)
    iq = iq_ref[0]
    low = _iota((TQ, LANES), 1) < HEAD_DIM
    iq_t = jnp.concatenate(
        [jnp.where(low if h % 2 == 0 else jnp.logical_not(low), iq[:, (h // 2) * LANES:(h // 2 + 1) * LANES], 0.0).T
         for h in range(IDX_HEADS)], axis=1)
    w_t = w_ref[0].T
    key_pos = _iota((BK, TQ), 0)
    q_pos = qi * TQ + _iota((BK, TQ), 1)
    n_tiles = qi + 1

    def score_body(kb, _):
        ks = pl.multiple_of(kb * BK, BK)
        sc = jnp.maximum(_dot3(ik_ref[0, pl.ds(ks, BK), :], iq_t), 0.0)
        tot = w_t[MISC_IDXW:MISC_IDXW + 1, :] * sc[:, 0:TQ]
        for h in range(1, IDX_HEADS):
            tot = tot + w_t[MISC_IDXW + h:MISC_IDXW + h + 1, :] * sc[:, h * TQ:(h + 1) * TQ]
        tot = jnp.where(ks + key_pos <= q_pos, tot, -jnp.inf)
        key_ref[kb] = _order_key(tot)
        return 0

    lax.fori_loop(0, n_tiles, score_body, 0)

    def count_where(pred):
        def body(kb, acc):
            return acc + jnp.where(pred(key_ref[kb]), 1.0, 0.0)
        return jnp.sum(lax.fori_loop(0, n_tiles, body, jnp.zeros((BK, TQ), F32)), axis=0, keepdims=True)

    kappa = _kth_largest_key(lambda c: count_where(lambda key: key >= c), (1, TQ), float(topk))
    need = float(topk) - count_where(lambda key: key > kappa)
    earlier = jnp.where(_iota((BK, BK), 1) < _iota((BK, BK), 0), 1.0, 0.0).astype(BF16)

    def attn_body(kb, carry):
        ties_before, m, l, acc = carry
        ks = pl.multiple_of(kb * BK, BK)
        key = key_ref[kb]
        tie = jnp.where(key == kappa, 1.0, 0.0)
        rank = _dot(earlier, tie.astype(BF16)) + ties_before
        chosen = jnp.where(key > kappa, 1.0, jnp.where(rank < need, tie, 0.0))
        dropped = jnp.where(key > NEG_INF_KEY, jnp.where(chosen > 0.5, 0.0, NEG), NEG)
        ties_before = ties_before + jnp.sum(tie, axis=0, keepdims=True)
        k = k_ref[0, pl.ds(ks, BK), :].astype(BF16)
        return (ties_before,) + _softmax_step_t(_dot(k, qst), [dropped] * N_Q_HEADS,
                                                v_ref[0, pl.ds(ks, BK), :], m, l, acc)

    _, _, l, acc = lax.fori_loop(0, n_tiles, attn_body, (jnp.zeros((1, TQ), F32),) + _softmax_init_t())
    _store_heads(o_ref, _transpose_heads(acc / jnp.maximum(l, 1e-30)), TQ)


def dsa_attention(h):
    bsz, t, _ = h.shape
    return pl.pallas_call(
        functools.partial(_dsa_kernel, topk=min(DSA_TOPK_MAX, t // 4)),
        grid=(bsz, t // TQ),
        in_specs=[_h_spec(TQ, T_DSAQ, 4), _h_spec(TQ, T_IDXQ, 2), _h_spec(None, T_IDXK)(t),
                  _h_spec(TQ, T_MISC), _h_spec(None, T_DSA_K)(t), _h_spec(None, T_DSA_V)(t)],
        out_specs=pl.BlockSpec((1, TQ, MIX_W), lambda b, i: (b, i, 0)),
        out_shape=jax.ShapeDtypeStruct((bsz, t, MIX_W), F32),
        scratch_shapes=[pltpu.VMEM((t // BK, TQ, BK), I32)],
        compiler_params=_params("arbitrary", "arbitrary"),
        name="dsa_attention",
    )(h, h, h, h, h, h)


PAGES_PER_STEP = 16


def _page_specs(rows, row_block, layer, n_chunks, reverse, page):
    def spec(p):
        def index(b, c, pt_ref, *_):
            chunk = (n_chunks - 1 - c) if reverse else c
            return (layer, pt_ref[b, chunk * PAGES_PER_STEP + p], row_block, 0)
        return index
    return [pl.BlockSpec((1, 1, rows, page), spec(p)) for p in range(PAGES_PER_STEP)]


def _suffix_sums_pages(xs):
    return _dot_x_exact(jnp.concatenate(xs, axis=0), _strict_upper_ones(LANES))


def _softmax_pages(zs, masks, vts, m, l, acc):
    if masks is not None:
        zs = [jnp.where(mk, z, NEG) for z, mk in zip(zs, masks)]
    top = zs[0]
    for z in zs[1:]:
        top = jnp.maximum(top, z)
    m_new = jnp.maximum(m, jnp.max(top, axis=-1, keepdims=True))
    alpha = jnp.exp(m - m_new)
    ps = [jnp.exp(z - m_new) for z in zs]
    if masks is not None:
        ps = [jnp.where(mk, p, 0.0) for p, mk in zip(ps, masks)]
    total = ps[0]
    for p in ps[1:]:
        total = total + p
    pv = _dot_nt(ps[0].astype(BF16), vts[0])
    for p, vt in zip(ps[1:], vts[1:]):
        pv = pv + _dot_nt(p.astype(BF16), vt)
    return m_new, alpha * l + jnp.sum(total, axis=-1, keepdims=True), alpha * acc + pv


def _row_of(ref, b):
    return ref[pl.ds(b, 1), :]


def _diag_column(row, lane0, stride):
    wide = jnp.broadcast_to(row, (N_Q_HEADS, LANES))
    pick = _iota((N_Q_HEADS, LANES), 1) == lane0 + stride * _iota((N_Q_HEADS, LANES), 0)
    return jnp.sum(jnp.where(pick, wide, 0.0), axis=-1, keepdims=True)


def _softmax_token(z, v_row, m, l, acc):
    m_new = jnp.maximum(m, z)
    alpha = jnp.exp(m - m_new)
    p = jnp.exp(z - m_new)
    return m_new, alpha * l + p, alpha * acc + p * v_row


def _sb_dec_kernel(pt_ref, q_ref, *refs):
    pages = refs[:PAGES_PER_STEP]
    o_ref, acc_ref, run_ref = refs[PAGES_PER_STEP:]
    c = pl.program_id(1)

    @pl.when(c == 0)
    def _():
        acc_ref[...] = jnp.zeros_like(acc_ref)
        run_ref[...] = jnp.zeros_like(run_ref)

    qs = (q_ref[0] * HEAD_DIM ** -0.5).astype(BF16)
    lss, lns = [], []
    for p in range(PAGES_PER_STEP):
        ls, ln = _log_sigmoid_pair(_dot(qs, pages[p][0, 0, 0:KV_W, :].astype(BF16)))
        lss.append(ls)
        lns.append(ln)
    within = _suffix_sums_pages(lns)
    later = [within[p * N_Q_HEADS:(p + 1) * N_Q_HEADS] for p in range(PAGES_PER_STEP)]
    totals = [later[p][:, 0:1] + lns[p][:, 0:1] for p in range(PAGES_PER_STEP)]
    acc = acc_ref[...]
    run = run_ref[:, 0:1]
    for p in reversed(range(PAGES_PER_STEP)):
        a = jnp.exp(lss[p] + later[p] + run)
        acc = acc + _dot_nt(a.astype(BF16), pages[p][0, 0, KV_W:2 * KV_W, :].astype(BF16))
        run = run + totals[p]
    acc_ref[...] = acc
    run_ref[...] = jnp.broadcast_to(run, run_ref.shape)

    @pl.when(c == pl.num_programs(1) - 1)
    def _():
        o_ref[0] = acc


def _decode_call(kernel, name, page_table, n_chunks, in_specs, args, out_specs, out_shape, scratch, n_prefetch=1,
                 grid=None, prefetch=None):
    bsz = page_table.shape[0]
    return pl.pallas_call(
        kernel,
        grid_spec=pltpu.PrefetchScalarGridSpec(
            num_scalar_prefetch=n_prefetch,
            grid=grid or (bsz, n_chunks),
            in_specs=in_specs,
            out_specs=out_specs,
            scratch_shapes=scratch),
        out_shape=out_shape,
        compiler_params=_params("arbitrary", "arbitrary"),
        name=name,
    )(*(prefetch or (page_table,)), *args)


def _q_spec():
    return pl.BlockSpec((1, N_Q_HEADS, LANES), lambda b, c, *_: (b, 0, 0))


def _hs_spec(bsz, tile):
    return pl.BlockSpec((bsz, LANES), lambda b, c, *_: (0, tile))


def _head_out(bsz):
    return (pl.BlockSpec((1, N_Q_HEADS, LANES), lambda b, c, *_: (b, 0, 0)),
            jax.ShapeDtypeStruct((bsz, N_Q_HEADS, LANES), F32))


def sb_decode(q, cache, layer, page_table):
    bsz, n_pages = page_table.shape
    n_chunks = n_pages // PAGES_PER_STEP
    page = cache.shape[3]
    out_spec, out_shape = _head_out(bsz)
    return _decode_call(
        _sb_dec_kernel, "sb_decode", page_table, n_chunks,
        [_q_spec()] + _page_specs(2 * KV_W, 0, layer, n_chunks, True, page),
        [q] + [cache] * PAGES_PER_STEP, out_spec, out_shape,
        [pltpu.VMEM((N_Q_HEADS, LANES), F32), pltpu.VMEM((N_Q_HEADS, LANES), F32)])


def _fox_dec_kernel(pt_ref, q_ref, kn_ref, vn_ref, misc_ref, bf_ref, *refs):
    pages = refs[:PAGES_PER_STEP]
    lf_pages = refs[PAGES_PER_STEP:2 * PAGES_PER_STEP]
    o_ref, lf_out_ref, m_ref, l_ref, acc_ref, run_ref = refs[2 * PAGES_PER_STEP:]
    b = pl.program_id(0)
    c = pl.program_id(1)
    qf = q_ref[0] * HEAD_DIM ** -0.5

    @pl.when(c == 0)
    def _():
        logf_row, _ = _log_sigmoid_pair(_row_of(misc_ref, b) + bf_ref[...])
        lf_out_ref[0] = logf_row
        z_new = jnp.sum(qf * _row_of(kn_ref, b), axis=-1, keepdims=True)
        m_ref[...] = jnp.broadcast_to(z_new, m_ref.shape)
        l_ref[...] = jnp.ones_like(l_ref)
        acc_ref[...] = jnp.broadcast_to(_row_of(vn_ref, b), acc_ref.shape)
        run_ref[...] = jnp.broadcast_to(_diag_column(logf_row, MISC_FOXF, 1), run_ref.shape)

    qs = qf.astype(BF16)
    run = run_ref[:, 0:1]
    zs = [None] * PAGES_PER_STEP
    lfs = [lf_pages[p][0, 0] for p in range(PAGES_PER_STEP)]
    within = _suffix_sums_pages(lfs)
    for p in reversed(range(PAGES_PER_STEP)):
        later = within[p * N_Q_HEADS:(p + 1) * N_Q_HEADS]
        zs[p] = _dot(qs, pages[p][0, 0, 0:KV_W, :].astype(BF16)) + (later + run)
        run = run + later[:, 0:1] + lfs[p][:, 0:1]
    vts = [pages[p][0, 0, KV_W:2 * KV_W, :].astype(BF16) for p in range(PAGES_PER_STEP)]
    state = _softmax_pages(zs, None, vts, m_ref[:, 0:1], l_ref[:, 0:1], acc_ref[...])
    m_ref[...] = jnp.broadcast_to(state[0], m_ref.shape)
    l_ref[...] = jnp.broadcast_to(state[1], l_ref.shape)
    acc_ref[...] = state[2]
    run_ref[...] = jnp.broadcast_to(run, run_ref.shape)

    @pl.when(c == pl.num_programs(1) - 1)
    def _():
        o_ref[0] = state[2] / state[1]


def fox_decode(q, hs, bias_row, cache, logf_t, layer, page_table):
    bsz, n_pages = page_table.shape
    n_chunks = n_pages // PAGES_PER_STEP
    page = cache.shape[3]
    out_spec, out_shape = _head_out(bsz)
    stat = pltpu.VMEM((N_Q_HEADS, LANES), F32)
    return _decode_call(
        _fox_dec_kernel, "fox_decode", page_table, n_chunks,
        [_q_spec(), _hs_spec(bsz, T_FOX_K), _hs_spec(bsz, T_FOX_V), _hs_spec(bsz, T_MISC),
         pl.BlockSpec((1, LANES), lambda b, c, *_: (0, 0))]
        + _page_specs(2 * KV_W, 0, layer, n_chunks, True, page)
        + _page_specs(N_Q_HEADS, 0, layer, n_chunks, True, page),
        [q, hs, hs, hs, bias_row] + [cache] * PAGES_PER_STEP + [logf_t] * PAGES_PER_STEP,
        [out_spec, pl.BlockSpec((1, 1, LANES), lambda b, c, *_: (b, 0, 0))],
        [out_shape, jax.ShapeDtypeStruct((bsz, 1, LANES), F32)],
        [stat, stat, stat, stat])


def _nsa_dec_cmp_kernel(pt_ref, q_ref, pek_ref, pev_ref, wk_ref, wv_ref, *refs, n_blocks_past, k_sel):
    pages = refs[:PAGES_PER_STEP]
    ocmp_ref, sel_ref, sk_ref, sv_ref = refs[PAGES_PER_STEP:]
    c = pl.program_id(1)
    page = pages[0].shape[3]
    blocks_per_page = page // NSA_BLOCK
    step_blocks = PAGES_PER_STEP * blocks_per_page
    steps_per_tile = LANES // step_blocks

    @pl.when(c == 0)
    def _():
        sk_ref[...] = jnp.zeros_like(sk_ref)
        sv_ref[...] = jnp.zeros_like(sv_ref)

    lane0 = (c % steps_per_tile) * step_blocks
    tok = _iota((page, LANES), 0)
    lane = _iota((page, LANES), 1)
    sum_k = jnp.zeros((KV_W, LANES), F32)
    sum_v = jnp.zeros((KV_W, LANES), F32)
    for p in range(PAGES_PER_STEP):
        place = jnp.where(lane == lane0 + p * blocks_per_page + (tok >> 6), 1.0, 0.0).astype(BF16)
        sum_k = sum_k + _dot_x_exact(pages[p][0, 0, 0:KV_W, :], place)
        sum_v = sum_v + _dot_x_exact(pages[p][0, 0, KV_W:2 * KV_W, :], place)
    sk_ref[c // steps_per_tile] += sum_k
    sv_ref[c // steps_per_tile] += sum_v

    @pl.when(c == pl.num_programs(1) - 1)
    def _():
        n_ptiles = sk_ref.shape[0]
        nb_pad = n_ptiles * LANES
        n_tiles = n_ptiles + 1
        qpos = n_blocks_past * NSA_BLOCK
        qf = q_ref[0] * HEAD_DIM ** -0.5
        pe_mean_k = jnp.mean(pek_ref[...], axis=-1, keepdims=True)
        pe_mean_v = jnp.mean(pev_ref[...], axis=-1, keepdims=True)
        cvs, zcs = [], []
        for t in range(n_ptiles):
            ck_t = _dot3(wk_ref[...], sk_ref[t] * (1.0 / NSA_BLOCK) + pe_mean_k)
            cvs.append(_dot3(wv_ref[...], sv_ref[t] * (1.0 / NSA_BLOCK) + pe_mean_v))
            zcs.append(_dot3(qf, ck_t))
        zc = zcs[0] if n_ptiles == 1 else jnp.concatenate(zcs, axis=1)
        blk = _iota(zc.shape, 1)
        vis = ((blk + 1) * NSA_BLOCK - 1 <= qpos) & (blk < n_blocks_past)
        m = jnp.max(jnp.where(vis, zc, NEG), axis=-1, keepdims=True)
        e = jnp.where(vis, jnp.exp(zc - m), 0.0)
        pc = e / jnp.maximum(jnp.sum(e, axis=-1, keepdims=True), 1e-30)
        o_cmp = _dot_nt(pc[:, 0:LANES].astype(BF16), cvs[0].astype(BF16))
        for t in range(1, n_ptiles):
            o_cmp = o_cmp + _dot_nt(pc[:, t * LANES:(t + 1) * LANES].astype(BF16), cvs[t].astype(BF16))
        ocmp_ref[0] = o_cmp
        n_blocks = n_blocks_past + 1
        rows = []
        for g in range(N_KV_HEADS):
            imp = jnp.sum(pc[g * GQA_GROUP:(g + 1) * GQA_GROUP], axis=0, keepdims=True)
            imp = jnp.where(vis[0:1], imp, -jnp.inf)
            rows.append(jnp.concatenate([imp, jnp.full((1, LANES), -jnp.inf, F32)], axis=1))
        imp_r = jnp.concatenate(rows + [jnp.full((LANES - N_KV_HEADS, n_tiles * LANES), -jnp.inf, F32)], axis=0)
        blk_r = _iota(imp_r.shape, 1)
        forced = (blk_r == qpos // NSA_BLOCK) | (blk_r == 0)
        imp_r = jnp.where(forced, jnp.inf, imp_r)
        imp_c = jnp.concatenate([imp_r[:, t * LANES:(t + 1) * LANES].T for t in range(n_tiles)], axis=0)
        n_col = _iota((n_tiles * LANES, LANES), 0)
        r_lane = _iota((n_tiles * LANES, LANES), 1)
        sel_ref[0] = jnp.zeros(sel_ref.shape[1:], I32)
        for g in range(N_KV_HEADS):
            col = imp_c[:, g:g + 1]
            rank = jnp.zeros((n_tiles * LANES, 1), F32)
            for t in range(n_tiles):
                row = imp_r[g:g + 1, t * LANES:(t + 1) * LANES]
                m_idx = t * LANES + _iota((n_tiles * LANES, LANES), 1)
                ahead = jnp.where(row > col, 1.0, jnp.where(row == col, jnp.where(m_idx < n_col, 1.0, 0.0), 0.0))
                ahead = jnp.where(m_idx < n_blocks, ahead, 0.0)
                rank = rank + jnp.sum(ahead, axis=-1, keepdims=True)
            hit = jnp.where((rank == r_lane.astype(F32)) & (n_col < n_blocks), 1.0, 0.0)
            idx_row = jnp.sum(hit * n_col.astype(F32), axis=0, keepdims=True)
            ok_row = jnp.sum(jnp.where(col > -jnp.inf, hit, 0.0), axis=0, keepdims=True)
            ok_row = jnp.where(_iota((1, LANES), 1) < k_sel, ok_row, 0.0)
            sel_ref[0, g:g + 1, :] = idx_row.astype(I32)
            sel_ref[0, N_KV_HEADS + g:N_KV_HEADS + g + 1, :] = ok_row.astype(I32)


def nsa_decode_compress(q, pe_k_t, pe_v_t, w_ck_t, w_cv_t, cache, layer, page_table):
    bsz, n_pages = page_table.shape
    n_chunks = n_pages // PAGES_PER_STEP
    page = cache.shape[3]
    n_blocks_past = n_pages * page // NSA_BLOCK
    assert n_blocks_past % LANES == 0 and LANES % (PAGES_PER_STEP * page // NSA_BLOCK) == 0
    k_sel = min(NSA_TOPK, n_blocks_past + 1)
    out_spec, out_shape = _head_out(bsz)
    full = lambda shape: pl.BlockSpec(shape, lambda b, c, *_: (0,) * len(shape))
    sums = pltpu.VMEM((n_blocks_past // LANES, KV_W, LANES), F32)
    return _decode_call(
        functools.partial(_nsa_dec_cmp_kernel, n_blocks_past=n_blocks_past, k_sel=k_sel),
        "nsa_decode_compress", page_table, n_chunks,
        [_q_spec(), full((LANES, NSA_BLOCK)), full((LANES, NSA_BLOCK)), full((LANES, LANES)), full((LANES, LANES))]
        + _page_specs(2 * KV_W, 0, layer, n_chunks, False, page),
        [q, pe_k_t, pe_v_t, w_ck_t, w_cv_t] + [cache] * PAGES_PER_STEP,
        [out_spec, pl.BlockSpec((1, N_Q_HEADS, LANES), lambda b, c, *_: (b, 0, 0))],
        [out_shape, jax.ShapeDtypeStruct((bsz, N_Q_HEADS, LANES), I32)],
        [sums, sums])


SELECTED_PER_STEP = 8


def _nsa_dec_sel_kernel(pt_ref, blk_ref, ok_ref, q_ref, kn_ref, vn_ref, *refs, n_blocks_past, k_sel):
    pages = refs[:SELECTED_PER_STEP]
    o_ref, m_ref, l_ref, acc_ref = refs[SELECTED_PER_STEP:]
    b = pl.program_id(0)
    s = pl.program_id(1)
    qf = q_ref[0] * HEAD_DIM ** -0.5

    @pl.when(s == 0)
    def _():
        m_ref[...] = jnp.full(m_ref.shape, NEG, F32)
        l_ref[...] = jnp.zeros_like(l_ref)
        acc_ref[...] = jnp.zeros_like(acc_ref)

    qs = qf.astype(BF16)
    blocks_per_page = pages[0].shape[3] // NSA_BLOCK
    zs, masks, vts = [], [], []
    for j in range(SELECTED_PER_STEP):
        i = s * SELECTED_PER_STEP + j
        blk = blk_ref[b, i]
        valid = (ok_ref[b, i] > 0) & (blk < n_blocks_past)
        z = _dot(qs, pages[j][0, 0, 0:KV_W, :].astype(BF16))
        in_block = (_iota(z.shape, 1) >> 6) == blk % blocks_per_page
        masks.append(jnp.where(in_block, _iota(z.shape, 0) >> 2, -2) == jnp.where(valid, i // k_sel, -1))
        zs.append(z)
        vts.append(pages[j][0, 0, KV_W:2 * KV_W, :].astype(BF16))
    state = _softmax_pages(zs, masks, vts, m_ref[:, 0:1], l_ref[:, 0:1], acc_ref[...])

    @pl.when(s < pl.num_programs(1) - 1)
    def _():
        m_ref[...] = jnp.broadcast_to(state[0], m_ref.shape)
        l_ref[...] = jnp.broadcast_to(state[1], l_ref.shape)
        acc_ref[...] = state[2]

    @pl.when(s == pl.num_programs(1) - 1)
    def _():
        z_new = jnp.sum(qf * _row_of(kn_ref, b), axis=-1, keepdims=True)
        _, l, acc = _softmax_token(z_new, _row_of(vn_ref, b), *state)
        o_ref[0] = acc / l


def nsa_decode_select(q, hs, sel, cache, layer, page_table):
    bsz, n_pages = page_table.shape
    page_len = cache.shape[3]
    n_blocks_past = n_pages * page_len // NSA_BLOCK
    k_sel = min(NSA_TOPK, n_blocks_past + 1)
    blocks_per_page = page_len // NSA_BLOCK
    blk = sel[:, 0:N_KV_HEADS, 0:k_sel].reshape(bsz, N_KV_HEADS * k_sel)
    ok = sel[:, N_KV_HEADS:2 * N_KV_HEADS, 0:k_sel].reshape(bsz, N_KV_HEADS * k_sel)
    out_spec, out_shape = _head_out(bsz)

    assert (N_KV_HEADS * k_sel) % SELECTED_PER_STEP == 0

    def page_index(j):
        def index(b, s, pt_ref, blk_ref, ok_ref):
            page = jnp.minimum(blk_ref[b, s * SELECTED_PER_STEP + j] // blocks_per_page, n_pages - 1)
            return (layer, pt_ref[b, page], 1, 0)
        return index

    stat = pltpu.VMEM((N_Q_HEADS, LANES), F32)
    return _decode_call(
        functools.partial(_nsa_dec_sel_kernel, n_blocks_past=n_blocks_past, k_sel=k_sel),
        "nsa_decode_select", page_table, None,
        [_q_spec(), _hs_spec(bsz, T_NSA_KS), _hs_spec(bsz, T_NSA_VS)]
        + [pl.BlockSpec((1, 1, 2 * KV_W, page_len), page_index(j)) for j in range(SELECTED_PER_STEP)],
        [q, hs, hs] + [cache] * SELECTED_PER_STEP, out_spec, out_shape, [stat, stat, stat],
        n_prefetch=3, grid=(bsz, N_KV_HEADS * k_sel // SELECTED_PER_STEP), prefetch=(page_table, blk, ok))


def _nsa_dec_win_kernel(q_ref, kn_ref, vn_ref, misc_ref, win_ref, ocmp_ref, osel_ref, o_ref):
    b = pl.program_id(0)
    qf = q_ref[0] * HEAD_DIM ** -0.5
    z = _dot(qf.astype(BF16), win_ref[0, 0, 0:KV_W, :].astype(BF16))
    state = _softmax_pages([z], None, [win_ref[0, 0, KV_W:2 * KV_W, :].astype(BF16)], *_softmax_init(N_Q_HEADS))
    z_new = jnp.sum(qf * _row_of(kn_ref, b), axis=-1, keepdims=True)
    _, l, acc = _softmax_token(z_new, _row_of(vn_ref, b), *state)
    gates = jax.nn.sigmoid(_row_of(misc_ref, b))
    o_ref[0] = (_diag_column(gates, MISC_NSAG, 3) * ocmp_ref[0] + _diag_column(gates, MISC_NSAG + 1, 3) * osel_ref[0]
                + _diag_column(gates, MISC_NSAG + 2, 3) * (acc / l))


def nsa_decode_combine(q, hs, win_state, layer, o_cmp, o_sel):
    bsz = q.shape[0]
    w_rows = win_state.shape[3]
    head = pl.BlockSpec((1, N_Q_HEADS, LANES), lambda b: (b, 0, 0))
    tile = lambda t: pl.BlockSpec((bsz, LANES), lambda b: (0, t))
    return pl.pallas_call(
        _nsa_dec_win_kernel,
        grid=(bsz,),
        in_specs=[head, tile(T_NSA_KW), tile(T_NSA_VW), tile(T_MISC),
                  pl.BlockSpec((1, 1, 2 * KV_W, w_rows), lambda b: (layer, b, 0, 0)), head, head],
        out_specs=head,
        out_shape=jax.ShapeDtypeStruct((bsz, N_Q_HEADS, LANES), F32),
        compiler_params=_params("arbitrary"),
        name="nsa_decode_combine",
    )(q, hs, hs, hs, win_state, o_cmp, o_sel)


def _dsa_dec_index_kernel(pt_ref, iq_ref, w_ref, kn_ref, *refs, topk):
    pages = refs[:PAGES_PER_STEP]
    mask_ref, flag_ref, key_ref = refs[PAGES_PER_STEP:]
    b = pl.program_id(0)
    c = pl.program_id(1)
    iq = iq_ref[0]
    w = w_ref[0]
    for p in range(PAGES_PER_STEP):
        sc = jnp.maximum(_dot3(iq, pages[p][0, 0]), 0.0)
        tot = jnp.sum(w * sc, axis=0, keepdims=True)
        key_ref[pl.ds(c * PAGES_PER_STEP + p, 1), :] = _order_key(tot)

    @pl.when(c == pl.num_programs(1) - 1)
    def _():
        k_new = _row_of(kn_ref, b)[:, 0:IDX_DIM]
        sc_new = jnp.maximum(jnp.sum(iq * k_new, axis=-1, keepdims=True), 0.0)
        key_new = _order_key(jnp.sum(w[:, 0:1] * sc_new, axis=0, keepdims=True))
        keys = key_ref[...]

        def total(x):
            return jnp.sum(jnp.sum(x, axis=-1, keepdims=True), axis=0, keepdims=True)

        def count(pred):
            return total(jnp.where(pred(keys), 1.0, 0.0)) + jnp.where(pred(key_new), 1.0, 0.0)

        k = float(topk)
        kappa = _kth_largest_key(lambda cand: count(lambda x: x >= cand), (1, 1), k)
        need = k - count(lambda x: x > kappa)
        tie = jnp.where(keys == kappa, 1.0, 0.0)
        n_rows = keys.shape[0]
        in_row = _dot(tie.astype(BF16), _strict_lower_ones(LANES))
        row_tot = jnp.broadcast_to(jnp.sum(tie, axis=-1, keepdims=True), (n_rows, LANES))
        rows_before = jnp.where(_iota((n_rows, n_rows), 1) < _iota((n_rows, n_rows), 0), 1.0, 0.0).astype(BF16)
        rank = in_row + _dot(rows_before, row_tot.astype(BF16))
        chosen = jnp.where(keys > kappa, 1.0, jnp.where(rank < need, tie, 0.0))
        mask_ref[0] = jnp.where(keys > NEG_INF_KEY, chosen, 0.0)
        new_in = jnp.where(key_new > kappa, 1.0, jnp.where((key_new == kappa) & (total(tie) < need), 1.0, 0.0))
        flag_ref[0] = jnp.broadcast_to(new_in, flag_ref.shape[1:])


def dsa_decode_index(iq, w_rows, hs, cache, layer, page_table):
    bsz, n_pages = page_table.shape
    n_chunks = n_pages // PAGES_PER_STEP
    rows = cache.shape[3]
    assert rows == LANES
    topk = min(DSA_TOPK_MAX, (n_pages * rows + 1) // 4)
    return _decode_call(
        functools.partial(_dsa_dec_index_kernel, topk=topk), "dsa_decode_index", page_table, n_chunks,
        [pl.BlockSpec((1, N_Q_HEADS, IDX_DIM), lambda b, c, *_: (b, 0, 0)), _q_spec(), _hs_spec(bsz, T_IDXK)]
        + _page_specs(IDX_DIM, 0, layer, n_chunks, False, rows),
        [iq, w_rows, hs] + [cache] * PAGES_PER_STEP,
        [pl.BlockSpec((1, n_pages, rows), lambda b, c, *_: (b, 0, 0)),
         pl.BlockSpec((1, N_Q_HEADS, LANES), lambda b, c, *_: (b, 0, 0))],
        [jax.ShapeDtypeStruct((bsz, n_pages, rows), F32), jax.ShapeDtypeStruct((bsz, N_Q_HEADS, LANES), F32)],
        [pltpu.VMEM((n_pages, rows), I32)])


def _dsa_dec_attn_kernel(pt_ref, q_ref, kn_ref, vn_ref, mask_ref, flag_ref, *refs):
    pages = refs[:PAGES_PER_STEP]
    o_ref, m_ref, l_ref, acc_ref = refs[PAGES_PER_STEP:]
    b = pl.program_id(0)
    c = pl.program_id(1)
    qf = q_ref[0] * HEAD_DIM ** -0.5

    @pl.when(c == 0)
    def _():
        m_ref[...] = jnp.full(m_ref.shape, NEG, F32)
        l_ref[...] = jnp.zeros_like(l_ref)
        acc_ref[...] = jnp.zeros_like(acc_ref)

    qs = qf.astype(BF16)
    zs = [_dot(qs, pages[p][0, 0, 0:KV_W, :].astype(BF16)) for p in range(PAGES_PER_STEP)]
    masks = [jnp.broadcast_to(mask_ref[0, p:p + 1, :], (N_Q_HEADS, LANES)) > 0.5 for p in range(PAGES_PER_STEP)]
    vts = [pages[p][0, 0, KV_W:2 * KV_W, :].astype(BF16) for p in range(PAGES_PER_STEP)]
    state = _softmax_pages(zs, masks, vts, m_ref[:, 0:1], l_ref[:, 0:1], acc_ref[...])
    m_ref[...] = jnp.broadcast_to(state[0], m_ref.shape)
    l_ref[...] = jnp.broadcast_to(state[1], l_ref.shape)
    acc_ref[...] = state[2]

    @pl.when(c == pl.num_programs(1) - 1)
    def _():
        z_new = jnp.sum(qf * _row_of(kn_ref, b), axis=-1, keepdims=True)
        z_new = jnp.where(flag_ref[0][:, 0:1] > 0.5, z_new, NEG)
        m, l, acc = state
        m_new = jnp.maximum(m, z_new)
        alpha = jnp.exp(m - m_new)
        p_new = jnp.where(flag_ref[0][:, 0:1] > 0.5, jnp.exp(z_new - m_new), 0.0)
        l = alpha * l + p_new
        acc = alpha * acc + p_new * _row_of(vn_ref, b)
        o_ref[0] = acc / jnp.maximum(l, 1e-30)


def dsa_decode_attention(q, hs, mask, flag, cache, layer, page_table):
    bsz, n_pages = page_table.shape
    n_chunks = n_pages // PAGES_PER_STEP
    rows = cache.shape[3]
    out_spec, out_shape = _head_out(bsz)
    stat = pltpu.VMEM((N_Q_HEADS, LANES), F32)
    return _decode_call(
        _dsa_dec_attn_kernel, "dsa_decode_attention", page_table, n_chunks,
        [_q_spec(), _hs_spec(bsz, T_DSA_K), _hs_spec(bsz, T_DSA_V),
         pl.BlockSpec((1, PAGES_PER_STEP, rows), lambda b, c, *_: (b, c, 0)),
         pl.BlockSpec((1, N_Q_HEADS, LANES), lambda b, c, *_: (b, 0, 0))]
        + _page_specs(2 * KV_W, 0, layer, n_chunks, False, rows),
        [q, hs, hs, mask, flag] + [cache] * PAGES_PER_STEP, out_spec, out_shape, [stat, stat, stat])


def _rope_tables(pos):
    half = HEAD_DIM // 2
    freq = ROPE_THETA ** (-jnp.arange(half, dtype=F32) / half)
    ang = pos.astype(F32)[:, None] * freq[None, :]
    cos = jnp.cos(ang)
    sin = jnp.sin(ang)
    return jnp.tile(cos, (1, 4)), jnp.tile(jnp.concatenate([-sin, sin], axis=1), (1, 2))


def _prepare_weights(p):
    depth = p['w_in'].shape[0]
    w_in = jnp.concatenate([p['w_in'], jnp.zeros((depth, D_MODEL, 1), F32)], axis=-1)
    eye2 = jnp.eye(N_KV_HEADS, dtype=F32)
    w = {
        'w_in': jnp.take(w_in, jnp.asarray(_proj_column_perm()), axis=-1, mode="clip").astype(BF16),
        'w_gate': p['w_gate'].astype(BF16),
        'b_gate': p['b_gate'].reshape(depth, 1, N_BRANCH * D_MODEL),
        'w_br': jnp.take(p['w_br'], jnp.asarray(_y_row_perm()), axis=2, mode="clip").astype(BF16),
        'w_o': p['w_o'].astype(BF16),
        'w_mq': p['w_mq'].astype(BF16),
        'w_mo': p['w_mo'].astype(BF16),
        'w_mkv': p['w_mkv'].astype(BF16),
        'w_router': jnp.concatenate(
            [p['w_rg'], p['w_re'], jnp.zeros((depth, D_MODEL, LANES - N_GROUPS - N_EXPERTS), F32)], axis=-1),
        'b_router': jnp.concatenate(
            [p['b_rg'], p['b_re'], jnp.zeros((depth, LANES - N_GROUPS - N_EXPERTS), F32)], axis=-1)[:, None, :],
        'w_eg': p['w_eg'].astype(BF16),
        'w_eu': p['w_eu'].astype(BF16),
        'w_ed': p['w_ed'].astype(BF16),
        'pe_k': jnp.tile(p['nsa_pe_k'], (1, 1, N_KV_HEADS)),
        'pe_v': jnp.tile(p['nsa_pe_v'], (1, 1, N_KV_HEADS)),
        'w_ck': jnp.einsum('gh,lde->lgdhe', eye2, p['nsa_w_ck']).reshape(depth, LANES, LANES),
        'w_cv': jnp.einsum('gh,lde->lgdhe', eye2, p['nsa_w_cv']).reshape(depth, LANES, LANES),
    }
    for name in ('pe_k', 'pe_v', 'w_ck', 'w_cv'):
        w[name + '_t'] = jnp.swapaxes(w[name], 1, 2)
    for name in ('fox_b_f', 'ln1_g', 'ln1_b', 'ln2_g', 'ln2_b', 'ln3_g', 'ln3_b'):
        w[name] = p[name]
    return w


def _tiles(h, tiles):
    parts = [h[:, :, t * LANES:(t + 1) * LANES] for t in tiles]
    return jnp.stack(parts, axis=2).reshape(h.shape[0], h.shape[1], len(tiles), N_KV_HEADS, HEAD_DIM)


def _new_state(h, logf):
    return {
        'sb_kv': _tiles(h, (T_SB_K, T_SB_V)),
        'nsa_kv': _tiles(h, (T_NSA_KC, T_NSA_VC, T_NSA_KS, T_NSA_VS)),
        'nsa_win_kv': _tiles(h, (T_NSA_KW, T_NSA_VW)),
        'dsa_kv': _tiles(h, (T_DSA_K, T_DSA_V)),
        'dsa_idx_k': h[:, :, T_IDXK * LANES:T_IDXK * LANES + IDX_DIM],
        'fox_kv': _tiles(h, (T_FOX_K, T_FOX_V)),
        'fox_logf': logf[:, :, MISC_FOXF:MISC_FOXF + N_Q_HEADS],
    }


def _dense_tail(x, ys, mem_kv, layer, w, alpha, tm, tm_mem, bsz):
    n = x.shape[0]
    l = layer
    merged = gated_merge(x, ys, w['w_gate'][l], w['b_gate'][l], w['w_br'][l], tm)
    x = matmul_res_ln(merged, w['w_o'][l], x, w['ln1_g'][l], w['ln1_b'][l], alpha, min(tm, 256))
    t = n // bsz
    xm = x.reshape(bsz, t, D_MODEL)
    if t < tm_mem:
        xm = jnp.broadcast_to(xm[:, 0:1], (bsz, tm_mem, D_MODEL))
    x = memory_block(xm, mem_kv, l, w['w_mq'][l], w['w_mo'][l], w['ln2_g'][l], w['ln2_b'][l], alpha,
                     tm_mem)[:, 0:t].reshape(n, D_MODEL)
    return moe_block(x, w['w_router'][l], w['b_router'][l], w['w_eg'][l], w['w_eu'][l], w['w_ed'][l],
                     w['ln3_g'][l], w['ln3_b'][l], alpha, tm)


def _prompt_layer(x, mem_kv, layer, w, rope, alpha, bsz):
    n = x.shape[0]
    t = n // bsz
    l = layer
    tm = min(512, t)
    h = project(x, w['w_in'][l], rope[0], rope[1], min(1024, t)).reshape(bsz, t, PROJ_W)
    logf, c_tok = fox_prep(h, w['fox_b_f'][l])
    cmp_k, cmp_v = nsa_compress(h, w['pe_k'][l], w['pe_v'][l], w['w_ck'][l], w['w_cv'][l])
    ys = [sb_attention(h), nsa_attention(h, cmp_k, cmp_v), dsa_attention(h), fox_attention(h, c_tok)]
    ys = [y.reshape(n, MIX_W) for y in ys]
    x = _dense_tail(x, ys, mem_kv, l, w, alpha, tm, min(256, t), bsz)
    return x, _new_state(h, logf)


def _feature_major_caches(cache_sb_kv, cache_nsa_kv, state_nsa_win_kv, cache_dsa_kv, cache_dsa_idx_k,
                          cache_fox_kv, cache_fox_logf):
    def rows_last(c):
        t = jnp.moveaxis(c, 2, -1)
        return t.reshape(t.shape[0], t.shape[1], -1, t.shape[-1])
    return {
        'sb_kv': rows_last(cache_sb_kv),
        'nsa_kv': rows_last(cache_nsa_kv),
        'nsa_win': rows_last(state_nsa_win_kv),
        'dsa_kv': rows_last(cache_dsa_kv),
        'dsa_idx_k': rows_last(cache_dsa_idx_k),
        'fox_kv': rows_last(cache_fox_kv),
        'fox_logf_t': rows_last(cache_fox_logf),
    }


def _decode_q(hs, tile0):
    q4 = hs[:, tile0 * LANES:(tile0 + GQA_GROUP) * LANES].reshape(hs.shape[0], GQA_GROUP, LANES)
    low = jnp.arange(LANES) < HEAD_DIM
    return jnp.concatenate([jnp.where(low, q4, 0.0), jnp.where(low, 0.0, q4)], axis=1)


def _decode_y(o):
    low = jnp.arange(LANES) < HEAD_DIM
    return jnp.where(low, o[:, 0:GQA_GROUP], o[:, GQA_GROUP:]).reshape(o.shape[0], MIX_W)


def _sample_layer(x, caches, mem_kv, layer, w, rope, alpha, page_table):
    bsz = x.shape[0]
    l = layer
    hs = project(x, w['w_in'][l], rope[0], rope[1], bsz)
    q_nsa = _decode_q(hs, T_NSAQ)
    o_cmp, sel = nsa_decode_compress(q_nsa, w['pe_k_t'][l], w['pe_v_t'][l], w['w_ck_t'][l], w['w_cv_t'][l],
                                     caches['nsa_kv'], l, page_table)
    o_sel = nsa_decode_select(q_nsa, hs, sel, caches['nsa_kv'], l, page_table)
    y_nsa = nsa_decode_combine(q_nsa, hs, caches['nsa_win'], l, o_cmp, o_sel)
    iq = hs[:, T_IDXQ * LANES:(T_IDXQ + 2) * LANES].reshape(bsz, IDX_HEADS, IDX_DIM)
    iq = jnp.concatenate([iq, jnp.zeros((bsz, N_Q_HEADS - IDX_HEADS, IDX_DIM), F32)], axis=1)
    iw = hs[:, T_MISC * LANES + MISC_IDXW:T_MISC * LANES + MISC_IDXW + IDX_HEADS]
    iw = jnp.concatenate([iw, jnp.zeros((bsz, N_Q_HEADS - IDX_HEADS), F32)], axis=1)
    iw = jnp.broadcast_to(iw[:, :, None], (bsz, N_Q_HEADS, LANES))
    mask, flag = dsa_decode_index(iq, iw, hs, caches['dsa_idx_k'], l, page_table)
    y_dsa = dsa_decode_attention(_decode_q(hs, T_DSAQ), hs, mask, flag, caches['dsa_kv'], l, page_table)
    bias_row = jnp.zeros((1, LANES), F32).at[0, MISC_FOXF:MISC_FOXF + N_Q_HEADS].set(w['fox_b_f'][l])
    y_fox, logf = fox_decode(_decode_q(hs, T_FOXQ), hs, bias_row, caches['fox_kv'], caches['fox_logf_t'], l,
                             page_table)
    y_sb = sb_decode(_decode_q(hs, T_SBQ), caches['sb_kv'], l, page_table)
    ys = [_decode_y(y) for y in (y_sb, y_nsa, y_dsa, y_fox)]
    x = _dense_tail(x, ys, mem_kv, l, w, alpha, bsz, 8, bsz)
    return x, _new_state(hs[:, None, :], logf)


def kernel(x_prompt, x_sample, mem_prompt, cache_sb_kv, cache_nsa_kv, state_nsa_win_kv, cache_dsa_kv,
           cache_dsa_idx_k, cache_fox_kv, cache_fox_logf, cache_mem_kv, page_table,
           ln_in_g, ln_in_b, w_in, fox_b_f, nsa_pe_k, nsa_pe_v, nsa_w_ck, nsa_w_cv, w_br, w_gate, b_gate, w_o,
           ln1_g, ln1_b, w_mq, w_mkv, w_mo, ln2_g, ln2_b, w_rg, b_rg, w_re, b_re, w_eg, w_eu, w_ed, ln3_g, ln3_b):
    depth = w_in.shape[0]
    bsz, seq, d = x_prompt.shape
    dec_b, dec_seq, _ = x_sample.shape
    n_pool, page = cache_sb_kv.shape[1:3]
    past_len = page_table.shape[1] * page
    assert dec_seq == 1 and state_nsa_win_kv.shape[2] == NSA_WINDOW and past_len >= NSA_WINDOW
    alpha = (2.0 * depth) ** 0.25
    w = _prepare_weights(dict(
        w_in=w_in, fox_b_f=fox_b_f, nsa_pe_k=nsa_pe_k, nsa_pe_v=nsa_pe_v, nsa_w_ck=nsa_w_ck, nsa_w_cv=nsa_w_cv,
        w_br=w_br, w_gate=w_gate, b_gate=b_gate, w_o=w_o, ln1_g=ln1_g, ln1_b=ln1_b, w_mq=w_mq, w_mkv=w_mkv,
        w_mo=w_mo, ln2_g=ln2_g, ln2_b=ln2_b, w_rg=w_rg, b_rg=b_rg, w_re=w_re, b_re=b_re, w_eg=w_eg, w_eu=w_eu,
        w_ed=w_ed, ln3_g=ln3_g, ln3_b=ln3_b))

    mem_rows = mem_prompt.reshape(bsz * N_MEM, d)
    mem_kv_p = jnp.stack([matmul(mem_rows, w['w_mkv'][l], 256, 512) for l in range(depth)], axis=0)
    mem_kv_p = mem_kv_p.reshape(depth, bsz, N_MEM, 2 * MEM_W)
    rope_p = _rope_tables(jnp.arange(seq))
    x = layer_norm_rows(x_prompt.reshape(bsz * seq, d), ln_in_g, ln_in_b, 256)
    st_p = []
    for l in range(depth):
        x, new = _prompt_layer(x, mem_kv_p, l, w, rope_p, alpha, bsz)
        st_p.append(new)
    y_prompt = x.reshape(bsz, seq, d)

    caches = _feature_major_caches(cache_sb_kv, cache_nsa_kv, state_nsa_win_kv, cache_dsa_kv, cache_dsa_idx_k,
                                   cache_fox_kv, cache_fox_logf)
    mem_kv_s = cache_mem_kv.reshape(depth, dec_b, N_MEM, 2 * MEM_W)
    rope_s = _rope_tables(jnp.full((dec_b,), past_len))
    x = layer_norm_rows(x_sample.reshape(dec_b, d), ln_in_g, ln_in_b, dec_b)
    st_s = []
    for l in range(depth):
        x, new = _sample_layer(x, caches, mem_kv_s, l, w, rope_s, alpha, page_table)
        new['nsa_win_kv'] = jnp.concatenate([state_nsa_win_kv[l][:, 1:], new['nsa_win_kv']], axis=1)
        st_s.append(new)
    y_sample = x.reshape(dec_b, dec_seq, d)

    def stacked(states, name):
        return jnp.stack([s[name] for s in states], axis=0)

    win_p = stacked(st_p, 'nsa_win_kv')[:, :, seq - min(NSA_WINDOW, seq):]
    return (y_prompt, y_sample,
            stacked(st_p, 'sb_kv'), stacked(st_s, 'sb_kv'),
            stacked(st_p, 'nsa_kv'), stacked(st_s, 'nsa_kv'),
            win_p, stacked(st_s, 'nsa_win_kv'),
            stacked(st_p, 'dsa_kv'), stacked(st_s, 'dsa_kv'),
            stacked(st_p, 'dsa_idx_k'), stacked(st_s, 'dsa_idx_k'),
            stacked(st_p, 'fox_kv'), stacked(st_s, 'fox_kv'),
            stacked(st_p, 'fox_logf'), stacked(st_s, 'fox_logf'),
            mem_kv_p.reshape(depth, bsz, N_MEM, 2, MEM_HEADS, MEM_HEAD_DIM))
```

```python
import functools
import math

import jax
import jax.numpy as jnp
import numpy as np
from jax import lax
from jax.experimental import pallas as pl
from jax.experimental.pallas import tpu as pltpu

F32 = jnp.float32
BF16 = jnp.bfloat16
I32 = jnp.int32

D_MODEL = 2048
HEAD_DIM = 64
N_Q_HEADS = 8
N_KV_HEADS = 2
GQA_GROUP = 4
MIX_W = 512
KV_W = 128
N_BRANCH = 4
ROPE_THETA = 10000.0
LN_EPS = 1e-5
NSA_BLOCK = 64
NSA_TOPK = 16
NSA_WINDOW = 512
IDX_HEADS = 4
IDX_DIM = 64
DSA_TOPK_MAX = 256
N_MEM = 256
MEM_HEADS = 4
MEM_HEAD_DIM = 128
MEM_W = 512
N_GROUPS = 4
EXPERTS_PER_GROUP = 4
N_EXPERTS = 16
D_FF_EXPERT = 256

LANES = 128
NEG = -1e30
F32_EXP_ZERO = -104.0
VMEM_LIMIT = 48 * 1024 * 1024

IN_SPLITS = (
    ('sb_q', MIX_W), ('sb_k', KV_W), ('sb_v', KV_W),
    ('nsa_q', MIX_W), ('nsa_kc', KV_W), ('nsa_vc', KV_W), ('nsa_ks', KV_W), ('nsa_vs', KV_W),
    ('nsa_kw', KV_W), ('nsa_vw', KV_W), ('nsa_g', 3 * N_Q_HEADS),
    ('dsa_q', MIX_W), ('dsa_k', KV_W), ('dsa_v', KV_W),
    ('idx_q', IDX_HEADS * IDX_DIM), ('idx_k', IDX_DIM), ('idx_w', IDX_HEADS),
    ('fox_q', MIX_W), ('fox_k', KV_W), ('fox_v', KV_W), ('fox_f', N_Q_HEADS),
)
N_IN = sum(w for _, w in IN_SPLITS)

T_NSAQ, T_DSAQ, T_IDXQ, T_IDXK, T_NSA_KC, T_NSA_KS, T_NSA_KW, T_DSA_K = 0, 4, 8, 10, 11, 12, 13, 14
N_ROPE_TILES = 16
T_SBQ, T_FOXQ, T_SB_K, T_SB_V, T_NSA_VC, T_NSA_VS, T_NSA_VW, T_DSA_V, T_FOX_K, T_FOX_V, T_MISC = (
    16, 20, 24, 25, 26, 27, 28, 29, 30, 31, 32)
N_TILES = 34
PROJ_W = N_TILES * LANES
MISC_FOXF, MISC_NSAG, MISC_IDXW = 0, 8, 32


def _in_offsets():
    off, out = 0, {}
    for name, width in IN_SPLITS:
        out[name] = off
        off += width
    return out


def _proj_column_perm():
    off = _in_offsets()
    perm = np.full((PROJ_W,), N_IN, np.int32)

    def put(tile, lane, name, start, width):
        base = tile * LANES + lane
        perm[base:base + width] = off[name] + start + np.arange(width)

    def put_q(tile0, name):
        for r in range(GQA_GROUP):
            put(tile0 + r, 0, name, r * HEAD_DIM, HEAD_DIM)
            put(tile0 + r, HEAD_DIM, name, (GQA_GROUP + r) * HEAD_DIM, HEAD_DIM)

    put_q(T_NSAQ, 'nsa_q')
    put_q(T_DSAQ, 'dsa_q')
    put_q(T_SBQ, 'sb_q')
    put_q(T_FOXQ, 'fox_q')
    put(T_IDXQ, 0, 'idx_q', 0, 128)
    put(T_IDXQ + 1, 0, 'idx_q', 128, 128)
    put(T_IDXK, 0, 'idx_k', 0, 64)
    put(T_IDXK, 64, 'idx_k', 0, 64)
    for tile, name in ((T_NSA_KC, 'nsa_kc'), (T_NSA_KS, 'nsa_ks'), (T_NSA_KW, 'nsa_kw'), (T_DSA_K, 'dsa_k'),
                       (T_SB_K, 'sb_k'), (T_SB_V, 'sb_v'), (T_NSA_VC, 'nsa_vc'), (T_NSA_VS, 'nsa_vs'),
                       (T_NSA_VW, 'nsa_vw'), (T_DSA_V, 'dsa_v'), (T_FOX_K, 'fox_k'), (T_FOX_V, 'fox_v')):
        put(tile, 0, name, 0, 128)
    put(T_MISC, MISC_FOXF, 'fox_f', 0, 8)
    put(T_MISC, MISC_NSAG, 'nsa_g', 0, 24)
    put(T_MISC, MISC_IDXW, 'idx_w', 0, 4)
    return perm


def _y_row_perm():
    perm = np.zeros((MIX_W,), np.int32)
    for r in range(GQA_GROUP):
        for half in range(2):
            h = half * GQA_GROUP + r
            perm[r * LANES + half * HEAD_DIM:r * LANES + (half + 1) * HEAD_DIM] = h * HEAD_DIM + np.arange(HEAD_DIM)
    return perm


def _dot(a, b):
    return jnp.dot(a, b, preferred_element_type=F32)


def _dot_nt(a, b):
    return lax.dot_general(a, b, (((1,), (1,)), ((), ())), preferred_element_type=F32)


def _split2(x):
    hi = x.astype(BF16)
    lo = (x - hi.astype(F32)).astype(BF16)
    return hi, lo


def _dot_x_exact(x, m_bf16):
    hi, lo = _split2(x)
    return _dot(hi, m_bf16) + _dot(lo, m_bf16)


def _dot_exact_x(m_bf16, x):
    hi, lo = _split2(x)
    return _dot(m_bf16, hi) + _dot(m_bf16, lo)


def _dot3(a, b):
    ah, al = _split2(a)
    bh, bl = _split2(b)
    return _dot(ah, bh) + (_dot(ah, bl) + _dot(al, bh))


def _dot3_nt(a, b):
    ah, al = _split2(a)
    bh, bl = _split2(b)
    return _dot_nt(ah, bh) + (_dot_nt(ah, bl) + _dot_nt(al, bh))


def _iota(shape, dim):
    return lax.broadcasted_iota(I32, shape, dim)


def _layer_norm(v, g, b):
    mu = jnp.mean(v, axis=-1, keepdims=True)
    c = v - mu
    var = jnp.mean(c * c, axis=-1, keepdims=True)
    return c * lax.rsqrt(var + LN_EPS) * g + b


def _stack_q(q, scale):
    tq = q.shape[0]
    low = _iota((tq, LANES), 1) < HEAD_DIM
    parts = []
    for g in range(N_KV_HEADS):
        for r in range(GQA_GROUP):
            t = q[:, r * LANES:(r + 1) * LANES]
            parts.append(jnp.where(low if g == 0 else jnp.logical_not(low), t * scale, 0.0))
    return jnp.concatenate(parts, axis=0)


def _store_heads(o_ref, o, tq):
    low = _iota((tq, LANES), 1) < HEAD_DIM
    for r in range(GQA_GROUP):
        lo_head = o[r * tq:(r + 1) * tq]
        hi_head = o[(GQA_GROUP + r) * tq:(GQA_GROUP + r + 1) * tq]
        o_ref[0, :, r * LANES:(r + 1) * LANES] = jnp.where(low, lo_head, hi_head)


def _rep_rows(x, n):
    return jnp.concatenate([x] * n, axis=0)


def _params(*sem):
    return pltpu.CompilerParams(dimension_semantics=sem, vmem_limit_bytes=VMEM_LIMIT)


def _ln_kernel(x_ref, g_ref, b_ref, o_ref):
    o_ref[...] = _layer_norm(x_ref[...], g_ref[...], b_ref[...])


def layer_norm_rows(x, g, b, tm):
    n, d = x.shape
    return pl.pallas_call(
        _ln_kernel,
        grid=(n // tm,),
        in_specs=[pl.BlockSpec((tm, d), lambda i: (i, 0)),
                  pl.BlockSpec((1, d), lambda i: (0, 0)),
                  pl.BlockSpec((1, d), lambda i: (0, 0))],
        out_specs=pl.BlockSpec((tm, d), lambda i: (i, 0)),
        out_shape=jax.ShapeDtypeStruct((n, d), F32),
        compiler_params=_params("arbitrary"),
        name="ln_in",
    )(x, g.reshape(1, d), b.reshape(1, d))


def _proj_kernel(x_ref, w_ref, cos_ref, sin_ref, o_ref, xb_ref, *, n_rope_blocks, tn):
    j = pl.program_id(1)

    @pl.when(j == 0)
    def _():
        xb_ref[...] = x_ref[...].astype(BF16)

    acc = _dot(xb_ref[...], w_ref[...])

    @pl.when(j < n_rope_blocks)
    def _():
        cos = cos_ref[...]
        sin = sin_ref[...]
        first = (_iota(cos.shape, 1) & (HEAD_DIM - 1)) < HEAD_DIM // 2
        for s in range(tn // LANES):
            a = acc[:, s * LANES:(s + 1) * LANES]
            rot = jnp.where(first, pltpu.roll(a, LANES - HEAD_DIM // 2, 1), pltpu.roll(a, HEAD_DIM // 2, 1))
            o_ref[:, s * LANES:(s + 1) * LANES] = a * cos + rot * sin

    @pl.when(j >= n_rope_blocks)
    def _():
        o_ref[...] = acc


def project(x, w_perm, cos_t, sin_t, tm, tn=256):
    n, d = x.shape
    n_pos_blocks = cos_t.shape[0] // tm
    return pl.pallas_call(
        functools.partial(_proj_kernel, n_rope_blocks=N_ROPE_TILES * LANES // tn, tn=tn),
        grid=(n // tm, PROJ_W // tn),
        in_specs=[pl.BlockSpec((tm, d), lambda i, j: (i, 0)),
                  pl.BlockSpec((d, tn), lambda i, j: (0, j)),
                  pl.BlockSpec((tm, LANES), lambda i, j: (i % n_pos_blocks, 0)),
                  pl.BlockSpec((tm, LANES), lambda i, j: (i % n_pos_blocks, 0))],
        out_specs=pl.BlockSpec((tm, tn), lambda i, j: (i, j)),
        out_shape=jax.ShapeDtypeStruct((n, PROJ_W), F32),
        scratch_shapes=[pltpu.VMEM((tm, d), BF16)],
        compiler_params=_params("arbitrary", "arbitrary"),
        name="proj",
    )(x, w_perm, cos_t, sin_t)


def _matmul_kernel(x_ref, w_ref, o_ref):
    o_ref[...] = _dot(x_ref[...].astype(BF16), w_ref[...])


def matmul(x, w, tm, tn):
    n, d = x.shape
    m = w.shape[1]
    return pl.pallas_call(
        _matmul_kernel,
        grid=(n // tm, m // tn),
        in_specs=[pl.BlockSpec((tm, d), lambda i, j: (i, 0)),
                  pl.BlockSpec((d, tn), lambda i, j: (0, j))],
        out_specs=pl.BlockSpec((tm, tn), lambda i, j: (i, j)),
        out_shape=jax.ShapeDtypeStruct((n, m), F32),
        compiler_params=_params("arbitrary", "arbitrary"),
        name="matmul",
    )(x, w)


def _merge_kernel(x_ref, ya_ref, yb_ref, yc_ref, yd_ref, wg0, wg1, wg2, wg3, bg0, bg1, bg2, bg3,
                  wb0, wb1, wb2, wb3, o_ref, xb_ref, yb16_ref):
    j = pl.program_id(1)

    @pl.when(j == 0)
    def _():
        xb_ref[...] = x_ref[...].astype(BF16)
        for b, y in enumerate((ya_ref, yb_ref, yc_ref, yd_ref)):
            yb16_ref[b] = y[...].astype(BF16)

    xb = xb_ref[...]
    acc = None
    for b, (wg, bg, wb) in enumerate(((wg0, bg0, wb0), (wg1, bg1, wb1), (wg2, bg2, wb2), (wg3, bg3, wb3))):
        gate = jax.nn.sigmoid(_dot(xb, wg[...]) + bg[...])
        term = gate * _dot(yb16_ref[b], wb[0])
        acc = term if acc is None else acc + term
    o_ref[...] = acc.astype(BF16)


def gated_merge(x, ys, w_gate, b_gate, w_br, tm, tn=256):
    n, d = x.shape
    nj = d // tn
    in_specs = [pl.BlockSpec((tm, d), lambda i, j: (i, 0))]
    in_specs += [pl.BlockSpec((tm, MIX_W), lambda i, j: (i, 0)) for _ in range(N_BRANCH)]
    in_specs += [pl.BlockSpec((d, tn), functools.partial(lambda i, j, b: (0, b * nj + j), b=b)) for b in range(N_BRANCH)]
    in_specs += [pl.BlockSpec((1, tn), functools.partial(lambda i, j, b: (0, b * nj + j), b=b)) for b in range(N_BRANCH)]
    in_specs += [pl.BlockSpec((1, MIX_W, tn), functools.partial(lambda i, j, b: (b, 0, j), b=b)) for b in range(N_BRANCH)]
    return pl.pallas_call(
        _merge_kernel,
        grid=(n // tm, nj),
        in_specs=in_specs,
        out_specs=pl.BlockSpec((tm, tn), lambda i, j: (i, j)),
        out_shape=jax.ShapeDtypeStruct((n, d), BF16),
        scratch_shapes=[pltpu.VMEM((tm, d), BF16), pltpu.VMEM((N_BRANCH, tm, MIX_W), BF16)],
        compiler_params=_params("arbitrary", "arbitrary"),
        name="gated_merge",
    )(x, *ys, *([w_gate] * N_BRANCH), *([b_gate] * N_BRANCH), *([w_br] * N_BRANCH))


def _matmul_res_ln_kernel(a_ref, w_ref, r_ref, g_ref, b_ref, o_ref, *, alpha):
    y = _dot(a_ref[...].astype(BF16), w_ref[...])
    o_ref[...] = _layer_norm(alpha * r_ref[...] + y, g_ref[...], b_ref[...])


def matmul_res_ln(a, w, res, g, b, alpha, tm):
    n, k = a.shape
    d = w.shape[1]
    return pl.pallas_call(
        functools.partial(_matmul_res_ln_kernel, alpha=alpha),
        grid=(n // tm,),
        in_specs=[pl.BlockSpec((tm, k), lambda i: (i, 0)),
                  pl.BlockSpec((k, d), lambda i: (0, 0)),
                  pl.BlockSpec((tm, d), lambda i: (i, 0)),
                  pl.BlockSpec((1, d), lambda i: (0, 0)),
                  pl.BlockSpec((1, d), lambda i: (0, 0))],
        out_specs=pl.BlockSpec((tm, d), lambda i: (i, 0)),
        out_shape=jax.ShapeDtypeStruct((n, d), F32),
        compiler_params=_params("arbitrary"),
        name="matmul_res_ln",
    )(a, w, res, g.reshape(1, d), b.reshape(1, d))


def _mem_kernel(x_ref, wq_ref, k_ref, v_ref, wo_ref, g_ref, b_ref, o_ref, *, alpha):
    x = x_ref[0]
    q = _dot(x.astype(BF16), wq_ref[...])
    outs = []
    for h in range(MEM_HEADS):
        sl = slice(h * MEM_HEAD_DIM, (h + 1) * MEM_HEAD_DIM)
        z = _dot_nt(q[:, sl].astype(BF16), k_ref[0, 0, :, sl].astype(BF16)) * (MEM_HEAD_DIM ** -0.5)
        m = jnp.max(z, axis=-1, keepdims=True)
        e = jnp.exp(z - m)
        p = e / jnp.sum(e, axis=-1, keepdims=True)
        outs.append(_dot(p.astype(BF16), v_ref[0, 0, :, sl].astype(BF16)))
    o = jnp.concatenate(outs, axis=1)
    y = _dot(o.astype(BF16), wo_ref[...])
    o_ref[0] = _layer_norm(alpha * x + y, g_ref[...], b_ref[...])


def memory_block(x, mem_kv, layer, w_mq, w_mo, g, b, alpha, tm):
    bsz, t, d = x.shape
    return pl.pallas_call(
        functools.partial(_mem_kernel, alpha=alpha),
        grid=(bsz, t // tm),
        in_specs=[pl.BlockSpec((1, tm, d), lambda bi, i: (bi, i, 0)),
                  pl.BlockSpec((d, MEM_W), lambda bi, i: (0, 0)),
                  pl.BlockSpec((1, 1, N_MEM, MEM_W), lambda bi, i: (layer, bi, 0, 0)),
                  pl.BlockSpec((1, 1, N_MEM, MEM_W), lambda bi, i: (layer, bi, 0, 1)),
                  pl.BlockSpec((MEM_W, d), lambda bi, i: (0, 0)),
                  pl.BlockSpec((1, d), lambda bi, i: (0, 0)),
                  pl.BlockSpec((1, d), lambda bi, i: (0, 0))],
        out_specs=pl.BlockSpec((1, tm, d), lambda bi, i: (bi, i, 0)),
        out_shape=jax.ShapeDtypeStruct((bsz, t, d), F32),
        compiler_params=_params("arbitrary", "arbitrary"),
        name="memory_block",
    )(x, w_mq, mem_kv, mem_kv, w_mo, g.reshape(1, d), b.reshape(1, d))


def _router_kernel(x_ref, wr_ref, br_ref, o_ref):
    tm = x_ref.shape[0]
    lane = _iota((tm, LANES), 1)
    logits = _dot3(x_ref[...], wr_ref[...]) + br_ref[...]
    is_grp = lane < N_GROUPS
    gl = jnp.where(is_grp, logits, NEG)
    gmax = jnp.max(gl, axis=-1, keepdims=True)
    grp = jnp.min(jnp.where(is_grp & (gl == gmax), lane, LANES), axis=-1, keepdims=True)
    p_grp = 1.0 / jnp.sum(jnp.where(is_grp, jnp.exp(gl - gmax), 0.0), axis=-1, keepdims=True)
    in_grp = (lane >= N_GROUPS) & (lane < N_GROUPS + N_EXPERTS) & (
        ((lane - N_GROUPS) >> 2) == grp)
    el = jnp.where(in_grp, logits, NEG)
    v1 = jnp.max(el, axis=-1, keepdims=True)
    i1 = jnp.min(jnp.where(in_grp & (el == v1), lane, LANES), axis=-1, keepdims=True)
    rest = in_grp & (lane != i1)
    el2 = jnp.where(rest, logits, NEG)
    v2 = jnp.max(el2, axis=-1, keepdims=True)
    i2 = jnp.min(jnp.where(rest & (el2 == v2), lane, LANES), axis=-1, keepdims=True)
    e2 = jnp.exp(v2 - v1)
    w1 = p_grp / (1.0 + e2)
    w2 = p_grp * e2 / (1.0 + e2)
    gate = jnp.where(lane == i1, w1, 0.0) + jnp.where(lane == i2, w2, 0.0)
    o_ref[...] = jnp.where(lane == 0, grp.astype(F32), gate)


def moe_route(x, w_router, b_router, tm):
    n, d = x.shape
    return pl.pallas_call(
        _router_kernel,
        grid=(n // tm,),
        in_specs=[pl.BlockSpec((tm, d), lambda i: (i, 0)),
                  pl.BlockSpec((d, LANES), lambda i: (0, 0)),
                  pl.BlockSpec((1, LANES), lambda i: (0, 0))],
        out_specs=pl.BlockSpec((tm, LANES), lambda i: (i, 0)),
        out_shape=jax.ShapeDtypeStruct((n, LANES), F32),
        compiler_params=_params("arbitrary"),
        name="moe_route",
    )(x, w_router, b_router)


def _moe_kernel(fetch_ref, x_ref, gate_ref, weg_ref, weu_ref, wed_ref, g_ref, b_ref, o_ref, xb_ref, acc_ref,
                *, alpha):
    e = pl.program_id(1)
    tm = x_ref.shape[0]
    lane = _iota((tm, LANES), 1)

    @pl.when(e == 0)
    def _():
        xb_ref[...] = x_ref[...].astype(BF16)
        acc_ref[...] = jnp.zeros_like(acc_ref)

    gate_e = jnp.sum(jnp.where(lane == e + N_GROUPS, gate_ref[...], 0.0), axis=-1, keepdims=True)

    @pl.when(jnp.max(gate_e) > 0.0)
    def _():
        xb = xb_ref[...]
        hid = jax.nn.silu(_dot(xb, weg_ref[0])) * _dot(xb, weu_ref[0])
        acc_ref[...] += _dot((hid * gate_e).astype(BF16), wed_ref[0])

    @pl.when(e == N_EXPERTS - 1)
    def _():
        o_ref[...] = _layer_norm(alpha * x_ref[...] + acc_ref[...], g_ref[...], b_ref[...])


def moe_block(x, w_router, b_router, w_eg, w_eu, w_ed, g, b, alpha, tm):
    n, d = x.shape
    f = w_eg.shape[-1]
    n_tiles = n // tm
    routed = moe_route(x, w_router, b_router, tm)
    sort_tokens = n_tiles > 1
    if sort_tokens:
        order = jnp.argsort(routed[:, 0].astype(I32), stable=True)
        xs = jnp.take(x, order, axis=0, mode="clip")
        gs = jnp.take(routed, order, axis=0, mode="clip")
    else:
        xs, gs = x, routed
    used = jnp.any(gs[:, N_GROUPS:N_GROUPS + N_EXPERTS].reshape(n_tiles, tm, N_EXPERTS) > 0.0, axis=1).reshape(-1)
    steps = jnp.arange(n_tiles * N_EXPERTS, dtype=I32)
    fetch = (lax.cummax(jnp.where(used, steps, 0), axis=0) % N_EXPERTS).astype(I32)
    weight = lambda shape: pl.BlockSpec(shape, lambda i, e, fetch_ref: (fetch_ref[i * N_EXPERTS + e], 0, 0))
    ys = pl.pallas_call(
        functools.partial(_moe_kernel, alpha=alpha),
        grid_spec=pltpu.PrefetchScalarGridSpec(
            num_scalar_prefetch=1,
            grid=(n_tiles, N_EXPERTS),
            in_specs=[pl.BlockSpec((tm, d), lambda i, e, *_: (i, 0)),
                      pl.BlockSpec((tm, LANES), lambda i, e, *_: (i, 0)),
                      weight((1, d, f)), weight((1, d, f)), weight((1, f, d)),
                      pl.BlockSpec((1, d), lambda i, e, *_: (0, 0)),
                      pl.BlockSpec((1, d), lambda i, e, *_: (0, 0))],
            out_specs=pl.BlockSpec((tm, d), lambda i, e, *_: (i, 0)),
            scratch_shapes=[pltpu.VMEM((tm, d), BF16), pltpu.VMEM((tm, d), F32)]),
        out_shape=jax.ShapeDtypeStruct((n, d), F32),
        compiler_params=_params("arbitrary", "arbitrary"),
        name="moe_block",
    )(fetch, xs, gs, w_eg, w_eu, w_ed, g.reshape(1, d), b.reshape(1, d))
    if not sort_tokens:
        return ys
    inverse = jnp.zeros((n,), I32).at[order].set(jnp.arange(n, dtype=I32))
    return jnp.take(ys, inverse, axis=0, mode="clip")


TQ = 128
BK = 128


def _log_sigmoid_pair(z):
    l1p = jnp.log1p(jnp.exp(-jnp.abs(z)))
    return jnp.minimum(z, 0.0) - l1p, jnp.minimum(-z, 0.0) - l1p


def _strict_upper_ones(n):
    return jnp.where(_iota((n, n), 0) > _iota((n, n), 1), 1.0, 0.0).astype(BF16)


def _strict_lower_ones(n):
    return jnp.where(_iota((n, n), 0) < _iota((n, n), 1), 1.0, 0.0).astype(BF16)


def _sb_kernel(q_ref, k_ref, v_ref, o_ref):
    qi = pl.program_id(1)
    r_rows = N_Q_HEADS * TQ
    qs = _stack_q(q_ref[0], HEAD_DIM ** -0.5).astype(BF16)
    upper = _strict_upper_ones(BK)
    strictly_causal = _iota((TQ, BK), 1) < _iota((TQ, BK), 0)

    def step(kb, acc, run, mask):
        ks = pl.multiple_of(kb * BK, BK)
        k = k_ref[0, pl.ds(ks, BK), :].astype(BF16)
        v = v_ref[0, pl.ds(ks, BK), :].astype(BF16)
        z = _dot_nt(qs, k)
        lss, lns = [], []
        for h in range(N_Q_HEADS):
            ls, ln = _log_sigmoid_pair(z[h * TQ:(h + 1) * TQ])
            lss.append(ls)
            lns.append(ln if mask is None else jnp.where(mask, ln, 0.0))
        ln = jnp.concatenate(lns, axis=0)
        between = _dot_x_exact(ln, upper)
        weights = []
        for h in range(N_Q_HEADS):
            sl = slice(h * TQ, (h + 1) * TQ)
            a = jnp.exp(lss[h] + between[sl] + run[sl])
            weights.append((a if mask is None else jnp.where(mask, a, 0.0)).astype(BF16))
        acc = acc + _dot(jnp.concatenate(weights, axis=0), v)
        return acc, run + between[:, 0:1] + ln[:, 0:1]

    acc, run = step(qi, jnp.zeros((r_rows, LANES), F32), jnp.zeros((r_rows, 1), F32), strictly_causal)

    def more(state):
        kb, _, run = state
        return jnp.logical_and(kb >= 0, jnp.max(run) > F32_EXP_ZERO)

    def body(state):
        kb, acc, run = state
        acc, run = step(kb, acc, run, None)
        return kb - 1, acc, run

    _, acc, _ = lax.while_loop(more, body, (qi - 1, acc, run))
    _store_heads(o_ref, acc, TQ)


def _h_spec(rows, tile, width=1):
    if rows is None:
        return lambda t: pl.BlockSpec((1, t, width * LANES), lambda b, i: (b, 0, tile // width))
    return pl.BlockSpec((1, rows, width * LANES), lambda b, i: (b, i, tile // width))


def sb_attention(h):
    bsz, t, _ = h.shape
    return pl.pallas_call(
        _sb_kernel,
        grid=(bsz, t // TQ),
        in_specs=[_h_spec(TQ, T_SBQ, 4), _h_spec(None, T_SB_K)(t), _h_spec(None, T_SB_V)(t)],
        out_specs=pl.BlockSpec((1, TQ, MIX_W), lambda b, i: (b, i, 0)),
        out_shape=jax.ShapeDtypeStruct((bsz, t, MIX_W), F32),
        compiler_params=_params("arbitrary", "arbitrary"),
        name="sb_attention",
    )(h, h, h)


def _fox_prep_kernel(m_ref, bf_ref, logf_ref, c_ref, carry_ref):
    i = pl.program_id(1)
    tc = m_ref.shape[1]

    @pl.when(i == 0)
    def _():
        carry_ref[...] = jnp.zeros_like(carry_ref)

    logf, _ = _log_sigmoid_pair(m_ref[0] + bf_ref[...])
    logf_ref[0] = logf
    incl = jnp.where(_iota((tc, tc), 0) >= _iota((tc, tc), 1), 1.0, 0.0).astype(BF16)
    c = _dot_exact_x(incl, logf) + carry_ref[...]
    c_ref[0] = c
    carry_ref[...] = c[tc - 1:tc, :]


def fox_prep(h, fox_b_f, tc=256):
    bsz, t, _ = h.shape
    bias = jnp.zeros((1, LANES), F32).at[0, MISC_FOXF:MISC_FOXF + N_Q_HEADS].set(fox_b_f)
    return pl.pallas_call(
        _fox_prep_kernel,
        grid=(bsz, t // tc),
        in_specs=[pl.BlockSpec((1, tc, LANES), lambda b, i: (b, i, T_MISC)),
                  pl.BlockSpec((1, LANES), lambda b, i: (0, 0))],
        out_specs=[pl.BlockSpec((1, tc, LANES), lambda b, i: (b, i, 0)),
                   pl.BlockSpec((1, tc, LANES), lambda b, i: (b, i, 0))],
        out_shape=[jax.ShapeDtypeStruct((bsz, t, LANES), F32),
                   jax.ShapeDtypeStruct((bsz, t, LANES), F32)],
        scratch_shapes=[pltpu.VMEM((1, LANES), F32)],
        compiler_params=_params("arbitrary", "arbitrary"),
        name="fox_prep",
    )(h, bias)


def _head_columns(x, lane0, stride=1):
    return jnp.concatenate([x[:, lane0 + stride * h:lane0 + stride * h + 1] for h in range(N_Q_HEADS)], axis=0)


def _softmax_init(rows):
    return (jnp.full((rows, 1), NEG, F32), jnp.zeros((rows, 1), F32), jnp.zeros((rows, LANES), F32))


def _transpose_heads(x):
    if x.shape[0] == LANES:
        return jnp.concatenate([x[:, h * TQ:(h + 1) * TQ].T for h in range(N_Q_HEADS)], axis=0)
    return jnp.concatenate([x[h * TQ:(h + 1) * TQ].T for h in range(N_Q_HEADS)], axis=1)


def _softmax_init_t():
    cols = N_Q_HEADS * TQ
    return (jnp.full((1, cols), NEG, F32), jnp.zeros((1, cols), F32), jnp.zeros((LANES, cols), F32))


LOG2E = 1.4426950408889634


def _softmax_step_t(zt, biases, v, m, l, acc):
    ps, ms, ls, alphas = [], [], [], []
    for h in range(N_Q_HEADS):
        sl = slice(h * TQ, (h + 1) * TQ)
        z = zt[:, sl]
        if biases is not None and biases[h] is not None:
            z = z + biases[h]
        m_new = jnp.maximum(m[:, sl], jnp.max(z, axis=0, keepdims=True))
        p = jnp.exp2(z - m_new)
        alpha = jnp.exp2(m[:, sl] - m_new)
        ls.append(alpha * l[:, sl] + jnp.sum(p, axis=0, keepdims=True))
        ms.append(m_new)
        alphas.append(alpha)
        ps.append(p.astype(BF16))
    cat = lambda xs: jnp.concatenate(xs, axis=1)
    pv = _dot(v.T.astype(BF16), cat(ps))
    return cat(ms), cat(ls), cat(alphas) * acc + pv


def _fox_kernel(q_ref, k_ref, v_ref, c_ref, o_ref):
    qi = pl.program_id(1)
    qst = _transpose_heads(_stack_q(q_ref[0], HEAD_DIM ** -0.5 * LOG2E)).astype(BF16)

    def step(kb, carry, mask_bias):
        ks = pl.multiple_of(kb * BK, BK)
        k = k_ref[0, pl.ds(ks, BK), :].astype(BF16)
        ck = c_ref[0, pl.ds(ks, BK), :] * (-LOG2E)
        biases = [ck[:, h:h + 1] if mask_bias is None else ck[:, h:h + 1] + mask_bias for h in range(N_Q_HEADS)]
        return _softmax_step_t(_dot(k, qst), biases, v_ref[0, pl.ds(ks, BK), :], *carry)

    carry = lax.fori_loop(0, qi, lambda kb, c: step(kb, c, None), _softmax_init_t())
    causal = jnp.where(_iota((BK, TQ), 0) <= _iota((BK, TQ), 1), 0.0, NEG)
    _, l, acc = step(qi, carry, causal)
    _store_heads(o_ref, _transpose_heads(acc / jnp.maximum(l, 1e-30)), TQ)


def fox_attention(h, c_tok):
    bsz, t, _ = h.shape
    return pl.pallas_call(
        _fox_kernel,
        grid=(bsz, t // TQ),
        in_specs=[_h_spec(TQ, T_FOXQ, 4), _h_spec(None, T_FOX_K)(t), _h_spec(None, T_FOX_V)(t),
                  pl.BlockSpec((1, t, LANES), lambda b, i: (b, 0, 0))],
        out_specs=pl.BlockSpec((1, TQ, MIX_W), lambda b, i: (b, i, 0)),
        out_shape=jax.ShapeDtypeStruct((bsz, t, MIX_W), F32),
        compiler_params=_params("arbitrary", "arbitrary"),
        name="fox_attention",
    )(h, h, h, c_tok)


def _nsa_cmp_kernel(k_ref, v_ref, pek_ref, pev_ref, wk_ref, wv_ref, ok_ref, ov_ref, *, nblk):
    for src, pe, w, dst in ((k_ref, pek_ref, wk_ref, ok_ref), (v_ref, pev_ref, wv_ref, ov_ref)):
        rows = src[0].reshape(nblk, NSA_BLOCK, LANES) + pe[...][None]
        mean = jnp.sum(rows, axis=1) * (1.0 / NSA_BLOCK)
        dst[0] = jnp.zeros(dst.shape[1:], F32)
        dst[0, 0:nblk, :] = _dot3(mean, w[...])


def nsa_compress(h, pe_k2, pe_v2, w_ck2, w_cv2):
    bsz, t, _ = h.shape
    nblk = t // NSA_BLOCK
    nblk_pad = -(-nblk // LANES) * LANES
    full = lambda shape: pl.BlockSpec(shape, lambda b: (0,) * len(shape))
    return pl.pallas_call(
        functools.partial(_nsa_cmp_kernel, nblk=nblk),
        grid=(bsz,),
        in_specs=[pl.BlockSpec((1, t, LANES), lambda b: (b, 0, T_NSA_KC)),
                  pl.BlockSpec((1, t, LANES), lambda b: (b, 0, T_NSA_VC)),
                  full((NSA_BLOCK, LANES)), full((NSA_BLOCK, LANES)),
                  full((LANES, LANES)), full((LANES, LANES))],
        out_specs=[pl.BlockSpec((1, nblk_pad, LANES), lambda b: (b, 0, 0))] * 2,
        out_shape=[jax.ShapeDtypeStruct((bsz, nblk_pad, LANES), F32)] * 2,
        compiler_params=_params("arbitrary"),
        name="nsa_compress",
    )(h, h, pe_k2, pe_v2, w_ck2, w_cv2)


def _topk_block_mask(imp_t, nblk, k_sel):
    n_idx = _iota(imp_t.shape, 0)
    rank = jnp.zeros(imp_t.shape, F32)
    for m in range(nblk):
        row = imp_t[m:m + 1, :]
        before = jnp.where(n_idx > m, 1.0, 0.0)
        rank = rank + jnp.where(row > imp_t, 1.0, jnp.where(row == imp_t, before, 0.0))
    return jnp.where(rank < k_sel, jnp.where(imp_t > -jnp.inf, 1.0, 0.0), 0.0)


def _nsa_kernel(q_ref, ck_ref, cv_ref, ks_ref, vs_ref, kw_ref, vw_ref, g_ref, o_ref, *, nblk, k_sel):
    qi = pl.program_id(1)
    r_rows = N_Q_HEADS * TQ
    half = GQA_GROUP * TQ
    qf = _stack_q(q_ref[0], HEAD_DIM ** -0.5)
    qpos = qi * TQ + (_iota((r_rows, BK), 0) & (TQ - 1))
    col = _iota((r_rows, BK), 1)

    n_ctile = ck_ref.shape[1] // LANES
    sel_tiles = []
    state = _softmax_init(r_rows)
    zc_tiles, vis_tiles = [], []
    for c in range(n_ctile):
        blk = c * LANES + col
        vis = ((blk + 1) * NSA_BLOCK - 1 <= qpos) & (blk < nblk)
        zc = _dot3_nt(qf, ck_ref[0, c * LANES:(c + 1) * LANES, :])
        zc_tiles.append(zc)
        vis_tiles.append(vis)
        m_run = jnp.maximum(state[0], jnp.max(jnp.where(vis, zc, NEG), axis=-1, keepdims=True))
        state = (m_run,) + state[1:]
    m_c = state[0]
    e_tiles = [jnp.where(vis, jnp.exp(zc - m_c), 0.0) for zc, vis in zip(zc_tiles, vis_tiles)]
    denom = e_tiles[0].sum(axis=-1, keepdims=True)
    for e in e_tiles[1:]:
        denom = denom + e.sum(axis=-1, keepdims=True)
    denom = jnp.maximum(denom, 1e-30)
    o_cmp = jnp.zeros((r_rows, LANES), F32)
    tq_pos = qi * TQ + _iota((TQ, LANES), 0)
    tq_col = _iota((TQ, LANES), 1)
    for c in range(n_ctile):
        pc = e_tiles[c] / denom
        o_cmp = o_cmp + _dot(pc.astype(BF16), cv_ref[0, c * LANES:(c + 1) * LANES, :].astype(BF16))
        blk = c * LANES + tq_col
        vis_t = ((blk + 1) * NSA_BLOCK - 1 <= tq_pos) & (blk < nblk)
        forced = (blk == (tq_pos >> 6)) | (blk == 0)
        per_group = []
        for g in range(N_KV_HEADS):
            imp = pc[g * half:g * half + TQ]
            for r in range(1, GQA_GROUP):
                imp = imp + pc[g * half + r * TQ:g * half + (r + 1) * TQ]
            imp = jnp.where(forced, jnp.inf, jnp.where(vis_t, imp, -jnp.inf))
            per_group.append(imp)
        sel_tiles.append(per_group)

    sel = []
    for g in range(N_KV_HEADS):
        imp_t = jnp.concatenate([sel_tiles[c][g].T for c in range(n_ctile)], axis=0)[0:nblk]
        chosen = _topk_block_mask(imp_t, nblk, k_sel)
        pad = n_ctile * LANES - nblk
        if pad:
            chosen = jnp.concatenate([chosen, jnp.zeros((pad, TQ), F32)], axis=0)
        sel.append([chosen[c * LANES:(c + 1) * LANES].astype(BF16) for c in range(n_ctile)])

    qst = _transpose_heads(qf * LOG2E).astype(BF16)
    blocks_per_tile = BK // NSA_BLOCK
    e_key = _iota((BK, LANES), 0) >> 6
    e_blk = _iota((BK, LANES), 1)
    kq_key = _iota((BK, TQ), 0)
    kq_pos = qi * TQ + _iota((BK, TQ), 1)

    def sel_body(kb, carry):
        ks = pl.multiple_of(kb * BK, BK)
        k = ks_ref[0, pl.ds(ks, BK), :].astype(BF16)
        causal = (ks + kq_key) <= kq_pos
        masks = []
        for g in range(N_KV_HEADS):
            hit = None
            for c in range(n_ctile):
                expand = jnp.where(e_blk + c * LANES == kb * blocks_per_tile + e_key, 1.0, 0.0).astype(BF16)
                part = _dot(expand, sel[g][c])
                hit = part if hit is None else hit + part
            masks += [jnp.where(jnp.where(causal, hit, 0.0) > 0.5, 0.0, NEG)] * GQA_GROUP
        return _softmax_step_t(_dot(k, qst), masks, vs_ref[0, pl.ds(ks, BK), :], *carry)

    _, l_s, acc_s = lax.fori_loop(0, qi + 1, sel_body, _softmax_init_t())
    o_sel = _transpose_heads(acc_s / jnp.maximum(l_s, 1e-30))

    def win_body(kb, carry):
        ks = pl.multiple_of(kb * BK, BK)
        k = kw_ref[0, pl.ds(ks, BK), :].astype(BF16)
        kpos = ks + kq_key
        band = jnp.where(kpos <= kq_pos, jnp.where(kpos >= kq_pos - NSA_WINDOW, 0.0, NEG), NEG)
        return _softmax_step_t(_dot(k, qst), [band] * N_Q_HEADS, vw_ref[0, pl.ds(ks, BK), :], *carry)

    first = jnp.maximum(qi - NSA_WINDOW // BK, 0)
    _, l_w, acc_w = lax.fori_loop(first, qi + 1, win_body, _softmax_init_t())
    o_win = _transpose_heads(acc_w / jnp.maximum(l_w, 1e-30))

    gates = jax.nn.sigmoid(g_ref[0])
    o = (_head_columns(gates, MISC_NSAG, 3) * o_cmp + _head_columns(gates, MISC_NSAG + 1, 3) * o_sel
         + _head_columns(gates, MISC_NSAG + 2, 3) * o_win)
    _store_heads(o_ref, o, TQ)


def nsa_attention(h, cmp_k, cmp_v):
    bsz, t, _ = h.shape
    nblk = t // NSA_BLOCK
    nblk_pad = cmp_k.shape[1]
    return pl.pallas_call(
        functools.partial(_nsa_kernel, nblk=nblk, k_sel=min(NSA_TOPK, nblk)),
        grid=(bsz, t // TQ),
        in_specs=[_h_spec(TQ, T_NSAQ, 4),
                  pl.BlockSpec((1, nblk_pad, LANES), lambda b, i: (b, 0, 0)),
                  pl.BlockSpec((1, nblk_pad, LANES), lambda b, i: (b, 0, 0)),
                  _h_spec(None, T_NSA_KS)(t), _h_spec(None, T_NSA_VS)(t),
                  _h_spec(None, T_NSA_KW)(t), _h_spec(None, T_NSA_VW)(t),
                  _h_spec(TQ, T_MISC)],
        out_specs=pl.BlockSpec((1, TQ, MIX_W), lambda b, i: (b, i, 0)),
        out_shape=jax.ShapeDtypeStruct((bsz, t, MIX_W), F32),
        compiler_params=_params("arbitrary", "arbitrary"),
        name="nsa_attention",
    )(h, cmp_k, cmp_v, h, h, h, h, h)


def _order_key(x):
    bits = lax.bitcast_convert_type(x + 0.0, I32)
    return bits ^ ((bits >> 31) & 0x7FFFFFFF)


INT_MIN = -2 ** 31
NEG_INF_KEY = -2139095041


def _kth_largest_key(count_ge, shape, k):
    def bit_body(it, kappa):
        bit = lax.shift_left(jnp.int32(1), 31 - it)
        cand = kappa | bit
        cnt = count_ge(cand ^ INT_MIN)
        return jnp.where(cnt >= k, cand, kappa)

    kappa = lax.fori_loop(0, 32, bit_body, jnp.zeros(shape, I32))
    return kappa ^ INT_MIN


def _dsa_kernel(q_ref, iq_ref, ik_ref, w_ref, k_ref, v_ref, o_ref, key_ref, *, topk):
    qi = pl.program_id(1)
    qst = _transpose_heads(_stack_q(q_ref[0], HEAD_DIM ** -0.5 * LOG2E)).astype(BF16)
    iq = iq_ref[0]
    low = _iota((TQ, LANES), 1) < HEAD_DIM
    iq_t = jnp.concatenate(
        [jnp.where(low if h % 2 == 0 else jnp.logical_not(low), iq[:, (h // 2) * LANES:(h // 2 + 1) * LANES], 0.0).T
         for h in range(IDX_HEADS)], axis=1)
    w_t = w_ref[0].T
    key_pos = _iota((BK, TQ), 0)
    q_pos = qi * TQ + _iota((BK, TQ), 1)
    n_tiles = qi + 1
    iq_hi, iq_lo = _split2(iq_t)
    iq_3 = jnp.concatenate([iq_hi, iq_lo, iq_hi], axis=0)

    def score_body(kb, _):
        ks = pl.multiple_of(kb * BK, BK)
        ik_hi, ik_lo = _split2(ik_ref[0, pl.ds(ks, BK), :])
        sc = jnp.maximum(_dot(jnp.concatenate([ik_hi, ik_hi, ik_lo], axis=1), iq_3), 0.0)
        tot = w_t[MISC_IDXW:MISC_IDXW + 1, :] * sc[:, 0:TQ]
        for h in range(1, IDX_HEADS):
            tot = tot + w_t[MISC_IDXW + h:MISC_IDXW + h + 1, :] * sc[:, h * TQ:(h + 1) * TQ]
        tot = jnp.where(ks + key_pos <= q_pos, tot, -jnp.inf)
        key_ref[kb] = _order_key(tot)
        return 0

    lax.fori_loop(0, n_tiles, score_body, 0)

    @pl.when(n_tiles % 2 == 1)
    def _():
        key_ref[n_tiles] = jnp.full((BK, TQ), INT_MIN, I32)

    def count_where(pred):
        def body(kb, acc):
            acc = acc + jnp.where(pred(key_ref[2 * kb]), 1.0, 0.0)
            return acc + jnp.where(pred(key_ref[2 * kb + 1]), 1.0, 0.0)
        total = lax.fori_loop(0, (n_tiles + 1) // 2, body, jnp.zeros((BK, TQ), F32))
        return jnp.sum(total, axis=0, keepdims=True)

    kappa = _kth_largest_key(lambda c: count_where(lambda key: key >= c), (1, TQ), float(topk))
    need = float(topk) - count_where(lambda key: key > kappa)
    earlier = jnp.where(_iota((BK, BK), 1) < _iota((BK, BK), 0), 1.0, 0.0).astype(BF16)

    def attn_body(kb, carry):
        ties_before, m, l, acc = carry
        ks = pl.multiple_of(kb * BK, BK)
        key = key_ref[kb]
        tie = jnp.where(key == kappa, 1.0, 0.0)
        rank = _dot(earlier, tie.astype(BF16)) + ties_before
        chosen = jnp.where(key > kappa, 1.0, jnp.where(rank < need, tie, 0.0))
        dropped = jnp.where(key > NEG_INF_KEY, jnp.where(chosen > 0.5, 0.0, NEG), NEG)
        ties_before = ties_before + jnp.sum(tie, axis=0, keepdims=True)
        k = k_ref[0, pl.ds(ks, BK), :].astype(BF16)
        return (ties_before,) + _softmax_step_t(_dot(k, qst), [dropped] * N_Q_HEADS,
                                                v_ref[0, pl.ds(ks, BK), :], m, l, acc)

    _, _, l, acc = lax.fori_loop(0, n_tiles, attn_body, (jnp.zeros((1, TQ), F32),) + _softmax_init_t())
    _store_heads(o_ref, _transpose_heads(acc / jnp.maximum(l, 1e-30)), TQ)


def dsa_attention(h):
    bsz, t, _ = h.shape
    assert (t // BK) % 2 == 0
    return pl.pallas_call(
        functools.partial(_dsa_kernel, topk=min(DSA_TOPK_MAX, t // 4)),
        grid=(bsz, t // TQ),
        in_specs=[_h_spec(TQ, T_DSAQ, 4), _h_spec(TQ, T_IDXQ, 2), _h_spec(None, T_IDXK)(t),
                  _h_spec(TQ, T_MISC), _h_spec(None, T_DSA_K)(t), _h_spec(None, T_DSA_V)(t)],
        out_specs=pl.BlockSpec((1, TQ, MIX_W), lambda b, i: (b, i, 0)),
        out_shape=jax.ShapeDtypeStruct((bsz, t, MIX_W), F32),
        scratch_shapes=[pltpu.VMEM((t // BK, TQ, BK), I32)],
        compiler_params=_params("arbitrary", "arbitrary"),
        name="dsa_attention",
    )(h, h, h, h, h, h)


PAGES_PER_STEP = 16


def _page_specs(rows, row_block, layer, n_chunks, reverse, page):
    def spec(p):
        def index(b, c, pt_ref, *_):
            chunk = (n_chunks - 1 - c) if reverse else c
            return (layer, pt_ref[b, chunk * PAGES_PER_STEP + p], row_block, 0)
        return index
    return [pl.BlockSpec((1, 1, rows, page), spec(p)) for p in range(PAGES_PER_STEP)]


def _suffix_sums_pages(xs):
    return _dot_x_exact(jnp.concatenate(xs, axis=0), _strict_upper_ones(LANES))


def _softmax_pages(zs, masks, vts, m, l, acc):
    if masks is not None:
        zs = [jnp.where(mk, z, NEG) for z, mk in zip(zs, masks)]
    top = zs[0]
    for z in zs[1:]:
        top = jnp.maximum(top, z)
    m_new = jnp.maximum(m, jnp.max(top, axis=-1, keepdims=True))
    alpha = jnp.exp(m - m_new)
    ps = [jnp.exp(z - m_new) for z in zs]
    if masks is not None:
        ps = [jnp.where(mk, p, 0.0) for p, mk in zip(ps, masks)]
    total = ps[0]
    for p in ps[1:]:
        total = total + p
    pv = _dot_nt(ps[0].astype(BF16), vts[0])
    for p, vt in zip(ps[1:], vts[1:]):
        pv = pv + _dot_nt(p.astype(BF16), vt)
    return m_new, alpha * l + jnp.sum(total, axis=-1, keepdims=True), alpha * acc + pv


def _row_of(ref, b):
    return ref[pl.ds(b, 1), :]


def _diag_column(row, lane0, stride):
    wide = jnp.broadcast_to(row, (N_Q_HEADS, LANES))
    pick = _iota((N_Q_HEADS, LANES), 1) == lane0 + stride * _iota((N_Q_HEADS, LANES), 0)
    return jnp.sum(jnp.where(pick, wide, 0.0), axis=-1, keepdims=True)


def _softmax_token(z, v_row, m, l, acc):
    m_new = jnp.maximum(m, z)
    alpha = jnp.exp(m - m_new)
    p = jnp.exp(z - m_new)
    return m_new, alpha * l + p, alpha * acc + p * v_row


def _sb_dec_kernel(pt_ref, q_ref, *refs):
    pages = refs[:PAGES_PER_STEP]
    o_ref, acc_ref, run_ref = refs[PAGES_PER_STEP:]
    c = pl.program_id(1)

    @pl.when(c == 0)
    def _():
        acc_ref[...] = jnp.zeros_like(acc_ref)
        run_ref[...] = jnp.zeros_like(run_ref)

    qs = (q_ref[0] * HEAD_DIM ** -0.5).astype(BF16)
    lss, lns = [], []
    for p in range(PAGES_PER_STEP):
        ls, ln = _log_sigmoid_pair(_dot(qs, pages[p][0, 0, 0:KV_W, :].astype(BF16)))
        lss.append(ls)
        lns.append(ln)
    within = _suffix_sums_pages(lns)
    later = [within[p * N_Q_HEADS:(p + 1) * N_Q_HEADS] for p in range(PAGES_PER_STEP)]
    totals = [later[p][:, 0:1] + lns[p][:, 0:1] for p in range(PAGES_PER_STEP)]
    acc = acc_ref[...]
    run = run_ref[:, 0:1]
    for p in reversed(range(PAGES_PER_STEP)):
        a = jnp.exp(lss[p] + later[p] + run)
        acc = acc + _dot_nt(a.astype(BF16), pages[p][0, 0, KV_W:2 * KV_W, :].astype(BF16))
        run = run + totals[p]
    acc_ref[...] = acc
    run_ref[...] = jnp.broadcast_to(run, run_ref.shape)

    @pl.when(c == pl.num_programs(1) - 1)
    def _():
        o_ref[0] = acc


def _decode_call(kernel, name, page_table, n_chunks, in_specs, args, out_specs, out_shape, scratch, n_prefetch=1,
                 grid=None, prefetch=None):
    bsz = page_table.shape[0]
    return pl.pallas_call(
        kernel,
        grid_spec=pltpu.PrefetchScalarGridSpec(
            num_scalar_prefetch=n_prefetch,
            grid=grid or (bsz, n_chunks),
            in_specs=in_specs,
            out_specs=out_specs,
            scratch_shapes=scratch),
        out_shape=out_shape,
        compiler_params=_params("arbitrary", "arbitrary"),
        name=name,
    )(*(prefetch or (page_table,)), *args)


def _q_spec():
    return pl.BlockSpec((1, N_Q_HEADS, LANES), lambda b, c, *_: (b, 0, 0))


def _hs_spec(bsz, tile):
    return pl.BlockSpec((bsz, LANES), lambda b, c, *_: (0, tile))


def _head_out(bsz):
    return (pl.BlockSpec((1, N_Q_HEADS, LANES), lambda b, c, *_: (b, 0, 0)),
            jax.ShapeDtypeStruct((bsz, N_Q_HEADS, LANES), F32))


def sb_decode(q, cache, layer, page_table):
    bsz, n_pages = page_table.shape
    n_chunks = n_pages // PAGES_PER_STEP
    page = cache.shape[3]
    out_spec, out_shape = _head_out(bsz)
    return _decode_call(
        _sb_dec_kernel, "sb_decode", page_table, n_chunks,
        [_q_spec()] + _page_specs(2 * KV_W, 0, layer, n_chunks, True, page),
        [q] + [cache] * PAGES_PER_STEP, out_spec, out_shape,
        [pltpu.VMEM((N_Q_HEADS, LANES), F32), pltpu.VMEM((N_Q_HEADS, LANES), F32)])


def _fox_dec_kernel(pt_ref, q_ref, kn_ref, vn_ref, misc_ref, bf_ref, *refs):
    pages = refs[:PAGES_PER_STEP]
    lf_pages = refs[PAGES_PER_STEP:2 * PAGES_PER_STEP]
    o_ref, lf_out_ref, m_ref, l_ref, acc_ref, run_ref = refs[2 * PAGES_PER_STEP:]
    b = pl.program_id(0)
    c = pl.program_id(1)
    qf = q_ref[0] * HEAD_DIM ** -0.5

    @pl.when(c == 0)
    def _():
        logf_row, _ = _log_sigmoid_pair(_row_of(misc_ref, b) + bf_ref[...])
        lf_out_ref[0] = logf_row
        z_new = jnp.sum(qf * _row_of(kn_ref, b), axis=-1, keepdims=True)
        m_ref[...] = jnp.broadcast_to(z_new, m_ref.shape)
        l_ref[...] = jnp.ones_like(l_ref)
        acc_ref[...] = jnp.broadcast_to(_row_of(vn_ref, b), acc_ref.shape)
        run_ref[...] = jnp.broadcast_to(_diag_column(logf_row, MISC_FOXF, 1), run_ref.shape)

    qs = qf.astype(BF16)
    run = run_ref[:, 0:1]
    zs = [None] * PAGES_PER_STEP
    lfs = [lf_pages[p][0, 0] for p in range(PAGES_PER_STEP)]
    within = _suffix_sums_pages(lfs)
    for p in reversed(range(PAGES_PER_STEP)):
        later = within[p * N_Q_HEADS:(p + 1) * N_Q_HEADS]
        zs[p] = _dot(qs, pages[p][0, 0, 0:KV_W, :].astype(BF16)) + (later + run)
        run = run + later[:, 0:1] + lfs[p][:, 0:1]
    vts = [pages[p][0, 0, KV_W:2 * KV_W, :].astype(BF16) for p in range(PAGES_PER_STEP)]
    state = _softmax_pages(zs, None, vts, m_ref[:, 0:1], l_ref[:, 0:1], acc_ref[...])
    m_ref[...] = jnp.broadcast_to(state[0], m_ref.shape)
    l_ref[...] = jnp.broadcast_to(state[1], l_ref.shape)
    acc_ref[...] = state[2]
    run_ref[...] = jnp.broadcast_to(run, run_ref.shape)

    @pl.when(c == pl.num_programs(1) - 1)
    def _():
        o_ref[0] = state[2] / state[1]


def fox_decode(q, hs, bias_row, cache, logf_t, layer, page_table):
    bsz, n_pages = page_table.shape
    n_chunks = n_pages // PAGES_PER_STEP
    page = cache.shape[3]
    out_spec, out_shape = _head_out(bsz)
    stat = pltpu.VMEM((N_Q_HEADS, LANES), F32)
    return _decode_call(
        _fox_dec_kernel, "fox_decode", page_table, n_chunks,
        [_q_spec(), _hs_spec(bsz, T_FOX_K), _hs_spec(bsz, T_FOX_V), _hs_spec(bsz, T_MISC),
         pl.BlockSpec((1, LANES), lambda b, c, *_: (0, 0))]
        + _page_specs(2 * KV_W, 0, layer, n_chunks, True, page)
        + _page_specs(N_Q_HEADS, 0, layer, n_chunks, True, page),
        [q, hs, hs, hs, bias_row] + [cache] * PAGES_PER_STEP + [logf_t] * PAGES_PER_STEP,
        [out_spec, pl.BlockSpec((1, 1, LANES), lambda b, c, *_: (b, 0, 0))],
        [out_shape, jax.ShapeDtypeStruct((bsz, 1, LANES), F32)],
        [stat, stat, stat, stat])


def _nsa_dec_cmp_kernel(pt_ref, q_ref, pek_ref, pev_ref, wk_ref, wv_ref, *refs, n_blocks_past, k_sel):
    pages = refs[:PAGES_PER_STEP]
    ocmp_ref, sel_ref, sk_ref, sv_ref = refs[PAGES_PER_STEP:]
    c = pl.program_id(1)
    page = pages[0].shape[3]
    blocks_per_page = page // NSA_BLOCK
    step_blocks = PAGES_PER_STEP * blocks_per_page
    steps_per_tile = LANES // step_blocks

    @pl.when(c == 0)
    def _():
        sk_ref[...] = jnp.zeros_like(sk_ref)
        sv_ref[...] = jnp.zeros_like(sv_ref)

    lane0 = (c % steps_per_tile) * step_blocks
    tok = _iota((page, LANES), 0)
    lane = _iota((page, LANES), 1)
    sum_k = jnp.zeros((KV_W, LANES), F32)
    sum_v = jnp.zeros((KV_W, LANES), F32)
    for p in range(PAGES_PER_STEP):
        place = jnp.where(lane == lane0 + p * blocks_per_page + (tok >> 6), 1.0, 0.0).astype(BF16)
        sum_k = sum_k + _dot_x_exact(pages[p][0, 0, 0:KV_W, :], place)
        sum_v = sum_v + _dot_x_exact(pages[p][0, 0, KV_W:2 * KV_W, :], place)
    sk_ref[c // steps_per_tile] += sum_k
    sv_ref[c // steps_per_tile] += sum_v

    @pl.when(c == pl.num_programs(1) - 1)
    def _():
        n_ptiles = sk_ref.shape[0]
        nb_pad = n_ptiles * LANES
        n_tiles = n_ptiles + 1
        qpos = n_blocks_past * NSA_BLOCK
        qf = q_ref[0] * HEAD_DIM ** -0.5
        pe_mean_k = jnp.mean(pek_ref[...], axis=-1, keepdims=True)
        pe_mean_v = jnp.mean(pev_ref[...], axis=-1, keepdims=True)
        cvs, zcs = [], []
        for t in range(n_ptiles):
            ck_t = _dot3(wk_ref[...], sk_ref[t] * (1.0 / NSA_BLOCK) + pe_mean_k)
            cvs.append(_dot3(wv_ref[...], sv_ref[t] * (1.0 / NSA_BLOCK) + pe_mean_v))
            zcs.append(_dot3(qf, ck_t))
        zc = zcs[0] if n_ptiles == 1 else jnp.concatenate(zcs, axis=1)
        blk = _iota(zc.shape, 1)
        vis = ((blk + 1) * NSA_BLOCK - 1 <= qpos) & (blk < n_blocks_past)
        m = jnp.max(jnp.where(vis, zc, NEG), axis=-1, keepdims=True)
        e = jnp.where(vis, jnp.exp(zc - m), 0.0)
        pc = e / jnp.maximum(jnp.sum(e, axis=-1, keepdims=True), 1e-30)
        o_cmp = _dot_nt(pc[:, 0:LANES].astype(BF16), cvs[0].astype(BF16))
        for t in range(1, n_ptiles):
            o_cmp = o_cmp + _dot_nt(pc[:, t * LANES:(t + 1) * LANES].astype(BF16), cvs[t].astype(BF16))
        ocmp_ref[0] = o_cmp
        n_blocks = n_blocks_past + 1
        rows = []
        for g in range(N_KV_HEADS):
            imp = jnp.sum(pc[g * GQA_GROUP:(g + 1) * GQA_GROUP], axis=0, keepdims=True)
            imp = jnp.where(vis[0:1], imp, -jnp.inf)
            rows.append(jnp.concatenate([imp, jnp.full((1, LANES), -jnp.inf, F32)], axis=1))
        imp_r = jnp.concatenate(rows + [jnp.full((LANES - N_KV_HEADS, n_tiles * LANES), -jnp.inf, F32)], axis=0)
        blk_r = _iota(imp_r.shape, 1)
        forced = (blk_r == qpos // NSA_BLOCK) | (blk_r == 0)
        imp_r = jnp.where(forced, jnp.inf, imp_r)
        imp_c = jnp.concatenate([imp_r[:, t * LANES:(t + 1) * LANES].T for t in range(n_tiles)], axis=0)
        n_col = _iota((n_tiles * LANES, LANES), 0)
        r_lane = _iota((n_tiles * LANES, LANES), 1)
        sel_ref[0] = jnp.zeros(sel_ref.shape[1:], I32)
        for g in range(N_KV_HEADS):
            col = imp_c[:, g:g + 1]
            rank = jnp.zeros((n_tiles * LANES, 1), F32)
            for t in range(n_tiles):
                row = imp_r[g:g + 1, t * LANES:(t + 1) * LANES]
                m_idx = t * LANES + _iota((n_tiles * LANES, LANES), 1)
                ahead = jnp.where(row > col, 1.0, jnp.where(row == col, jnp.where(m_idx < n_col, 1.0, 0.0), 0.0))
                ahead = jnp.where(m_idx < n_blocks, ahead, 0.0)
                rank = rank + jnp.sum(ahead, axis=-1, keepdims=True)
            hit = jnp.where((rank == r_lane.astype(F32)) & (n_col < n_blocks), 1.0, 0.0)
            idx_row = jnp.sum(hit * n_col.astype(F32), axis=0, keepdims=True)
            ok_row = jnp.sum(jnp.where(col > -jnp.inf, hit, 0.0), axis=0, keepdims=True)
            ok_row = jnp.where(_iota((1, LANES), 1) < k_sel, ok_row, 0.0)
            sel_ref[0, g:g + 1, :] = idx_row.astype(I32)
            sel_ref[0, N_KV_HEADS + g:N_KV_HEADS + g + 1, :] = ok_row.astype(I32)


def nsa_decode_compress(q, pe_k_t, pe_v_t, w_ck_t, w_cv_t, cache, layer, page_table):
    bsz, n_pages = page_table.shape
    n_chunks = n_pages // PAGES_PER_STEP
    page = cache.shape[3]
    n_blocks_past = n_pages * page // NSA_BLOCK
    assert n_blocks_past % LANES == 0 and LANES % (PAGES_PER_STEP * page // NSA_BLOCK) == 0
    k_sel = min(NSA_TOPK, n_blocks_past + 1)
    out_spec, out_shape = _head_out(bsz)
    full = lambda shape: pl.BlockSpec(shape, lambda b, c, *_: (0,) * len(shape))
    sums = pltpu.VMEM((n_blocks_past // LANES, KV_W, LANES), F32)
    return _decode_call(
        functools.partial(_nsa_dec_cmp_kernel, n_blocks_past=n_blocks_past, k_sel=k_sel),
        "nsa_decode_compress", page_table, n_chunks,
        [_q_spec(), full((LANES, NSA_BLOCK)), full((LANES, NSA_BLOCK)), full((LANES, LANES)), full((LANES, LANES))]
        + _page_specs(2 * KV_W, 0, layer, n_chunks, False, page),
        [q, pe_k_t, pe_v_t, w_ck_t, w_cv_t] + [cache] * PAGES_PER_STEP,
        [out_spec, pl.BlockSpec((1, N_Q_HEADS, LANES), lambda b, c, *_: (b, 0, 0))],
        [out_shape, jax.ShapeDtypeStruct((bsz, N_Q_HEADS, LANES), I32)],
        [sums, sums])


SELECTED_PER_STEP = 8


def _nsa_dec_sel_kernel(pt_ref, blk_ref, ok_ref, q_ref, kn_ref, vn_ref, *refs, n_blocks_past, k_sel):
    pages = refs[:SELECTED_PER_STEP]
    o_ref, m_ref, l_ref, acc_ref = refs[SELECTED_PER_STEP:]
    b = pl.program_id(0)
    s = pl.program_id(1)
    qf = q_ref[0] * HEAD_DIM ** -0.5

    @pl.when(s == 0)
    def _():
        m_ref[...] = jnp.full(m_ref.shape, NEG, F32)
        l_ref[...] = jnp.zeros_like(l_ref)
        acc_ref[...] = jnp.zeros_like(acc_ref)

    qs = qf.astype(BF16)
    blocks_per_page = pages[0].shape[3] // NSA_BLOCK
    zs, masks, vts = [], [], []
    for j in range(SELECTED_PER_STEP):
        i = s * SELECTED_PER_STEP + j
        blk = blk_ref[b, i]
        valid = (ok_ref[b, i] > 0) & (blk < n_blocks_past)
        z = _dot(qs, pages[j][0, 0, 0:KV_W, :].astype(BF16))
        in_block = (_iota(z.shape, 1) >> 6) == blk % blocks_per_page
        masks.append(jnp.where(in_block, _iota(z.shape, 0) >> 2, -2) == jnp.where(valid, i // k_sel, -1))
        zs.append(z)
        vts.append(pages[j][0, 0, KV_W:2 * KV_W, :].astype(BF16))
    state = _softmax_pages(zs, masks, vts, m_ref[:, 0:1], l_ref[:, 0:1], acc_ref[...])

    @pl.when(s < pl.num_programs(1) - 1)
    def _():
        m_ref[...] = jnp.broadcast_to(state[0], m_ref.shape)
        l_ref[...] = jnp.broadcast_to(state[1], l_ref.shape)
        acc_ref[...] = state[2]

    @pl.when(s == pl.num_programs(1) - 1)
    def _():
        z_new = jnp.sum(qf * _row_of(kn_ref, b), axis=-1, keepdims=True)
        _, l, acc = _softmax_token(z_new, _row_of(vn_ref, b), *state)
        o_ref[0] = acc / l


def nsa_decode_select(q, hs, sel, cache, layer, page_table):
    bsz, n_pages = page_table.shape
    page_len = cache.shape[3]
    n_blocks_past = n_pages * page_len // NSA_BLOCK
    k_sel = min(NSA_TOPK, n_blocks_past + 1)
    blocks_per_page = page_len // NSA_BLOCK
    blk = sel[:, 0:N_KV_HEADS, 0:k_sel].reshape(bsz, N_KV_HEADS * k_sel)
    ok = sel[:, N_KV_HEADS:2 * N_KV_HEADS, 0:k_sel].reshape(bsz, N_KV_HEADS * k_sel)
    out_spec, out_shape = _head_out(bsz)

    assert (N_KV_HEADS * k_sel) % SELECTED_PER_STEP == 0

    def page_index(j):
        def index(b, s, pt_ref, blk_ref, ok_ref):
            page = jnp.minimum(blk_ref[b, s * SELECTED_PER_STEP + j] // blocks_per_page, n_pages - 1)
            return (layer, pt_ref[b, page], 1, 0)
        return index

    stat = pltpu.VMEM((N_Q_HEADS, LANES), F32)
    return _decode_call(
        functools.partial(_nsa_dec_sel_kernel, n_blocks_past=n_blocks_past, k_sel=k_sel),
        "nsa_decode_select", page_table, None,
        [_q_spec(), _hs_spec(bsz, T_NSA_KS), _hs_spec(bsz, T_NSA_VS)]
        + [pl.BlockSpec((1, 1, 2 * KV_W, page_len), page_index(j)) for j in range(SELECTED_PER_STEP)],
        [q, hs, hs] + [cache] * SELECTED_PER_STEP, out_spec, out_shape, [stat, stat, stat],
        n_prefetch=3, grid=(bsz, N_KV_HEADS * k_sel // SELECTED_PER_STEP), prefetch=(page_table, blk, ok))


def _nsa_dec_win_kernel(q_ref, kn_ref, vn_ref, misc_ref, win_ref, ocmp_ref, osel_ref, o_ref):
    b = pl.program_id(0)
    qf = q_ref[0] * HEAD_DIM ** -0.5
    z = _dot(qf.astype(BF16), win_ref[0, 0, 0:KV_W, :].astype(BF16))
    state = _softmax_pages([z], None, [win_ref[0, 0, KV_W:2 * KV_W, :].astype(BF16)], *_softmax_init(N_Q_HEADS))
    z_new = jnp.sum(qf * _row_of(kn_ref, b), axis=-1, keepdims=True)
    _, l, acc = _softmax_token(z_new, _row_of(vn_ref, b), *state)
    gates = jax.nn.sigmoid(_row_of(misc_ref, b))
    o_ref[0] = (_diag_column(gates, MISC_NSAG, 3) * ocmp_ref[0] + _diag_column(gates, MISC_NSAG + 1, 3) * osel_ref[0]
                + _diag_column(gates, MISC_NSAG + 2, 3) * (acc / l))


def nsa_decode_combine(q, hs, win_state, layer, o_cmp, o_sel):
    bsz = q.shape[0]
    w_rows = win_state.shape[3]
    head = pl.BlockSpec((1, N_Q_HEADS, LANES), lambda b: (b, 0, 0))
    tile = lambda t: pl.BlockSpec((bsz, LANES), lambda b: (0, t))
    return pl.pallas_call(
        _nsa_dec_win_kernel,
        grid=(bsz,),
        in_specs=[head, tile(T_NSA_KW), tile(T_NSA_VW), tile(T_MISC),
                  pl.BlockSpec((1, 1, 2 * KV_W, w_rows), lambda b: (layer, b, 0, 0)), head, head],
        out_specs=head,
        out_shape=jax.ShapeDtypeStruct((bsz, N_Q_HEADS, LANES), F32),
        compiler_params=_params("arbitrary"),
        name="nsa_decode_combine",
    )(q, hs, hs, hs, win_state, o_cmp, o_sel)


def _dsa_dec_index_kernel(pt_ref, iq_ref, w_ref, kn_ref, *refs, topk):
    pages = refs[:PAGES_PER_STEP]
    mask_ref, flag_ref, key_ref = refs[PAGES_PER_STEP:]
    b = pl.program_id(0)
    c = pl.program_id(1)
    iq = iq_ref[0]
    w = w_ref[0]
    for p in range(PAGES_PER_STEP):
        sc = jnp.maximum(_dot3(iq, pages[p][0, 0]), 0.0)
        tot = jnp.sum(w * sc, axis=0, keepdims=True)
        key_ref[pl.ds(c * PAGES_PER_STEP + p, 1), :] = _order_key(tot)

    @pl.when(c == pl.num_programs(1) - 1)
    def _():
        k_new = _row_of(kn_ref, b)[:, 0:IDX_DIM]
        sc_new = jnp.maximum(jnp.sum(iq * k_new, axis=-1, keepdims=True), 0.0)
        key_new = _order_key(jnp.sum(w[:, 0:1] * sc_new, axis=0, keepdims=True))
        keys = key_ref[...]

        def total(x):
            return jnp.sum(jnp.sum(x, axis=-1, keepdims=True), axis=0, keepdims=True)

        def count(pred):
            return total(jnp.where(pred(keys), 1.0, 0.0)) + jnp.where(pred(key_new), 1.0, 0.0)

        k = float(topk)
        kappa = _kth_largest_key(lambda cand: count(lambda x: x >= cand), (1, 1), k)
        need = k - count(lambda x: x > kappa)
        tie = jnp.where(keys == kappa, 1.0, 0.0)
        n_rows = keys.shape[0]
        in_row = _dot(tie.astype(BF16), _strict_lower_ones(LANES))
        row_tot = jnp.broadcast_to(jnp.sum(tie, axis=-1, keepdims=True), (n_rows, LANES))
        rows_before = jnp.where(_iota((n_rows, n_rows), 1) < _iota((n_rows, n_rows), 0), 1.0, 0.0).astype(BF16)
        rank = in_row + _dot(rows_before, row_tot.astype(BF16))
        chosen = jnp.where(keys > kappa, 1.0, jnp.where(rank < need, tie, 0.0))
        mask_ref[0] = jnp.where(keys > NEG_INF_KEY, chosen, 0.0)
        new_in = jnp.where(key_new > kappa, 1.0, jnp.where((key_new == kappa) & (total(tie) < need), 1.0, 0.0))
        flag_ref[0] = jnp.broadcast_to(new_in, flag_ref.shape[1:])


def dsa_decode_index(iq, w_rows, hs, cache, layer, page_table):
    bsz, n_pages = page_table.shape
    n_chunks = n_pages // PAGES_PER_STEP
    rows = cache.shape[3]
    assert rows == LANES
    topk = min(DSA_TOPK_MAX, (n_pages * rows + 1) // 4)
    return _decode_call(
        functools.partial(_dsa_dec_index_kernel, topk=topk), "dsa_decode_index", page_table, n_chunks,
        [pl.BlockSpec((1, N_Q_HEADS, IDX_DIM), lambda b, c, *_: (b, 0, 0)), _q_spec(), _hs_spec(bsz, T_IDXK)]
        + _page_specs(IDX_DIM, 0, layer, n_chunks, False, rows),
        [iq, w_rows, hs] + [cache] * PAGES_PER_STEP,
        [pl.BlockSpec((1, n_pages, rows), lambda b, c, *_: (b, 0, 0)),
         pl.BlockSpec((1, N_Q_HEADS, LANES), lambda b, c, *_: (b, 0, 0))],
        [jax.ShapeDtypeStruct((bsz, n_pages, rows), F32), jax.ShapeDtypeStruct((bsz, N_Q_HEADS, LANES), F32)],
        [pltpu.VMEM((n_pages, rows), I32)])


def _dsa_dec_attn_kernel(pt_ref, q_ref, kn_ref, vn_ref, mask_ref, flag_ref, *refs):
    pages = refs[:PAGES_PER_STEP]
    o_ref, m_ref, l_ref, acc_ref = refs[PAGES_PER_STEP:]
    b = pl.program_id(0)
    c = pl.program_id(1)
    qf = q_ref[0] * HEAD_DIM ** -0.5

    @pl.when(c == 0)
    def _():
        m_ref[...] = jnp.full(m_ref.shape, NEG, F32)
        l_ref[...] = jnp.zeros_like(l_ref)
        acc_ref[...] = jnp.zeros_like(acc_ref)

    qs = qf.astype(BF16)
    zs = [_dot(qs, pages[p][0, 0, 0:KV_W, :].astype(BF16)) for p in range(PAGES_PER_STEP)]
    masks = [jnp.broadcast_to(mask_ref[0, p:p + 1, :], (N_Q_HEADS, LANES)) > 0.5 for p in range(PAGES_PER_STEP)]
    vts = [pages[p][0, 0, KV_W:2 * KV_W, :].astype(BF16) for p in range(PAGES_PER_STEP)]
    state = _softmax_pages(zs, masks, vts, m_ref[:, 0:1], l_ref[:, 0:1], acc_ref[...])
    m_ref[...] = jnp.broadcast_to(state[0], m_ref.shape)
    l_ref[...] = jnp.broadcast_to(state[1], l_ref.shape)
    acc_ref[...] = state[2]

    @pl.when(c == pl.num_programs(1) - 1)
    def _():
        z_new = jnp.sum(qf * _row_of(kn_ref, b), axis=-1, keepdims=True)
        z_new = jnp.where(flag_ref[0][:, 0:1] > 0.5, z_new, NEG)
        m, l, acc = state
        m_new = jnp.maximum(m, z_new)
        alpha = jnp.exp(m - m_new)
        p_new = jnp.where(flag_ref[0][:, 0:1] > 0.5, jnp.exp(z_new - m_new), 0.0)
        l = alpha * l + p_new
        acc = alpha * acc + p_new * _row_of(vn_ref, b)
        o_ref[0] = acc / jnp.maximum(l, 1e-30)


def dsa_decode_attention(q, hs, mask, flag, cache, layer, page_table):
    bsz, n_pages = page_table.shape
    n_chunks = n_pages // PAGES_PER_STEP
    rows = cache.shape[3]
    out_spec, out_shape = _head_out(bsz)
    stat = pltpu.VMEM((N_Q_HEADS, LANES), F32)
    return _decode_call(
        _dsa_dec_attn_kernel, "dsa_decode_attention", page_table, n_chunks,
        [_q_spec(), _hs_spec(bsz, T_DSA_K), _hs_spec(bsz, T_DSA_V),
         pl.BlockSpec((1, PAGES_PER_STEP, rows), lambda b, c, *_: (b, c, 0)),
         pl.BlockSpec((1, N_Q_HEADS, LANES), lambda b, c, *_: (b, 0, 0))]
        + _page_specs(2 * KV_W, 0, layer, n_chunks, False, rows),
        [q, hs, hs, mask, flag] + [cache] * PAGES_PER_STEP, out_spec, out_shape, [stat, stat, stat])


def _rope_tables(pos):
    half = HEAD_DIM // 2
    freq = ROPE_THETA ** (-jnp.arange(half, dtype=F32) / half)
    ang = pos.astype(F32)[:, None] * freq[None, :]
    cos = jnp.cos(ang)
    sin = jnp.sin(ang)
    return jnp.tile(cos, (1, 4)), jnp.tile(jnp.concatenate([-sin, sin], axis=1), (1, 2))


def _prepare_weights(p):
    depth = p['w_in'].shape[0]
    w_in = jnp.concatenate([p['w_in'], jnp.zeros((depth, D_MODEL, 1), F32)], axis=-1)
    eye2 = jnp.eye(N_KV_HEADS, dtype=F32)
    w = {
        'w_in': jnp.take(w_in, jnp.asarray(_proj_column_perm()), axis=-1, mode="clip").astype(BF16),
        'w_gate': p['w_gate'].astype(BF16),
        'b_gate': p['b_gate'].reshape(depth, 1, N_BRANCH * D_MODEL),
        'w_br': jnp.take(p['w_br'], jnp.asarray(_y_row_perm()), axis=2, mode="clip").astype(BF16),
        'w_o': p['w_o'].astype(BF16),
        'w_mq': p['w_mq'].astype(BF16),
        'w_mo': p['w_mo'].astype(BF16),
        'w_mkv': p['w_mkv'].astype(BF16),
        'w_router': jnp.concatenate(
            [p['w_rg'], p['w_re'], jnp.zeros((depth, D_MODEL, LANES - N_GROUPS - N_EXPERTS), F32)], axis=-1),
        'b_router': jnp.concatenate(
            [p['b_rg'], p['b_re'], jnp.zeros((depth, LANES - N_GROUPS - N_EXPERTS), F32)], axis=-1)[:, None, :],
        'w_eg': p['w_eg'].astype(BF16),
        'w_eu': p['w_eu'].astype(BF16),
        'w_ed': p['w_ed'].astype(BF16),
        'pe_k': jnp.tile(p['nsa_pe_k'], (1, 1, N_KV_HEADS)),
        'pe_v': jnp.tile(p['nsa_pe_v'], (1, 1, N_KV_HEADS)),
        'w_ck': jnp.einsum('gh,lde->lgdhe', eye2, p['nsa_w_ck']).reshape(depth, LANES, LANES),
        'w_cv': jnp.einsum('gh,lde->lgdhe', eye2, p['nsa_w_cv']).reshape(depth, LANES, LANES),
    }
    for name in ('pe_k', 'pe_v', 'w_ck', 'w_cv'):
        w[name + '_t'] = jnp.swapaxes(w[name], 1, 2)
    for name in ('fox_b_f', 'ln1_g', 'ln1_b', 'ln2_g', 'ln2_b', 'ln3_g', 'ln3_b'):
        w[name] = p[name]
    return w


def _tiles(h, tiles):
    parts = [h[:, :, t * LANES:(t + 1) * LANES] for t in tiles]
    return jnp.stack(parts, axis=2).reshape(h.shape[0], h.shape[1], len(tiles), N_KV_HEADS, HEAD_DIM)


def _new_state(h, logf):
    return {
        'sb_kv': _tiles(h, (T_SB_K, T_SB_V)),
        'nsa_kv': _tiles(h, (T_NSA_KC, T_NSA_VC, T_NSA_KS, T_NSA_VS)),
        'nsa_win_kv': _tiles(h, (T_NSA_KW, T_NSA_VW)),
        'dsa_kv': _tiles(h, (T_DSA_K, T_DSA_V)),
        'dsa_idx_k': h[:, :, T_IDXK * LANES:T_IDXK * LANES + IDX_DIM],
        'fox_kv': _tiles(h, (T_FOX_K, T_FOX_V)),
        'fox_logf': logf[:, :, MISC_FOXF:MISC_FOXF + N_Q_HEADS],
    }


def _dense_tail(x, ys, mem_kv, layer, w, alpha, tm, tm_mem, bsz):
    n = x.shape[0]
    l = layer
    merged = gated_merge(x, ys, w['w_gate'][l], w['b_gate'][l], w['w_br'][l], tm)
    x = matmul_res_ln(merged, w['w_o'][l], x, w['ln1_g'][l], w['ln1_b'][l], alpha, min(tm, 256))
    t = n // bsz
    xm = x.reshape(bsz, t, D_MODEL)
    if t < tm_mem:
        xm = jnp.broadcast_to(xm[:, 0:1], (bsz, tm_mem, D_MODEL))
    x = memory_block(xm, mem_kv, l, w['w_mq'][l], w['w_mo'][l], w['ln2_g'][l], w['ln2_b'][l], alpha,
                     tm_mem)[:, 0:t].reshape(n, D_MODEL)
    return moe_block(x, w['w_router'][l], w['b_router'][l], w['w_eg'][l], w['w_eu'][l], w['w_ed'][l],
                     w['ln3_g'][l], w['ln3_b'][l], alpha, tm)


def _prompt_layer(x, mem_kv, layer, w, rope, alpha, bsz):
    n = x.shape[0]
    t = n // bsz
    l = layer
    tm = min(512, t)
    h = project(x, w['w_in'][l], rope[0], rope[1], min(1024, t)).reshape(bsz, t, PROJ_W)
    logf, c_tok = fox_prep(h, w['fox_b_f'][l])
    cmp_k, cmp_v = nsa_compress(h, w['pe_k'][l], w['pe_v'][l], w['w_ck'][l], w['w_cv'][l])
    ys = [sb_attention(h), nsa_attention(h, cmp_k, cmp_v), dsa_attention(h), fox_attention(h, c_tok)]
    ys = [y.reshape(n, MIX_W) for y in ys]
    x = _dense_tail(x, ys, mem_kv, l, w, alpha, tm, min(256, t), bsz)
    return x, _new_state(h, logf)


def _feature_major_caches(cache_sb_kv, cache_nsa_kv, state_nsa_win_kv, cache_dsa_kv, cache_dsa_idx_k,
                          cache_fox_kv, cache_fox_logf):
    def rows_last(c):
        t = jnp.moveaxis(c, 2, -1)
        return t.reshape(t.shape[0], t.shape[1], -1, t.shape[-1])
    return {
        'sb_kv': rows_last(cache_sb_kv),
        'nsa_kv': rows_last(cache_nsa_kv),
        'nsa_win': rows_last(state_nsa_win_kv),
        'dsa_kv': rows_last(cache_dsa_kv),
        'dsa_idx_k': rows_last(cache_dsa_idx_k),
        'fox_kv': rows_last(cache_fox_kv),
        'fox_logf_t': rows_last(cache_fox_logf),
    }


def _decode_q(hs, tile0):
    q4 = hs[:, tile0 * LANES:(tile0 + GQA_GROUP) * LANES].reshape(hs.shape[0], GQA_GROUP, LANES)
    low = jnp.arange(LANES) < HEAD_DIM
    return jnp.concatenate([jnp.where(low, q4, 0.0), jnp.where(low, 0.0, q4)], axis=1)


def _decode_y(o):
    low = jnp.arange(LANES) < HEAD_DIM
    return jnp.where(low, o[:, 0:GQA_GROUP], o[:, GQA_GROUP:]).reshape(o.shape[0], MIX_W)


def _sample_layer(x, caches, mem_kv, layer, w, rope, alpha, page_table):
    bsz = x.shape[0]
    l = layer
    hs = project(x, w['w_in'][l], rope[0], rope[1], bsz)
    q_nsa = _decode_q(hs, T_NSAQ)
    o_cmp, sel = nsa_decode_compress(q_nsa, w['pe_k_t'][l], w['pe_v_t'][l], w['w_ck_t'][l], w['w_cv_t'][l],
                                     caches['nsa_kv'], l, page_table)
    o_sel = nsa_decode_select(q_nsa, hs, sel, caches['nsa_kv'], l, page_table)
    y_nsa = nsa_decode_combine(q_nsa, hs, caches['nsa_win'], l, o_cmp, o_sel)
    iq = hs[:, T_IDXQ * LANES:(T_IDXQ + 2) * LANES].reshape(bsz, IDX_HEADS, IDX_DIM)
    iq = jnp.concatenate([iq, jnp.zeros((bsz, N_Q_HEADS - IDX_HEADS, IDX_DIM), F32)], axis=1)
    iw = hs[:, T_MISC * LANES + MISC_IDXW:T_MISC * LANES + MISC_IDXW + IDX_HEADS]
    iw = jnp.concatenate([iw, jnp.zeros((bsz, N_Q_HEADS - IDX_HEADS), F32)], axis=1)
    iw = jnp.broadcast_to(iw[:, :, None], (bsz, N_Q_HEADS, LANES))
    mask, flag = dsa_decode_index(iq, iw, hs, caches['dsa_idx_k'], l, page_table)
    y_dsa = dsa_decode_attention(_decode_q(hs, T_DSAQ), hs, mask, flag, caches['dsa_kv'], l, page_table)
    bias_row = jnp.zeros((1, LANES), F32).at[0, MISC_FOXF:MISC_FOXF + N_Q_HEADS].set(w['fox_b_f'][l])
    y_fox, logf = fox_decode(_decode_q(hs, T_FOXQ), hs, bias_row, caches['fox_kv'], caches['fox_logf_t'], l,
                             page_table)
    y_sb = sb_decode(_decode_q(hs, T_SBQ), caches['sb_kv'], l, page_table)
    ys = [_decode_y(y) for y in (y_sb, y_nsa, y_dsa, y_fox)]
    x = _dense_tail(x, ys, mem_kv, l, w, alpha, bsz, 8, bsz)
    return x, _new_state(hs[:, None, :], logf)


def kernel(x_prompt, x_sample, mem_prompt, cache_sb_kv, cache_nsa_kv, state_nsa_win_kv, cache_dsa_kv,
           cache_dsa_idx_k, cache_fox_kv, cache_fox_logf, cache_mem_kv, page_table,
           ln_in_g, ln_in_b, w_in, fox_b_f, nsa_pe_k, nsa_pe_v, nsa_w_ck, nsa_w_cv, w_br, w_gate, b_gate, w_o,
           ln1_g, ln1_b, w_mq, w_mkv, w_mo, ln2_g, ln2_b, w_rg, b_rg, w_re, b_re, w_eg, w_eu, w_ed, ln3_g, ln3_b):
    depth = w_in.shape[0]
    bsz, seq, d = x_prompt.shape
    dec_b, dec_seq, _ = x_sample.shape
    n_pool, page = cache_sb_kv.shape[1:3]
    past_len = page_table.shape[1] * page
    assert dec_seq == 1 and state_nsa_win_kv.shape[2] == NSA_WINDOW and past_len >= NSA_WINDOW
    alpha = (2.0 * depth) ** 0.25
    w = _prepare_weights(dict(
        w_in=w_in, fox_b_f=fox_b_f, nsa_pe_k=nsa_pe_k, nsa_pe_v=nsa_pe_v, nsa_w_ck=nsa_w_ck, nsa_w_cv=nsa_w_cv,
        w_br=w_br, w_gate=w_gate, b_gate=b_gate, w_o=w_o, ln1_g=ln1_g, ln1_b=ln1_b, w_mq=w_mq, w_mkv=w_mkv,
        w_mo=w_mo, ln2_g=ln2_g, ln2_b=ln2_b, w_rg=w_rg, b_rg=b_rg, w_re=w_re, b_re=b_re, w_eg=w_eg, w_eu=w_eu,
        w_ed=w_ed, ln3_g=ln3_g, ln3_b=ln3_b))

    mem_rows = mem_prompt.reshape(bsz * N_MEM, d)
    mem_kv_p = jnp.stack([matmul(mem_rows, w['w_mkv'][l], 256, 512) for l in range(depth)], axis=0)
    mem_kv_p = mem_kv_p.reshape(depth, bsz, N_MEM, 2 * MEM_W)
    rope_p = _rope_tables(jnp.arange(seq))
    x = layer_norm_rows(x_prompt.reshape(bsz * seq, d), ln_in_g, ln_in_b, 256)
    st_p = []
    for l in range(depth):
        x, new = _prompt_layer(x, mem_kv_p, l, w, rope_p, alpha, bsz)
        st_p.append(new)
    y_prompt = x.reshape(bsz, seq, d)

    caches = _feature_major_caches(cache_sb_kv, cache_nsa_kv, state_nsa_win_kv, cache_dsa_kv, cache_dsa_idx_k,
                                   cache_fox_kv, cache_fox_logf)
    mem_kv_s = cache_mem_kv.reshape(depth, dec_b, N_MEM, 2 * MEM_W)
    rope_s = _rope_tables(jnp.full((dec_b,), past_len))
    x = layer_norm_rows(x_sample.reshape(dec_b, d), ln_in_g, ln_in_b, dec_b)
    st_s = []
    for l in range(depth):
        x, new = _sample_layer(x, caches, mem_kv_s, l, w, rope_s, alpha, page_table)
        new['nsa_win_kv'] = jnp.concatenate([state_nsa_win_kv[l][:, 1:], new['nsa_win_kv']], axis=1)
        st_s.append(new)
    y_sample = x.reshape(dec_b, dec_seq, d)

    def stacked(states, name):
        return jnp.stack([s[name] for s in states], axis=0)

    win_p = stacked(st_p, 'nsa_win_kv')[:, :, seq - min(NSA_WINDOW, seq):]
    return (y_prompt, y_sample,
            stacked(st_p, 'sb_kv'), stacked(st_s, 'sb_kv'),
            stacked(st_p, 'nsa_kv'), stacked(st_s, 'nsa_kv'),
            win_p, stacked(st_s, 'nsa_win_kv'),
            stacked(st_p, 'dsa_kv'), stacked(st_s, 'dsa_kv'),
            stacked(st_p, 'dsa_idx_k'), stacked(st_s, 'dsa_idx_k'),
            stacked(st_p, 'fox_kv'), stacked(st_s, 'fox_kv'),
            stacked(st_p, 'fox_logf'), stacked(st_s, 'fox_logf'),
            mem_kv_p.reshape(depth, bsz, N_MEM, 2, MEM_HEADS, MEM_HEAD_DIM))
```
